```python
import math
import jax, jax.numpy as jnp
from jax import lax
import numpy as np

D_MODEL = 2048
BATCH = 1
SEQ = 8192
DEPTH = 2

CHUNK = 64
D_MIX = D_MODEL
ATT_HEADS = 8
ATT_HEAD_DIM = 128
D_ATT = ATT_HEADS * ATT_HEAD_DIM
Q_BLOCK = 128
SSD_HEADS = 16
SSD_HEAD_DIM = 64
D_SSD = SSD_HEADS * SSD_HEAD_DIM
SSD_GROUPS = 4
SSD_STATE = 128
CONV_WIDTH = 4
D_CONV = D_SSD + 2 * SSD_GROUPS * SSD_STATE
IN_SPLITS = (D_ATT, 2 * D_ATT, 3 * D_ATT, 3 * D_ATT + ATT_HEADS,
             3 * D_ATT + ATT_HEADS + D_SSD,
             3 * D_ATT + ATT_HEADS + D_SSD + D_CONV)
D_IN_PROJ = 3 * D_ATT + ATT_HEADS + D_SSD + D_CONV + SSD_HEADS
N_EXPERTS = 16
N_EXPERT_GROUPS = 4
EXPERTS_PER_GROUP = N_EXPERTS // N_EXPERT_GROUPS
TOP_GROUPS = 1
TOP_K = 2
D_EXPERT = 1024
PLE_DIM = 256
DEEPNORM_ALPHA = (2 * DEPTH) ** 0.25
DEEPNORM_BETA = (8 * DEPTH) ** -0.25
LN_EPS = 1e-5
RMS_EPS = 1e-5

kernel_name = "fox_ssd_hybrid_deepnorm_moe"


def layer_norm(x, g, b):
    xf = x.astype(jnp.float32)
    mu = xf.mean(-1, keepdims=True)
    var = jnp.square(xf - mu).mean(-1, keepdims=True)
    return ((xf - mu) * lax.rsqrt(var + LN_EPS) * g + b).astype(x.dtype)


def forgetting_attention(q, k, v, log_f):
    b, s, h, dh = q.shape
    f_cum = jnp.transpose(jnp.cumsum(log_f, axis=1), (0, 2, 1))
    pos = jnp.arange(s)
    scale = dh ** -0.5

    def block(i):
        start = i * Q_BLOCK
        qb = lax.dynamic_slice_in_dim(q, start, Q_BLOCK, axis=1)
        fq = lax.dynamic_slice_in_dim(f_cum, start, Q_BLOCK, axis=2)
        qpos = start + jnp.arange(Q_BLOCK)
        logits = jnp.einsum('bqhd,bkhd->bhqk', qb, k,
                            preferred_element_type=jnp.float32) * scale
        logits = logits + fq[..., :, None] - f_cum[..., None, :]
        logits = jnp.where(pos[None, :] <= qpos[:, None], logits, -jnp.inf)
        probs = jax.nn.softmax(logits, axis=-1)
        return jnp.einsum('bhqk,bkhd->bqhd', probs.astype(v.dtype), v)

    out = lax.map(block, jnp.arange(s // Q_BLOCK))
    return jnp.moveaxis(out, 0, 1).reshape(b, s, h * dh)


def causal_depthwise_conv(u, w, bias):
    c = u.shape[-1]
    k = w.shape[0]
    out = lax.conv_general_dilated(u, w[:, None, :], window_strides=(1,),
                                   padding=[(k - 1, 0)],
                                   dimension_numbers=('NWC', 'WIO', 'NWC'),
                                   feature_group_count=c)
    return out + bias


def ssd_scan(xh, dt, a, bm, cm):
    f32 = jnp.float32
    b, s, h, p = xh.shape
    g, n = bm.shape[2], bm.shape[3]
    r = h // g
    nc = s // CHUNK
    xc = xh.reshape(b, nc, CHUNK, g, r, p).astype(f32)
    dtc = dt.reshape(b, nc, CHUNK, g, r)
    bc = bm.reshape(b, nc, CHUNK, g, n).astype(f32)
    cc = cm.reshape(b, nc, CHUNK, g, n).astype(f32)
    a_cum = jnp.cumsum(dtc * a.reshape(g, r), axis=2)
    seg = a_cum[:, :, :, None] - a_cum[:, :, None, :]
    causal = jnp.tril(jnp.ones((CHUNK, CHUNK), dtype=bool))[:, :, None, None]
    decay = jnp.exp(jnp.where(causal, seg, -jnp.inf))
    cb = jnp.einsum('bcign,bcjgn->bcijg', cc, bc)
    y_diag = jnp.einsum('bcijg,bcijgr,bcjgr,bcjgrp->bcigrp', cb, decay, dtc, xc)
    decay_to_end = jnp.exp(a_cum[:, :, -1:] - a_cum)
    states = jnp.einsum('bcjgn,bcjgr,bcjgrp->bcgrpn', bc, decay_to_end * dtc, xc)
    chunk_decay = jnp.exp(a_cum[:, :, -1])

    def step(hstate, inp):
        st, dec = inp
        return hstate * dec[..., None, None] + st, hstate

    h0 = jnp.zeros((b, g, r, p, n), f32)
    _, prev = lax.scan(step, h0, (jnp.moveaxis(states, 1, 0),
                                  jnp.moveaxis(chunk_decay, 1, 0)))
    prev = jnp.moveaxis(prev, 0, 1)
    y_off = jnp.einsum('bcign,bcgrpn,bcigr->bcigrp', cc, prev, jnp.exp(a_cum))
    return (y_diag + y_off).reshape(b, s, h, p)


def hybrid_mixer(x, w_in, b_forget, conv_w, conv_b, dt_bias, a_log, d_skip,
                 ssd_norm_w, w_out):
    f32 = jnp.float32
    b, s, _ = x.shape
    proj = jnp.einsum('bsd,de->bse', x, w_in)
    q, k, v, f_logit, z, xbc, dt_raw = jnp.split(proj, IN_SPLITS, axis=-1)
    qh = q.reshape(b, s, ATT_HEADS, ATT_HEAD_DIM)
    kh = k.reshape(b, s, ATT_HEADS, ATT_HEAD_DIM)
    vh = v.reshape(b, s, ATT_HEADS, ATT_HEAD_DIM)
    log_f = jax.nn.log_sigmoid((f_logit + b_forget).astype(f32))
    att = forgetting_attention(qh, kh, vh, log_f)
    xbc = jax.nn.silu(causal_depthwise_conv(xbc, conv_w, conv_b))
    xs, bm, cm = jnp.split(xbc, (D_SSD, D_SSD + SSD_GROUPS * SSD_STATE), axis=-1)
    dt = jax.nn.softplus((dt_raw + dt_bias).astype(f32))
    a = -jnp.exp(a_log.astype(f32))
    xh = xs.reshape(b, s, SSD_HEADS, SSD_HEAD_DIM)
    y = ssd_scan(xh, dt, a,
                 bm.reshape(b, s, SSD_GROUPS, SSD_STATE),
                 cm.reshape(b, s, SSD_GROUPS, SSD_STATE))
    y = y + d_skip.astype(f32)[:, None] * xh.astype(f32)
    yg = (y.reshape(b, s, D_SSD) * jax.nn.silu(z.astype(f32))).reshape(b, s, SSD_GROUPS, -1)
    yg = yg * lax.rsqrt(jnp.mean(jnp.square(yg), -1, keepdims=True) + RMS_EPS)
    ssd = (yg.reshape(b, s, D_SSD) * ssd_norm_w).astype(x.dtype)
    mixed = jnp.concatenate([att.astype(x.dtype), ssd], axis=-1)
    return jnp.einsum('bse,ed->bsd', mixed, w_out)


def routed_experts(x, w_router, router_bias, w_gate, w_up, w_down):
    f32 = jnp.float32
    b, s, d = x.shape
    xt = x.reshape(b * s, d)
    scores = jax.nn.sigmoid(jnp.einsum('td,de->te', xt, w_router,
                                       preferred_element_type=f32))
    biased = (scores + router_bias.astype(f32)).reshape(-1, N_EXPERT_GROUPS, EXPERTS_PER_GROUP)
    group_score = lax.top_k(biased, 2)[0].sum(-1)
    _, top_group = lax.top_k(group_score, TOP_GROUPS)
    group_mask = (top_group[:, :, None] == jnp.arange(N_EXPERT_GROUPS)).any(axis=1)
    masked = jnp.where(group_mask[:, :, None], biased, -jnp.inf).reshape(-1, N_EXPERTS)
    _, top_idx = lax.top_k(masked, TOP_K)
    top_w = jnp.take_along_axis(scores, top_idx, axis=-1)
    top_w = top_w / top_w.sum(-1, keepdims=True)
    combine = jnp.einsum('tk,tke->te', top_w, jax.nn.one_hot(top_idx, N_EXPERTS, dtype=f32))
    out = jnp.zeros((b * s, d), f32)
    for e in range(N_EXPERTS):
        hid = jax.nn.silu(xt @ w_gate[e]) * (xt @ w_up[e])
        out = out + combine[:, e:e + 1] * (hid @ w_down[e])
    return out.reshape(b, s, d).astype(x.dtype)


def per_layer_embedding(x, p_i, w_ple, w_ple_gate):
    gate = jax.nn.sigmoid(jnp.einsum('bsd,de->bse', x, w_ple_gate))
    return gate * jnp.einsum('bsk,kd->bsd', p_i, w_ple)


def setup_inputs(seed: int = 0) -> dict:
    key = jax.random.key(seed)
    ks = jax.random.split(key, 24)
    f32 = jnp.float32
    nrm = lambda k, shape: jax.random.normal(k, shape, f32)
    x = nrm(ks[0], (BATCH, SEQ, D_MODEL))
    p = nrm(ks[1], (DEPTH, BATCH, SEQ, PLE_DIM))
    col_scale = jnp.concatenate([
        jnp.ones((2 * D_ATT,), f32),
        jnp.full((D_ATT,), DEEPNORM_BETA, f32),
        jnp.ones((D_IN_PROJ - 3 * D_ATT,), f32)])
    w_in = nrm(ks[2], (DEPTH, D_MODEL, D_IN_PROJ)) * D_MODEL ** -0.5 * col_scale
    b_forget = jax.random.uniform(ks[3], (DEPTH, ATT_HEADS), f32, 1.0, 5.0)
    conv_w = nrm(ks[4], (DEPTH, CONV_WIDTH, D_CONV)) * CONV_WIDTH ** -0.5
    conv_b = 0.02 * nrm(ks[5], (DEPTH, D_CONV))
    dt0 = jnp.exp(jax.random.uniform(ks[6], (DEPTH, SSD_HEADS), f32,
                                     math.log(1e-3), math.log(1e-1)))
    dt_bias = dt0 + jnp.log(-jnp.expm1(-dt0))
    a_log = jnp.log(jax.random.uniform(ks[7], (DEPTH, SSD_HEADS), f32, 1.0, 16.0))
    d_skip = 1.0 + 0.1 * nrm(ks[8], (DEPTH, SSD_HEADS))
    ssd_norm_w = 1.0 + 0.1 * nrm(ks[9], (DEPTH, D_SSD))
    w_out = nrm(ks[10], (DEPTH, D_MIX, D_MODEL)) * D_MIX ** -0.5 * DEEPNORM_BETA
    ln1_g = 1.0 + 0.05 * nrm(ks[11], (DEPTH, D_MODEL))
    ln1_b = 0.02 * nrm(ks[12], (DEPTH, D_MODEL))
    w_router = nrm(ks[13], (D_MODEL, N_EXPERTS)) * D_MODEL ** -0.5
    router_bias = 0.01 * nrm(ks[14], (N_EXPERTS,))
    w_gate = nrm(ks[15], (DEPTH, N_EXPERTS, D_MODEL, D_EXPERT)) * D_MODEL ** -0.5
    w_up = nrm(ks[16], (DEPTH, N_EXPERTS, D_MODEL, D_EXPERT)) * D_MODEL ** -0.5 * DEEPNORM_BETA
    w_down = nrm(ks[17], (DEPTH, N_EXPERTS, D_EXPERT, D_MODEL)) * D_EXPERT ** -0.5 * DEEPNORM_BETA
    w_ple = nrm(ks[18], (DEPTH, PLE_DIM, D_MODEL)) * PLE_DIM ** -0.5
    w_ple_gate = nrm(ks[19], (DEPTH, D_MODEL, D_MODEL)) * D_MODEL ** -0.5
    ln2_g = 1.0 + 0.05 * nrm(ks[20], (DEPTH, D_MODEL))
    ln2_b = 0.02 * nrm(ks[21], (DEPTH, D_MODEL))
    return {"x": x, "p": p, "w_in": w_in, "b_forget": b_forget,
            "conv_w": conv_w, "conv_b": conv_b, "dt_bias": dt_bias,
            "a_log": a_log, "d_skip": d_skip, "ssd_norm_w": ssd_norm_w,
            "w_out": w_out, "ln1_g": ln1_g, "ln1_b": ln1_b,
            "w_router": w_router, "router_bias": router_bias,
            "w_gate": w_gate, "w_up": w_up, "w_down": w_down,
            "w_ple": w_ple, "w_ple_gate": w_ple_gate,
            "ln2_g": ln2_g, "ln2_b": ln2_b}


def reference(x, p, w_in, b_forget, conv_w, conv_b, dt_bias, a_log, d_skip,
              ssd_norm_w, w_out, ln1_g, ln1_b, w_router, router_bias,
              w_gate, w_up, w_down, w_ple, w_ple_gate, ln2_g, ln2_b):
    h = x
    for i in range(DEPTH):
        mix = hybrid_mixer(h, w_in[i], b_forget[i], conv_w[i], conv_b[i],
                           dt_bias[i], a_log[i], d_skip[i], ssd_norm_w[i], w_out[i])
        h = layer_norm(DEEPNORM_ALPHA * h + mix, ln1_g[i], ln1_b[i])
        moe = routed_experts(h, w_router, router_bias, w_gate[i], w_up[i], w_down[i])
        ple = per_layer_embedding(h, p[i], w_ple[i], w_ple_gate[i])
        h = layer_norm(DEEPNORM_ALPHA * h + moe + ple, ln2_g[i], ln2_b[i])
    return h
```

```python
import functools
import math

import jax
import jax.numpy as jnp
from jax import lax
from jax.experimental import pallas as pl
from jax.experimental.pallas import tpu as pltpu

F32 = jnp.float32
BF16 = jnp.bfloat16
I32 = jnp.int32

D_MODEL = 2048
ATT_HEADS = 8
ATT_HEAD_DIM = 128
D_ATT = ATT_HEADS * ATT_HEAD_DIM
SSD_HEADS = 16
SSD_HEAD_DIM = 64
D_SSD = SSD_HEADS * SSD_HEAD_DIM
SSD_GROUPS = 4
HEADS_PER_GROUP = SSD_HEADS // SSD_GROUPS
SSD_STATE = 128
D_GROUP = D_SSD // SSD_GROUPS
CONV_WIDTH = 4
D_BC = SSD_GROUPS * SSD_STATE
D_CONV = D_SSD + 2 * D_BC
N_EXPERTS = 16
N_EXPERT_GROUPS = 4
EXPERTS_PER_GROUP = N_EXPERTS // N_EXPERT_GROUPS
D_EXPERT = 1024
PLE_DIM = 256
DEPTH = 2
DEEPNORM_ALPHA = (2 * DEPTH) ** 0.25
LN_EPS = 1e-5
RMS_EPS = 1e-5

LANES = 128
SUBLANES = 8
VMEM_LIMIT_BYTES = 56 * 1024 * 1024

D_MAIN = 3 * D_ATT + D_SSD + D_CONV
COL_Z = 3 * D_ATT
COL_XBC = 3 * D_ATT + D_SSD

SSD_CHUNK = 128
CONV_HALO = SUBLANES


def _params(*sem):
    return pltpu.CompilerParams(dimension_semantics=sem,
                                vmem_limit_bytes=VMEM_LIMIT_BYTES)


def _softplus(u):
    return jnp.maximum(u, 0.0) + jnp.log1p(jnp.exp(-jnp.abs(u)))


def _sigmoid(u):
    return 1.0 / (1.0 + jnp.exp(-u))


def _silu(u):
    return u * _sigmoid(u)


def _split3(v):
    hi = v.astype(BF16)
    r1 = v - hi.astype(F32)
    mid = r1.astype(BF16)
    lo = (r1 - mid.astype(F32)).astype(BF16)
    return hi, mid, lo


def _dot(a, b):
    return jnp.dot(a, b, preferred_element_type=F32)


def _exact_left_dot(ones_bf16, v):
    hi, mid, lo = _split3(v)
    return _dot(ones_bf16, hi) + (_dot(ones_bf16, mid) + _dot(ones_bf16, lo))


def _proj_kernel(x_ref, w_ref, o_ref):
    o_ref[...] = _dot(x_ref[...], w_ref[...]).astype(o_ref.dtype)


def _in_proj(h_bf, w_main, tm, tn):
    s = h_bf.shape[0]
    n = w_main.shape[1]
    return pl.pallas_call(
        _proj_kernel,
        grid=(s // tm, n // tn),
        in_specs=[pl.BlockSpec((tm, D_MODEL), lambda i, j: (i, 0)),
                  pl.BlockSpec((D_MODEL, tn), lambda i, j: (0, j))],
        out_specs=pl.BlockSpec((tm, tn), lambda i, j: (i, j)),
        out_shape=jax.ShapeDtypeStruct((s, n), BF16),
        compiler_params=_params("parallel", "arbitrary"),
        name="in_proj",
    )(h_bf, w_main)


def _gates_kernel(x_ref, w_ref, b_ref, tri_ref, fcum_ref, dt_ref, carry_ref):
    @pl.when(pl.program_id(0) == 0)
    def _():
        carry_ref[...] = jnp.zeros_like(carry_ref)

    v = _dot(x_ref[...], w_ref[...]) + b_ref[...]
    log_f = -_softplus(-v[:, :LANES])
    dt_ref[...] = _softplus(v[:, LANES:])
    fcum = _exact_left_dot(tri_ref[...], log_f) + carry_ref[...]
    fcum_ref[...] = fcum
    carry_ref[...] = fcum[-1:, :]


def _gates(h_bf, w_small, b_small, tm):
    s = h_bf.shape[0]
    tri = jnp.tril(jnp.ones((tm, tm), F32)).astype(BF16)
    return pl.pallas_call(
        _gates_kernel,
        grid=(s // tm,),
        in_specs=[pl.BlockSpec((tm, D_MODEL), lambda i: (i, 0)),
                  pl.BlockSpec((D_MODEL, 2 * LANES), lambda i: (0, 0)),
                  pl.BlockSpec((1, 2 * LANES), lambda i: (0, 0)),
                  pl.BlockSpec((tm, tm), lambda i: (0, 0))],
        out_specs=[pl.BlockSpec((tm, LANES), lambda i: (i, 0)),
                   pl.BlockSpec((tm, LANES), lambda i: (i, 0))],
        out_shape=[jax.ShapeDtypeStruct((s, LANES), F32),
                   jax.ShapeDtypeStruct((s, LANES), F32)],
        scratch_shapes=[pltpu.VMEM((1, LANES), F32)],
        compiler_params=_params("arbitrary"),
        name="gates",
    )(h_bf, w_small, b_small, tri)


def _attn_kernel(q_ref, k_ref, v_ref, fcol_ref, frow_ref, o_ref, *, tq, scale):
    h = pl.program_id(0)
    i = pl.program_id(1)
    q = q_ref[...]
    lane = lax.broadcasted_iota(I32, (tq, LANES), 1)
    f_q = jnp.sum(jnp.where(lane == h, fcol_ref[...], 0.0), axis=1, keepdims=True)
    row = lax.broadcasted_iota(I32, (tq, tq), 0)
    col = lax.broadcasted_iota(I32, (tq, tq), 1)

    def step(j, carry, diagonal):
        m, l, acc = carry
        start = pl.multiple_of(j * tq, tq)
        kj = k_ref[pl.ds(start, tq), :]
        vj = v_ref[pl.ds(start, tq), :]
        s = lax.dot_general(q, kj, (((1,), (1,)), ((), ())),
                            preferred_element_type=F32) * scale
        s = s + (f_q - frow_ref[0, :, pl.ds(start, tq)])
        if diagonal:
            s = jnp.where(col <= row, s, -jnp.inf)
        m_new = jnp.maximum(m, jnp.max(s, axis=1, keepdims=True))
        p = jnp.exp(s - m_new)
        alpha = jnp.exp(m - m_new)
        l = alpha * l + jnp.sum(p, axis=1, keepdims=True)
        acc = alpha * acc + _dot(p.astype(BF16), vj)
        return m_new, l, acc

    init = (jnp.full((tq, 1), -jnp.inf, F32), jnp.zeros((tq, 1), F32),
            jnp.zeros((tq, ATT_HEAD_DIM), F32))
    carry = lax.fori_loop(0, i, lambda j, c: step(j, c, False), init)
    _, l, acc = step(i, carry, True)
    o_ref[...] = (acc / l).astype(o_ref.dtype)


def _attention(y, fcum, frow, tq):
    s = y.shape[0]
    kern = functools.partial(_attn_kernel, tq=tq, scale=ATT_HEAD_DIM ** -0.5)
    return pl.pallas_call(
        kern,
        grid=(ATT_HEADS, s // tq),
        in_specs=[pl.BlockSpec((tq, ATT_HEAD_DIM), lambda h, i: (i, h)),
                  pl.BlockSpec((s, ATT_HEAD_DIM), lambda h, i: (0, ATT_HEADS + h)),
                  pl.BlockSpec((s, ATT_HEAD_DIM), lambda h, i: (0, 2 * ATT_HEADS + h)),
                  pl.BlockSpec((tq, LANES), lambda h, i: (i, 0)),
                  pl.BlockSpec((1, 1, s), lambda h, i: (h, 0, 0))],
        out_specs=pl.BlockSpec((tq, ATT_HEAD_DIM), lambda h, i: (i, h)),
        out_shape=jax.ShapeDtypeStruct((s, D_ATT), BF16),
        compiler_params=_params("parallel", "arbitrary"),
        name="fox_attention",
    )(y, y, y, fcum, frow)


def _expand_heads(w, n_heads):
    length = w.shape[0]
    lane = lax.broadcasted_iota(I32, (length, LANES), 1)
    blocks = []
    for m in range(n_heads // 2):
        blocks.append(jnp.where(lane < SSD_HEAD_DIM, w[:, 2 * m:2 * m + 1],
                                w[:, 2 * m + 1:2 * m + 2]))
    return jnp.concatenate(blocks, axis=1)


def _ssd_kernel(xbc_ref, z_ref, dt_ref, convw_ref, convb_ref, a_ref, dskip_ref,
                normw_ref, tri_ref, o_ref, ubuf_ref, state_ref):
    length = SSD_CHUNK

    @pl.when(pl.program_id(0) == 0)
    def _():
        ubuf_ref[0:CONV_HALO, :] = jnp.zeros((CONV_HALO, D_CONV), F32)
        state_ref[...] = jnp.zeros_like(state_ref)

    ubuf_ref[CONV_HALO:CONV_HALO + length, :] = xbc_ref[...].astype(F32)
    conv = convb_ref[...]
    for k in range(CONV_WIDTH):
        off = CONV_HALO - (CONV_WIDTH - 1) + k
        conv = conv + convw_ref[k:k + 1, :] * ubuf_ref[off:off + length, :]
    tail = ubuf_ref[length:length + CONV_HALO, :]
    ubuf_ref[0:CONV_HALO, :] = tail
    xc = _silu(conv)
    xs = xc[:, :D_SSD]

    dt = dt_ref[...]
    da = dt * a_ref[...]
    acum = _exact_left_dot(tri_ref[...], da)
    acum_t = acum.T
    a_last = acum[length - 1:length, :]
    w_off = _expand_heads(jnp.exp(acum), SSD_HEADS)
    w_state = _expand_heads(jnp.exp(a_last - acum) * dt, SSD_HEADS)
    w_dt = _expand_heads(dt, SSD_HEADS)
    chunk_decay = _expand_heads(jnp.exp(a_last), SSD_HEADS)

    x_dt = (xs * w_dt).astype(BF16)
    x_state = (xs * w_state).astype(BF16)
    row = lax.broadcasted_iota(I32, (length, length), 0)
    col = lax.broadcasted_iota(I32, (length, length), 1)
    causal = col <= row
    grp_lane = lax.broadcasted_iota(I32, (length, D_GROUP), 1) // SSD_HEAD_DIM

    y_parts = []
    for g in range(SSD_GROUPS):
        b_g = xc[:, D_SSD + g * SSD_STATE:D_SSD + (g + 1) * SSD_STATE].astype(BF16)
        c_g = xc[:, D_SSD + D_BC + g * SSD_STATE:
                 D_SSD + D_BC + (g + 1) * SSD_STATE].astype(BF16)
        cb = lax.dot_general(c_g, b_g, (((1,), (1,)), ((), ())),
                             preferred_element_type=F32)
        cols = slice(g * D_GROUP, (g + 1) * D_GROUP)
        x_dt_g = x_dt[:, cols]
        m_blocks, x_blocks = [], []
        for r in range(HEADS_PER_GROUP):
            hd = g * HEADS_PER_GROUP + r
            seg = acum[:, hd:hd + 1] - acum_t[hd:hd + 1, :]
            decay = jnp.exp(jnp.where(causal, seg, -jnp.inf))
            m_blocks.append((cb * decay).astype(BF16))
            x_blocks.append(jnp.where(grp_lane == r, x_dt_g, jnp.zeros_like(x_dt_g)))
        y_diag = _dot(jnp.concatenate(m_blocks, axis=1),
                      jnp.concatenate(x_blocks, axis=0))
        st = state_ref[g]
        y_off = _dot(c_g, st.astype(BF16)) * w_off[:, cols]
        y_parts.append(y_diag + y_off)
        new_st = lax.dot_general(b_g, x_state[:, cols], (((0,), (0,)), ((), ())),
                                 preferred_element_type=F32)
        state_ref[g] = st * chunk_decay[:, cols] + new_st

    y = jnp.concatenate(y_parts, axis=1) + dskip_ref[...] * xs
    yg = y * _silu(z_ref[...].astype(F32))
    outs = []
    for g in range(SSD_GROUPS):
        blk = yg[:, g * D_GROUP:(g + 1) * D_GROUP]
        ms = jnp.mean(blk * blk, axis=1, keepdims=True)
        outs.append(blk * lax.rsqrt(ms + RMS_EPS))
    o_ref[...] = (jnp.concatenate(outs, axis=1) * normw_ref[...]).astype(o_ref.dtype)


def _ssd(y, dtv, conv_w, conv_b, a_row, dskip_row, normw_row):
    s = y.shape[0]
    length = SSD_CHUNK
    tri = jnp.tril(jnp.ones((length, length), F32)).astype(BF16)
    full = lambda shape: pl.BlockSpec(shape, lambda c: (0,) * len(shape))
    return pl.pallas_call(
        _ssd_kernel,
        grid=(s // length,),
        in_specs=[pl.BlockSpec((length, D_CONV), lambda c: (c, COL_XBC // D_CONV)),
                  pl.BlockSpec((length, D_SSD), lambda c: (c, COL_Z // D_SSD)),
                  pl.BlockSpec((length, LANES), lambda c: (c, 0)),
                  full((CONV_WIDTH, D_CONV)), full((1, D_CONV)), full((1, LANES)),
                  full((1, D_SSD)), full((1, D_SSD)), full((length, length))],
        out_specs=pl.BlockSpec((length, D_SSD), lambda c: (c, 0)),
        out_shape=jax.ShapeDtypeStruct((s, D_SSD), BF16),
        scratch_shapes=[pltpu.VMEM((length + CONV_HALO, D_CONV), F32),
                        pltpu.VMEM((SSD_GROUPS, SSD_STATE, D_GROUP), F32)],
        compiler_params=_params("arbitrary"),
        name="ssd",
    )(y, y, dtv, conv_w, conv_b, a_row, dskip_row, normw_row, tri)


def _layer_norm(xf, g, b):
    mu = jnp.mean(xf, axis=1, keepdims=True)
    xc = xf - mu
    var = jnp.mean(xc * xc, axis=1, keepdims=True)
    return xc * lax.rsqrt(var + LN_EPS) * g + b


def _out_proj_kernel(att_ref, ssd_ref, wa_ref, ws_ref, h_ref, g_ref, b_ref,
                     o_ref, obf_ref):
    mix = _dot(att_ref[...], wa_ref[...]) + _dot(ssd_ref[...], ws_ref[...])
    out = _layer_norm(DEEPNORM_ALPHA * h_ref[...] + mix, g_ref[...], b_ref[...])
    o_ref[...] = out
    obf_ref[...] = out.astype(BF16)


def _out_proj_ln(att, ssd, w_att, w_ssd, h, g, b, tm):
    s = h.shape[0]
    full = lambda shape: pl.BlockSpec(shape, lambda i: (0,) * len(shape))
    return pl.pallas_call(
        _out_proj_kernel,
        grid=(s // tm,),
        in_specs=[pl.BlockSpec((tm, D_ATT), lambda i: (i, 0)),
                  pl.BlockSpec((tm, D_SSD), lambda i: (i, 0)),
                  full((D_ATT, D_MODEL)), full((D_SSD, D_MODEL)),
                  pl.BlockSpec((tm, D_MODEL), lambda i: (i, 0)),
                  full((1, D_MODEL)), full((1, D_MODEL))],
        out_specs=[pl.BlockSpec((tm, D_MODEL), lambda i: (i, 0)),
                   pl.BlockSpec((tm, D_MODEL), lambda i: (i, 0))],
        out_shape=[jax.ShapeDtypeStruct((s, D_MODEL), F32),
                   jax.ShapeDtypeStruct((s, D_MODEL), BF16)],
        compiler_params=_params("parallel"),
        name="out_proj_ln",
    )(att, ssd, w_att, w_ssd, h, g, b)


def _router_kernel(h_ref, wr_ref, bias_ref, triu_ref, idx_ref, wcol_ref, cnt_ref,
                   carry_ref, *, tm):
    @pl.when(pl.program_id(0) == 0)
    def _():
        carry_ref[...] = jnp.zeros_like(carry_ref)

    h = h_ref[...]
    h_hi = h.astype(BF16)
    h_lo = (h - h_hi.astype(F32)).astype(BF16)
    logits = _dot(h_hi, wr_ref[0]) + (_dot(h_hi, wr_ref[1]) + _dot(h_lo, wr_ref[0]))
    lt = logits.T
    scores = _sigmoid(lt[:N_EXPERTS, :])
    biased = scores + bias_ref[...]
    npg = EXPERTS_PER_GROUP
    s_r = [scores[npg * r:npg * (r + 1), :] for r in range(npg)]
    b_r = [biased[npg * r:npg * (r + 1), :] for r in range(npg)]

    gs = None
    for a in range(npg):
        for b in range(a + 1, npg):
            pair = b_r[a] + b_r[b]
            gs = pair if gs is None else jnp.maximum(gs, pair)
    best = jnp.zeros((1, tm), I32)
    top = gs[0:1, :]
    for g in range(1, N_EXPERT_GROUPS):
        better = gs[g:g + 1, :] > top
        best = jnp.where(better, g, best)
        top = jnp.where(better, gs[g:g + 1, :], top)

    def pick_group(a):
        out = a[0:1, :]
        for g in range(1, N_EXPERT_GROUPS):
            out = jnp.where(best == g, a[g:g + 1, :], out)
        return out

    sb = [pick_group(b) for b in b_r]
    ss = [pick_group(s) for s in s_r]
    first = jnp.zeros((1, tm), I32)
    fmax = sb[0]
    for r in range(1, npg):
        better = sb[r] > fmax
        first = jnp.where(better, r, first)
        fmax = jnp.where(better, sb[r], fmax)
    second = jnp.zeros((1, tm), I32)
    smax = jnp.full((1, tm), -jnp.inf, F32)
    for r in range(npg):
        cand = jnp.where(first == r, -jnp.inf, sb[r])
        better = cand > smax
        second = jnp.where(better, r, second)
        smax = jnp.where(better, cand, smax)

    def pick_expert(which):
        out = ss[0]
        for r in range(1, npg):
            out = jnp.where(which == r, ss[r], out)
        return out

    a0 = pick_expert(first)
    a1 = pick_expert(second)
    denom = a0 + a1
    e0 = best * npg + first
    e1 = best * npg + second

    erow = lax.broadcasted_iota(I32, (N_EXPERTS, tm), 0)
    hit0 = erow == e0
    hit1 = erow == e1
    onehot = jnp.where(hit0, 1.0, 0.0) + jnp.where(hit1, 1.0, 0.0)
    cum = _dot(onehot.astype(BF16), triu_ref[...])
    carry = carry_ref[...][:, 0:1]
    before = cum - onehot + carry
    rank0 = jnp.sum(jnp.where(hit0, before, 0.0), axis=0, keepdims=True)
    rank1 = jnp.sum(jnp.where(hit1, before, 0.0), axis=0, keepdims=True)
    new_carry = carry_ref[...] + cum[:, tm - 1:tm]
    carry_ref[...] = new_carry
    cnt_ref[...] = new_carry

    zeros_i = jnp.zeros((SUBLANES - 4, tm), I32)
    idx_ref[...] = jnp.concatenate(
        [e0, e1, rank0.astype(I32), rank1.astype(I32), zeros_i], axis=0)
    wrows = jnp.concatenate([a0 / denom, a1 / denom,
                             jnp.zeros((LANES - 2, tm), F32)], axis=0)
    wcol_ref[...] = wrows.T


def _router(h, w_router2, bias_col, tm):
    s = h.shape[0]
    triu = jnp.triu(jnp.ones((tm, tm), F32)).astype(BF16)
    kern = functools.partial(_router_kernel, tm=tm)
    return pl.pallas_call(
        kern,
        grid=(s // tm,),
        in_specs=[pl.BlockSpec((tm, D_MODEL), lambda i: (i, 0)),
                  pl.BlockSpec((2, D_MODEL, LANES), lambda i: (0, 0, 0)),
                  pl.BlockSpec((N_EXPERTS, 1), lambda i: (0, 0)),
                  pl.BlockSpec((tm, tm), lambda i: (0, 0))],
        out_specs=[pl.BlockSpec((SUBLANES, tm), lambda i: (0, i)),
                   pl.BlockSpec((tm, LANES), lambda i: (i, 0)),
                   pl.BlockSpec((N_EXPERTS, LANES), lambda i: (0, 0))],
        out_shape=[jax.ShapeDtypeStruct((SUBLANES, s), I32),
                   jax.ShapeDtypeStruct((s, LANES), F32),
                   jax.ShapeDtypeStruct((N_EXPERTS, LANES), F32)],
        scratch_shapes=[pltpu.VMEM((N_EXPERTS, LANES), F32)],
        compiler_params=_params("arbitrary"),
        name="router",
    )(h, w_router2, bias_col, triu)


def _work_items(counts, tm, n_tiles):
    ends = jnp.cumsum(counts)
    starts = ends - counts
    first_tile = starts // tm
    last_tile = (ends - 1) // tm
    n_e = jnp.where(counts > 0, last_tile - first_tile + 1, 0)
    item_end = jnp.cumsum(n_e)
    item_start = item_end - n_e
    n_items = n_tiles + N_EXPERTS - 1
    w = jnp.arange(n_items, dtype=I32)
    total = item_end[-1]
    valid = w < total
    wc = jnp.minimum(w, total - 1)
    e_w = jnp.minimum(jnp.sum((item_end[None, :] <= wc[:, None]).astype(I32), axis=1),
                      N_EXPERTS - 1)
    tile_w = first_tile[e_w] + (wc - item_start[e_w])
    lo = jnp.clip(starts[e_w] - tile_w * tm, 0, tm)
    hi = jnp.clip(ends[e_w] - tile_w * tm, 0, tm)
    lo = jnp.where(valid, lo, 0)
    hi = jnp.where(valid, hi, 0)
    return tile_w.astype(I32), e_w, lo.astype(I32), hi.astype(I32)


def _dispatch_kernel(pos0_ref, pos1_ref, h_ref, xs_ref, sems, *, tm):
    base = pl.program_id(0) * tm

    def issue(r, carry):
        src = h_ref.at[pl.ds(r, 1), :]
        pltpu.make_async_copy(src, xs_ref.at[pl.ds(pos0_ref[base + r], 1), :],
                              sems.at[0]).start()
        pltpu.make_async_copy(src, xs_ref.at[pl.ds(pos1_ref[base + r], 1), :],
                              sems.at[1]).start()
        return carry

    lax.fori_loop(0, tm, issue, 0)
    for k in range(2):
        pltpu.make_async_copy(h_ref, xs_ref.at[pl.ds(0, tm), :], sems.at[k]).wait()


def _dispatch(pos0, pos1, h, tm):
    s = h.shape[0]
    kern = functools.partial(_dispatch_kernel, tm=tm)
    return pl.pallas_call(
        kern,
        grid_spec=pltpu.PrefetchScalarGridSpec(
            num_scalar_prefetch=2,
            grid=(s // tm,),
            in_specs=[pl.BlockSpec((tm, D_MODEL), lambda i, p0, p1: (i, 0))],
            out_specs=pl.BlockSpec(memory_space=pl.ANY),
            scratch_shapes=[pltpu.SemaphoreType.DMA((2,))]),
        out_shape=jax.ShapeDtypeStruct((2 * s, D_MODEL), F32),
        compiler_params=_params("arbitrary"),
        name="dispatch",
    )(pos0, pos1, h)


def _moe_kernel(tile_ref, exp_ref, lo_ref, hi_ref, x_ref, wg_ref, wu_ref, wd_ref,
                o_ref, *, tm):
    w = pl.program_id(0)
    lo = lo_ref[w]
    hi = hi_ref[w]
    fresh = jnp.logical_or(w == 0, tile_ref[w] != tile_ref[jnp.maximum(w - 1, 0)])

    @pl.when(hi > lo)
    def _():
        x = x_ref[...].astype(BF16)
        hid = _silu(_dot(x, wg_ref[0])) * _dot(x, wu_ref[0])
        y = _dot(hid.astype(BF16), wd_ref[0])
        row = lax.broadcasted_iota(I32, (tm, 1), 0)
        keep = jnp.logical_and(row >= lo, row < hi)

        @pl.when(fresh)
        def _():
            o_ref[...] = jnp.where(keep, y, 0.0)

        @pl.when(jnp.logical_not(fresh))
        def _():
            o_ref[...] = jnp.where(keep, y, o_ref[...])


def _moe(items, xs, w_gate, w_up, w_down, tm):
    rows = xs.shape[0]
    n_items = items[0].shape[0]
    kern = functools.partial(_moe_kernel, tm=tm)
    wspec = lambda shape: pl.BlockSpec(
        (1,) + shape, lambda w, tile, exp, lo, hi: (exp[w], 0, 0))
    return pl.pallas_call(
        kern,
        grid_spec=pltpu.PrefetchScalarGridSpec(
            num_scalar_prefetch=4,
            grid=(n_items,),
            in_specs=[pl.BlockSpec((tm, D_MODEL), lambda w, tile, exp, lo, hi: (tile[w], 0)),
                      wspec((D_MODEL, D_EXPERT)), wspec((D_MODEL, D_EXPERT)),
                      wspec((D_EXPERT, D_MODEL))],
            out_specs=pl.BlockSpec((tm, D_MODEL), lambda w, tile, exp, lo, hi: (tile[w], 0))),
        out_shape=jax.ShapeDtypeStruct((rows, D_MODEL), F32),
        compiler_params=_params("arbitrary"),
        name="moe_experts",
    )(*items, xs, w_gate, w_up, w_down)


def _ple_kernel(pos0_ref, pos1_ref, hbf_ref, h_ref, p_ref, wgate_ref, wple_ref,
                wcol_ref, g_ref, b_ref, ys_ref, o_ref, obf_ref, buf_ref, sems, *, tm):
    base = pl.program_id(0) * tm

    def issue(r, carry):
        pltpu.make_async_copy(ys_ref.at[pl.ds(pos0_ref[base + r], 1), :],
                              buf_ref.at[0, pl.ds(r, 1), :], sems.at[0]).start()
        pltpu.make_async_copy(ys_ref.at[pl.ds(pos1_ref[base + r], 1), :],
                              buf_ref.at[1, pl.ds(r, 1), :], sems.at[1]).start()
        return carry

    lax.fori_loop(0, tm, issue, 0)
    gate = _sigmoid(_dot(hbf_ref[...], wgate_ref[...]))
    ple = gate * _dot(p_ref[...].astype(BF16), wple_ref[...])
    for k in range(2):
        pltpu.make_async_copy(ys_ref.at[pl.ds(0, tm), :], buf_ref.at[k], sems.at[k]).wait()
    wc = wcol_ref[...]
    moe = wc[:, 0:1] * buf_ref[0] + wc[:, 1:2] * buf_ref[1]
    out = _layer_norm(DEEPNORM_ALPHA * h_ref[...] + moe + ple, g_ref[...], b_ref[...])
    o_ref[...] = out
    obf_ref[...] = out.astype(BF16)


def _ple_combine_ln(pos0, pos1, h_bf, h, p_i, w_gate, w_ple, wcol, g, b, ys, tm):
    s = h.shape[0]
    kern = functools.partial(_ple_kernel, tm=tm)
    tile = lambda width: pl.BlockSpec((tm, width), lambda i, p0, p1: (i, 0))
    full = lambda shape: pl.BlockSpec(shape, lambda i, p0, p1: (0,) * len(shape))
    return pl.pallas_call(
        kern,
        grid_spec=pltpu.PrefetchScalarGridSpec(
            num_scalar_prefetch=2,
            grid=(s // tm,),
            in_specs=[tile(D_MODEL), tile(D_MODEL), tile(PLE_DIM),
                      full((D_MODEL, D_MODEL)), full((PLE_DIM, D_MODEL)),
                      tile(LANES), full((1, D_MODEL)), full((1, D_MODEL)),
                      pl.BlockSpec(memory_space=pl.ANY)],
            out_specs=[tile(D_MODEL), tile(D_MODEL)],
            scratch_shapes=[pltpu.VMEM((2, tm, D_MODEL), F32),
                            pltpu.SemaphoreType.DMA((2,))]),
        out_shape=[jax.ShapeDtypeStruct((s, D_MODEL), F32),
                   jax.ShapeDtypeStruct((s, D_MODEL), BF16)],
        compiler_params=_params("arbitrary"),
        name="ple_combine_ln",
    )(pos0, pos1, h_bf, h, p_i, w_gate, w_ple, wcol, g, b, ys)


def _mixer_layer(h, h_bf, w_main, w_small, b_small, conv_w, conv_b, a_row,
                 dskip_row, normw_row, w_att, w_ssd, ln_g, ln_b):
    s = h.shape[0]
    y = _in_proj(h_bf, w_main, min(s, 1024), 512)
    fcum, dtv = _gates(h_bf, w_small, b_small, min(s, 256))
    frow = jnp.transpose(fcum[:, :ATT_HEADS]).reshape(ATT_HEADS, 1, s)
    att = _attention(y, fcum, frow, min(s, 256))
    ssd = _ssd(y, dtv, conv_w, conv_b, a_row, dskip_row, normw_row)
    return _out_proj_ln(att, ssd, w_att, w_ssd, h, ln_g, ln_b, min(s, 256))


def _channel_layer(h, h_bf, p_i, w_router2, bias_col, w_gate, w_up, w_down,
                   w_ple_gate, w_ple, ln_g, ln_b):
    s = h.shape[0]
    tm = min(s, 256)
    idx, wcol, cnt = _router(h, w_router2, bias_col, tm)
    counts = cnt[:, 0].astype(I32)
    offsets = jnp.cumsum(counts) - counts
    pos0 = offsets[idx[0]] + idx[2]
    pos1 = offsets[idx[1]] + idx[3]
    xs = _dispatch(pos0, pos1, h, tm)
    items = _work_items(counts, tm, 2 * s // tm)
    ys = _moe(items, xs, w_gate, w_up, w_down, tm)
    return _ple_combine_ln(pos0, pos1, h_bf, h, p_i, w_ple_gate, w_ple, wcol,
                           ln_g, ln_b, ys, tm)


def _prep_router(p):
    order = jnp.arange(N_EXPERTS).reshape(N_EXPERT_GROUPS, EXPERTS_PER_GROUP).T.reshape(-1)
    w = p["w_router"].astype(F32)[:, order]
    w = jnp.concatenate([w, jnp.zeros((D_MODEL, LANES - N_EXPERTS), F32)], axis=1)
    w_hi = w.astype(BF16)
    w_lo = (w - w_hi.astype(F32)).astype(BF16)
    bias_col = p["router_bias"].astype(F32)[order].reshape(N_EXPERTS, 1)
    return jnp.stack([w_hi, w_lo]), bias_col


def _prep_layer(p, i):
    w_in = p["w_in"][i]
    c0 = 3 * D_ATT
    c1 = c0 + ATT_HEADS
    c2 = c1 + D_SSD
    c3 = c2 + D_CONV
    w_main = jnp.concatenate([w_in[:, :c0], w_in[:, c1:c3]], axis=1).astype(BF16)
    pad_f = jnp.zeros((D_MODEL, LANES - ATT_HEADS), F32)
    pad_dt = jnp.zeros((D_MODEL, LANES - SSD_HEADS), F32)
    w_small = jnp.concatenate([w_in[:, c0:c1], pad_f, w_in[:, c3:], pad_dt],
                              axis=1).astype(BF16)
    b_small = jnp.concatenate([p["b_forget"][i], jnp.zeros((LANES - ATT_HEADS,), F32),
                               p["dt_bias"][i], jnp.zeros((LANES - SSD_HEADS,), F32)]
                              ).reshape(1, 2 * LANES)
    a_row = jnp.concatenate([-jnp.exp(p["a_log"][i].astype(F32)),
                             jnp.zeros((LANES - SSD_HEADS,), F32)]).reshape(1, LANES)
    dskip_row = jnp.repeat(p["d_skip"][i].astype(F32), SSD_HEAD_DIM).reshape(1, D_SSD)
    w_out = p["w_out"][i].astype(BF16)
    mixer = (w_main, w_small, b_small, p["conv_w"][i], p["conv_b"][i].reshape(1, D_CONV),
             a_row, dskip_row, p["ssd_norm_w"][i].reshape(1, D_SSD),
             w_out[:D_ATT], w_out[D_ATT:],
             p["ln1_g"][i].reshape(1, D_MODEL), p["ln1_b"][i].reshape(1, D_MODEL))
    channel = (p["w_gate"][i].astype(BF16), p["w_up"][i].astype(BF16),
               p["w_down"][i].astype(BF16), p["w_ple_gate"][i].astype(BF16),
               p["w_ple"][i].astype(BF16),
               p["ln2_g"][i].reshape(1, D_MODEL), p["ln2_b"][i].reshape(1, D_MODEL))
    return {"mixer": mixer, "channel": channel}


def kernel(x, p, w_in, b_forget, conv_w, conv_b, dt_bias, a_log, d_skip, ssd_norm_w,
           w_out, ln1_g, ln1_b, w_router, router_bias, w_gate, w_up, w_down, w_ple,
           w_ple_gate, ln2_g, ln2_b):
    params = dict(w_in=w_in, b_forget=b_forget, conv_w=conv_w, conv_b=conv_b,
                  dt_bias=dt_bias, a_log=a_log, d_skip=d_skip, ssd_norm_w=ssd_norm_w,
                  w_out=w_out, ln1_g=ln1_g, ln1_b=ln1_b, w_router=w_router,
                  router_bias=router_bias, w_gate=w_gate, w_up=w_up, w_down=w_down,
                  w_ple=w_ple, w_ple_gate=w_ple_gate, ln2_g=ln2_g, ln2_b=ln2_b)
    batch, s, _ = x.shape
    w_router2, bias_col = _prep_router(params)
    outs = []
    for bi in range(batch):
        h = x[bi]
        h_bf = h.astype(BF16)
        for i in range(w_in.shape[0]):
            prep = _prep_layer(params, i)
            h, h_bf = _mixer_layer(h, h_bf, *prep["mixer"])
            w_g, w_u, w_d, w_pg, w_pl, g2, b2 = prep["channel"]
            h, h_bf = _channel_layer(h, h_bf, p[i, bi], w_router2, bias_col,
                                     w_g, w_u, w_d, w_pg, w_pl, g2, b2)
        outs.append(h)
    return jnp.stack(outs)
```

```python
import functools
import math

import jax
import jax.numpy as jnp
from jax import lax
from jax.experimental import pallas as pl
from jax.experimental.pallas import tpu as pltpu

F32 = jnp.float32
BF16 = jnp.bfloat16
I32 = jnp.int32

D_MODEL = 2048
ATT_HEADS = 8
ATT_HEAD_DIM = 128
D_ATT = ATT_HEADS * ATT_HEAD_DIM
SSD_HEADS = 16
SSD_HEAD_DIM = 64
D_SSD = SSD_HEADS * SSD_HEAD_DIM
SSD_GROUPS = 4
HEADS_PER_GROUP = SSD_HEADS // SSD_GROUPS
SSD_STATE = 128
D_GROUP = D_SSD // SSD_GROUPS
CONV_WIDTH = 4
D_BC = SSD_GROUPS * SSD_STATE
D_CONV = D_SSD + 2 * D_BC
N_EXPERTS = 16
N_EXPERT_GROUPS = 4
EXPERTS_PER_GROUP = N_EXPERTS // N_EXPERT_GROUPS
D_EXPERT = 1024
PLE_DIM = 256
DEPTH = 2
DEEPNORM_ALPHA = (2 * DEPTH) ** 0.25
LN_EPS = 1e-5
RMS_EPS = 1e-5
LOG2_E = math.log2(math.e)

LANES = 128
SUBLANES = 8
VMEM_LIMIT_BYTES = 56 * 1024 * 1024

D_MAIN = 3 * D_ATT + D_SSD + D_CONV
COL_Z = 3 * D_ATT
COL_XBC = 3 * D_ATT + D_SSD

SSD_CHUNK = 128
CONV_HALO = SUBLANES


def _params(*sem):
    return pltpu.CompilerParams(dimension_semantics=sem,
                                vmem_limit_bytes=VMEM_LIMIT_BYTES)


def _softplus(u):
    return jnp.maximum(u, 0.0) + jnp.log1p(jnp.exp(-jnp.abs(u)))


def _sigmoid(u):
    return 1.0 / (1.0 + jnp.exp(-u))


def _silu(u):
    return u * _sigmoid(u)


def _split3(v):
    hi = v.astype(BF16)
    r1 = v - hi.astype(F32)
    mid = r1.astype(BF16)
    lo = (r1 - mid.astype(F32)).astype(BF16)
    return hi, mid, lo


def _dot(a, b):
    return jnp.dot(a, b, preferred_element_type=F32)


def _exact_left_dot(ones_bf16, v):
    hi, mid, lo = _split3(v)
    return _dot(ones_bf16, hi) + (_dot(ones_bf16, mid) + _dot(ones_bf16, lo))


def _proj_kernel(x_ref, w_ref, o_ref, *, q_blocks):
    scale = jnp.where(pl.program_id(1) < q_blocks, LOG2_E * ATT_HEAD_DIM ** -0.5, 1.0)
    o_ref[...] = (_dot(x_ref[...], w_ref[...]) * scale).astype(o_ref.dtype)


def _in_proj(h_bf, w_main, tm, tn):
    s = h_bf.shape[0]
    n = w_main.shape[1]
    return pl.pallas_call(
        functools.partial(_proj_kernel, q_blocks=D_ATT // tn),
        grid=(s // tm, n // tn),
        in_specs=[pl.BlockSpec((tm, D_MODEL), lambda i, j: (i, 0)),
                  pl.BlockSpec((D_MODEL, tn), lambda i, j: (0, j))],
        out_specs=pl.BlockSpec((tm, tn), lambda i, j: (i, j)),
        out_shape=jax.ShapeDtypeStruct((s, n), BF16),
        compiler_params=_params("parallel", "arbitrary"),
        name="in_proj",
    )(h_bf, w_main)


def _gates_kernel(x_ref, w_ref, b_ref, tri_ref, sel_ref, ones_ref, faug_ref, dt_ref,
                  carry_ref):
    @pl.when(pl.program_id(0) == 0)
    def _():
        carry_ref[...] = jnp.zeros_like(carry_ref)

    v = _dot(x_ref[...], w_ref[...]) + b_ref[...]
    log_f = -_softplus(-v[:, :LANES])
    dt_ref[...] = _softplus(v[:, LANES:])
    fcum = _exact_left_dot(tri_ref[...], log_f) + carry_ref[...]
    carry_ref[...] = fcum[-1:, :]
    hi, mid, lo = _split3(fcum * LOG2_E)
    aug = (_dot(hi, sel_ref[0]) + _dot(mid, sel_ref[1]) + _dot(lo, sel_ref[2])
           + ones_ref[...])
    faug_ref[...] = aug.astype(BF16)


def _forget_layout():
    import numpy as np
    sel = np.zeros((3, LANES, 2 * D_ATT), np.float32)
    ones = np.zeros((1, 2 * D_ATT), np.float32)
    for h in range(ATT_HEADS):
        for k in range(3):
            sel[k, h, ATT_HEAD_DIM * h + k] = 1.0
            sel[k, h, D_ATT + ATT_HEAD_DIM * h + 3 + k] = -1.0
            ones[0, ATT_HEAD_DIM * h + 3 + k] = 1.0
            ones[0, D_ATT + ATT_HEAD_DIM * h + k] = 1.0
    return jnp.asarray(sel, BF16), jnp.asarray(ones, F32)


def _gates(h_bf, w_small, b_small, tm):
    s = h_bf.shape[0]
    tri = jnp.tril(jnp.ones((tm, tm), F32)).astype(BF16)
    sel, ones = _forget_layout()
    return pl.pallas_call(
        _gates_kernel,
        grid=(s // tm,),
        in_specs=[pl.BlockSpec((tm, D_MODEL), lambda i: (i, 0)),
                  pl.BlockSpec((D_MODEL, 2 * LANES), lambda i: (0, 0)),
                  pl.BlockSpec((1, 2 * LANES), lambda i: (0, 0)),
                  pl.BlockSpec((tm, tm), lambda i: (0, 0)),
                  pl.BlockSpec((3, LANES, 2 * D_ATT), lambda i: (0, 0, 0)),
                  pl.BlockSpec((1, 2 * D_ATT), lambda i: (0, 0))],
        out_specs=[pl.BlockSpec((tm, 2 * D_ATT), lambda i: (i, 0)),
                   pl.BlockSpec((tm, LANES), lambda i: (i, 0))],
        out_shape=[jax.ShapeDtypeStruct((s, 2 * D_ATT), BF16),
                   jax.ShapeDtypeStruct((s, LANES), F32)],
        scratch_shapes=[pltpu.VMEM((1, LANES), F32)],
        compiler_params=_params("arbitrary"),
        name="gates",
    )(h_bf, w_small, b_small, tri, sel, ones)


def _attn_kernel(q_ref, fq_ref, k_ref, fk_ref, v_ref, o_ref, vt_ref, sa_ref, sb_ref,
                 *, tq, tk, nh):
    assert tq == 2 * tk
    i = pl.program_id(1)
    n_chunks = vt_ref.shape[1]
    dh = ATT_HEAD_DIM
    heads = range(nh)

    @pl.when(i == 0)
    def _():
        for hd in heads:
            for c in range(n_chunks):
                blk = v_ref[c * tk:(c + 1) * tk, hd * dh:(hd + 1) * dh]
                vt_ref[hd, c] = blk.astype(F32).T.astype(BF16)

    q_t = []
    for hd in heads:
        cols = slice(hd * dh, (hd + 1) * dh)
        q_aug = jnp.concatenate([q_ref[:, cols], fq_ref[:, cols]], axis=1)
        q_t.append(q_aug.astype(F32).T.astype(BF16))
    kv_pos = lax.broadcasted_iota(I32, (tk, tq), 0)
    q_pos = lax.broadcasted_iota(I32, (tk, tq), 1)

    def logits(hd, c):
        start = pl.multiple_of(c * tk, tk)
        cols = slice(hd * dh, (hd + 1) * dh)
        k_aug = jnp.concatenate([k_ref[pl.ds(start, tk), cols],
                                 fk_ref[pl.ds(start, tk), cols]], axis=1)
        return _dot(k_aug, q_t[hd])

    def update(st, hd, c, carry, diag_offset):
        m, l, acc = carry
        if diag_offset is not None:
            st = jnp.where(kv_pos + diag_offset <= q_pos, st, -jnp.inf)
        m_new = jnp.maximum(m, jnp.max(st, axis=0, keepdims=True))
        p = jnp.exp2(st - m_new)
        alpha = jnp.exp2(m - m_new)
        l = alpha * l + jnp.sum(p, axis=0, keepdims=True)
        acc = alpha * acc + _dot(vt_ref[hd, c], p.astype(BF16))
        return m_new, l, acc

    first = 2 * i
    for hd in heads:
        sa_ref[hd] = logits(hd, 0)

    def pair(t, carries):
        out = []
        for hd in heads:
            sb_ref[hd] = logits(hd, 2 * t + 1)
        for hd in heads:
            out.append(update(sa_ref[hd], hd, 2 * t, carries[hd], None))
        for hd in heads:
            sa_ref[hd] = logits(hd, 2 * t + 2)
        return tuple(update(sb_ref[hd], hd, 2 * t + 1, out[hd], None) for hd in heads)

    init = (jnp.full((1, tq), -jnp.inf, F32), jnp.zeros((1, tq), F32),
            jnp.zeros((dh, tq), F32))
    carries = lax.fori_loop(0, i, pair, tuple(init for _ in heads))
    for hd in heads:
        sb_ref[hd] = logits(hd, first + 1)
    for hd in heads:
        carry = update(sa_ref[hd], hd, first, carries[hd], 0)
        _, l, acc = update(sb_ref[hd], hd, first + 1, carry, tk)
        o_ref[:, hd * dh:(hd + 1) * dh] = (acc / l).T.astype(o_ref.dtype)


def _attention(y, faug, tq, tk, nh):
    s = y.shape[0]
    width = nh * ATT_HEAD_DIM
    groups = ATT_HEADS // nh
    kern = functools.partial(_attn_kernel, tq=tq, tk=tk, nh=nh)
    q_block = lambda g0: pl.BlockSpec((tq, width), lambda g, i: (i, g0 + g))
    kv_block = lambda g0: pl.BlockSpec((s, width), lambda g, i: (0, g0 + g))
    return pl.pallas_call(
        kern,
        grid=(groups, s // tq),
        in_specs=[q_block(0), q_block(0), kv_block(groups), kv_block(groups),
                  kv_block(2 * groups)],
        out_specs=pl.BlockSpec((tq, width), lambda g, i: (i, g)),
        out_shape=jax.ShapeDtypeStruct((s, D_ATT), BF16),
        scratch_shapes=[pltpu.VMEM((nh, s // tk, ATT_HEAD_DIM, tk), BF16),
                        pltpu.VMEM((nh, tk, tq), F32), pltpu.VMEM((nh, tk, tq), F32)],
        compiler_params=_params("arbitrary", "arbitrary"),
        name="fox_attention",
    )(y, faug, y, faug, y)


def _expand_heads(w, n_heads):
    length = w.shape[0]
    lane = lax.broadcasted_iota(I32, (length, LANES), 1)
    blocks = []
    for m in range(n_heads // 2):
        blocks.append(jnp.where(lane < SSD_HEAD_DIM, w[:, 2 * m:2 * m + 1],
                                w[:, 2 * m + 1:2 * m + 2]))
    return jnp.concatenate(blocks, axis=1)


def _ssd_kernel(xbc_ref, z_ref, dt_ref, convw_ref, convb_ref, a_ref, dskip_ref,
                normw_ref, tri_ref, o_ref, ubuf_ref, state_ref):
    length = SSD_CHUNK

    @pl.when(pl.program_id(0) == 0)
    def _():
        ubuf_ref[0:CONV_HALO, :] = jnp.zeros((CONV_HALO, D_CONV), F32)
        state_ref[...] = jnp.zeros_like(state_ref)

    ubuf_ref[CONV_HALO:CONV_HALO + length, :] = xbc_ref[...].astype(F32)
    conv = convb_ref[...]
    for k in range(CONV_WIDTH):
        off = CONV_HALO - (CONV_WIDTH - 1) + k
        conv = conv + convw_ref[k:k + 1, :] * ubuf_ref[off:off + length, :]
    tail = ubuf_ref[length:length + CONV_HALO, :]
    ubuf_ref[0:CONV_HALO, :] = tail
    xc = _silu(conv)
    xs = xc[:, :D_SSD]

    dt = dt_ref[...]
    da = dt * a_ref[...]
    acum = _exact_left_dot(tri_ref[...], da)
    acum_t = acum.T
    a_last = acum[length - 1:length, :]
    w_off = _expand_heads(jnp.exp(acum), SSD_HEADS)
    w_state = _expand_heads(jnp.exp(a_last - acum) * dt, SSD_HEADS)
    w_dt = _expand_heads(dt, SSD_HEADS)
    chunk_decay = _expand_heads(jnp.exp(a_last), SSD_HEADS)

    x_dt = (xs * w_dt).astype(BF16)
    x_state = (xs * w_state).astype(BF16)
    row = lax.broadcasted_iota(I32, (length, length), 0)
    col = lax.broadcasted_iota(I32, (length, length), 1)
    causal = col <= row
    grp_lane = lax.broadcasted_iota(I32, (length, D_GROUP), 1) // SSD_HEAD_DIM

    y_parts = []
    for g in range(SSD_GROUPS):
        b_g = xc[:, D_SSD + g * SSD_STATE:D_SSD + (g + 1) * SSD_STATE].astype(BF16)
        c_g = xc[:, D_SSD + D_BC + g * SSD_STATE:
                 D_SSD + D_BC + (g + 1) * SSD_STATE].astype(BF16)
        cb = lax.dot_general(c_g, b_g, (((1,), (1,)), ((), ())),
                             preferred_element_type=F32)
        cols = slice(g * D_GROUP, (g + 1) * D_GROUP)
        x_dt_g = x_dt[:, cols]
        m_blocks, x_blocks = [], []
        for r in range(HEADS_PER_GROUP):
            hd = g * HEADS_PER_GROUP + r
            seg = acum[:, hd:hd + 1] - acum_t[hd:hd + 1, :]
            decay = jnp.exp(jnp.where(causal, seg, -jnp.inf))
            m_blocks.append((cb * decay).astype(BF16))
            x_blocks.append(jnp.where(grp_lane == r, x_dt_g, jnp.zeros_like(x_dt_g)))
        y_diag = _dot(jnp.concatenate(m_blocks, axis=1),
                      jnp.concatenate(x_blocks, axis=0))
        st = state_ref[g]
        y_off = _dot(c_g, st.astype(BF16)) * w_off[:, cols]
        y_parts.append(y_diag + y_off)
        new_st = lax.dot_general(b_g, x_state[:, cols], (((0,), (0,)), ((), ())),
                                 preferred_element_type=F32)
        state_ref[g] = st * chunk_decay[:, cols] + new_st

    y = jnp.concatenate(y_parts, axis=1) + dskip_ref[...] * xs
    yg = y * _silu(z_ref[...].astype(F32))
    outs = []
    for g in range(SSD_GROUPS):
        blk = yg[:, g * D_GROUP:(g + 1) * D_GROUP]
        ms = jnp.mean(blk * blk, axis=1, keepdims=True)
        outs.append(blk * lax.rsqrt(ms + RMS_EPS))
    o_ref[...] = (jnp.concatenate(outs, axis=1) * normw_ref[...]).astype(o_ref.dtype)


def _ssd(y, dtv, conv_w, conv_b, a_row, dskip_row, normw_row):
    s = y.shape[0]
    length = SSD_CHUNK
    tri = jnp.tril(jnp.ones((length, length), F32)).astype(BF16)
    full = lambda shape: pl.BlockSpec(shape, lambda c: (0,) * len(shape))
    return pl.pallas_call(
        _ssd_kernel,
        grid=(s // length,),
        in_specs=[pl.BlockSpec((length, D_CONV), lambda c: (c, COL_XBC // D_CONV)),
                  pl.BlockSpec((length, D_SSD), lambda c: (c, COL_Z // D_SSD)),
                  pl.BlockSpec((length, LANES), lambda c: (c, 0)),
                  full((CONV_WIDTH, D_CONV)), full((1, D_CONV)), full((1, LANES)),
                  full((1, D_SSD)), full((1, D_SSD)), full((length, length))],
        out_specs=pl.BlockSpec((length, D_SSD), lambda c: (c, 0)),
        out_shape=jax.ShapeDtypeStruct((s, D_SSD), BF16),
        scratch_shapes=[pltpu.VMEM((length + CONV_HALO, D_CONV), F32),
                        pltpu.VMEM((SSD_GROUPS, SSD_STATE, D_GROUP), F32)],
        compiler_params=_params("arbitrary"),
        name="ssd",
    )(y, y, dtv, conv_w, conv_b, a_row, dskip_row, normw_row, tri)


def _layer_norm(xf, g, b):
    mu = jnp.mean(xf, axis=1, keepdims=True)
    xc = xf - mu
    var = jnp.mean(xc * xc, axis=1, keepdims=True)
    return xc * lax.rsqrt(var + LN_EPS) * g + b


def _out_proj_kernel(att_ref, ssd_ref, wa_ref, ws_ref, h_ref, g_ref, b_ref,
                     o_ref, obf_ref):
    mix = _dot(att_ref[...], wa_ref[...]) + _dot(ssd_ref[...], ws_ref[...])
    out = _layer_norm(DEEPNORM_ALPHA * h_ref[...] + mix, g_ref[...], b_ref[...])
    o_ref[...] = out
    obf_ref[...] = out.astype(BF16)


def _out_proj_ln(att, ssd, w_att, w_ssd, h, g, b, tm):
    s = h.shape[0]
    full = lambda shape: pl.BlockSpec(shape, lambda i: (0,) * len(shape))
    return pl.pallas_call(
        _out_proj_kernel,
        grid=(s // tm,),
        in_specs=[pl.BlockSpec((tm, D_ATT), lambda i: (i, 0)),
                  pl.BlockSpec((tm, D_SSD), lambda i: (i, 0)),
                  full((D_ATT, D_MODEL)), full((D_SSD, D_MODEL)),
                  pl.BlockSpec((tm, D_MODEL), lambda i: (i, 0)),
                  full((1, D_MODEL)), full((1, D_MODEL))],
        out_specs=[pl.BlockSpec((tm, D_MODEL), lambda i: (i, 0)),
                   pl.BlockSpec((tm, D_MODEL), lambda i: (i, 0))],
        out_shape=[jax.ShapeDtypeStruct((s, D_MODEL), F32),
                   jax.ShapeDtypeStruct((s, D_MODEL), BF16)],
        compiler_params=_params("parallel"),
        name="out_proj_ln",
    )(att, ssd, w_att, w_ssd, h, g, b)


def _router_kernel(h_ref, wr_ref, bias_ref, triu_ref, idx_ref, wcol_ref, cnt_ref,
                   carry_ref, *, tm):
    @pl.when(pl.program_id(0) == 0)
    def _():
        carry_ref[...] = jnp.zeros_like(carry_ref)

    h = h_ref[...]
    h_hi = h.astype(BF16)
    h_lo = (h - h_hi.astype(F32)).astype(BF16)
    logits = _dot(h_hi, wr_ref[0]) + (_dot(h_hi, wr_ref[1]) + _dot(h_lo, wr_ref[0]))
    lt = logits.T
    scores = _sigmoid(lt[:N_EXPERTS, :])
    biased = scores + bias_ref[...]
    npg = EXPERTS_PER_GROUP
    s_r = [scores[npg * r:npg * (r + 1), :] for r in range(npg)]
    b_r = [biased[npg * r:npg * (r + 1), :] for r in range(npg)]

    gs = None
    for a in range(npg):
        for b in range(a + 1, npg):
            pair = b_r[a] + b_r[b]
            gs = pair if gs is None else jnp.maximum(gs, pair)
    best = jnp.zeros((1, tm), I32)
    top = gs[0:1, :]
    for g in range(1, N_EXPERT_GROUPS):
        better = gs[g:g + 1, :] > top
        best = jnp.where(better, g, best)
        top = jnp.where(better, gs[g:g + 1, :], top)

    def pick_group(a):
        out = a[0:1, :]
        for g in range(1, N_EXPERT_GROUPS):
            out = jnp.where(best == g, a[g:g + 1, :], out)
        return out

    sb = [pick_group(b) for b in b_r]
    ss = [pick_group(s) for s in s_r]
    first = jnp.zeros((1, tm), I32)
    fmax = sb[0]
    for r in range(1, npg):
        better = sb[r] > fmax
        first = jnp.where(better, r, first)
        fmax = jnp.where(better, sb[r], fmax)
    second = jnp.zeros((1, tm), I32)
    smax = jnp.full((1, tm), -jnp.inf, F32)
    for r in range(npg):
        cand = jnp.where(first == r, -jnp.inf, sb[r])
        better = cand > smax
        second = jnp.where(better, r, second)
        smax = jnp.where(better, cand, smax)

    def pick_expert(which):
        out = ss[0]
        for r in range(1, npg):
            out = jnp.where(which == r, ss[r], out)
        return out

    a0 = pick_expert(first)
    a1 = pick_expert(second)
    denom = a0 + a1
    e0 = best * npg + first
    e1 = best * npg + second

    erow = lax.broadcasted_iota(I32, (N_EXPERTS, tm), 0)
    hit0 = erow == e0
    hit1 = erow == e1
    onehot = jnp.where(hit0, 1.0, 0.0) + jnp.where(hit1, 1.0, 0.0)
    cum = _dot(onehot.astype(BF16), triu_ref[...])
    carry = carry_ref[...][:, 0:1]
    before = cum - onehot + carry
    rank0 = jnp.sum(jnp.where(hit0, before, 0.0), axis=0, keepdims=True)
    rank1 = jnp.sum(jnp.where(hit1, before, 0.0), axis=0, keepdims=True)
    new_carry = carry_ref[...] + cum[:, tm - 1:tm]
    carry_ref[...] = new_carry
    cnt_ref[...] = new_carry

    zeros_i = jnp.zeros((SUBLANES - 4, tm), I32)
    idx_ref[...] = jnp.concatenate(
        [e0, e1, rank0.astype(I32), rank1.astype(I32), zeros_i], axis=0)
    wrows = jnp.concatenate([a0 / denom, a1 / denom,
                             jnp.zeros((LANES - 2, tm), F32)], axis=0)
    wcol_ref[...] = wrows.T


def _router(h, w_router2, bias_col, tm):
    s = h.shape[0]
    triu = jnp.triu(jnp.ones((tm, tm), F32)).astype(BF16)
    kern = functools.partial(_router_kernel, tm=tm)
    return pl.pallas_call(
        kern,
        grid=(s // tm,),
        in_specs=[pl.BlockSpec((tm, D_MODEL), lambda i: (i, 0)),
                  pl.BlockSpec((2, D_MODEL, LANES), lambda i: (0, 0, 0)),
                  pl.BlockSpec((N_EXPERTS, 1), lambda i: (0, 0)),
                  pl.BlockSpec((tm, tm), lambda i: (0, 0))],
        out_specs=[pl.BlockSpec((SUBLANES, tm), lambda i: (0, i)),
                   pl.BlockSpec((tm, LANES), lambda i: (i, 0)),
                   pl.BlockSpec((N_EXPERTS, LANES), lambda i: (0, 0))],
        out_shape=[jax.ShapeDtypeStruct((SUBLANES, s), I32),
                   jax.ShapeDtypeStruct((s, LANES), F32),
                   jax.ShapeDtypeStruct((N_EXPERTS, LANES), F32)],
        scratch_shapes=[pltpu.VMEM((N_EXPERTS, LANES), F32)],
        compiler_params=_params("arbitrary"),
        name="router",
    )(h, w_router2, bias_col, triu)


def _work_items(counts, tm, n_tiles):
    ends = jnp.cumsum(counts)
    starts = ends - counts
    first_tile = starts // tm
    last_tile = (ends - 1) // tm
    n_e = jnp.where(counts > 0, last_tile - first_tile + 1, 0)
    item_end = jnp.cumsum(n_e)
    item_start = item_end - n_e
    n_items = n_tiles + N_EXPERTS - 1
    w = jnp.arange(n_items, dtype=I32)
    total = item_end[-1]
    valid = w < total
    wc = jnp.minimum(w, total - 1)
    e_w = jnp.minimum(jnp.sum((item_end[None, :] <= wc[:, None]).astype(I32), axis=1),
                      N_EXPERTS - 1)
    tile_w = first_tile[e_w] + (wc - item_start[e_w])
    lo = jnp.clip(starts[e_w] - tile_w * tm, 0, tm)
    hi = jnp.clip(ends[e_w] - tile_w * tm, 0, tm)
    lo = jnp.where(valid, lo, 0)
    hi = jnp.where(valid, hi, 0)
    return tile_w.astype(I32), e_w, lo.astype(I32), hi.astype(I32)


def _dispatch_kernel(pos0_ref, pos1_ref, h_ref, xs_ref, sems, *, tm):
    base = pl.program_id(0) * tm

    def issue(r, carry):
        src = h_ref.at[pl.ds(r, 1), :]
        pltpu.make_async_copy(src, xs_ref.at[pl.ds(pos0_ref[base + r], 1), :],
                              sems.at[0]).start()
        pltpu.make_async_copy(src, xs_ref.at[pl.ds(pos1_ref[base + r], 1), :],
                              sems.at[1]).start()
        return carry

    lax.fori_loop(0, tm, issue, 0)
    for k in range(2):
        pltpu.make_async_copy(h_ref, xs_ref.at[pl.ds(0, tm), :], sems.at[k]).wait()


def _dispatch(pos0, pos1, h, tm):
    s = h.shape[0]
    kern = functools.partial(_dispatch_kernel, tm=tm)
    return pl.pallas_call(
        kern,
        grid_spec=pltpu.PrefetchScalarGridSpec(
            num_scalar_prefetch=2,
            grid=(s // tm,),
            in_specs=[pl.BlockSpec((tm, D_MODEL), lambda i, p0, p1: (i, 0))],
            out_specs=pl.BlockSpec(memory_space=pl.ANY),
            scratch_shapes=[pltpu.SemaphoreType.DMA((2,))]),
        out_shape=jax.ShapeDtypeStruct((2 * s, D_MODEL), F32),
        compiler_params=_params("arbitrary"),
        name="dispatch",
    )(pos0, pos1, h)


def _moe_kernel(tile_ref, exp_ref, lo_ref, hi_ref, x_ref, wg_ref, wu_ref, wd_ref,
                o_ref, *, tm):
    w = pl.program_id(0)
    lo = lo_ref[w]
    hi = hi_ref[w]
    fresh = jnp.logical_or(w == 0, tile_ref[w] != tile_ref[jnp.maximum(w - 1, 0)])

    @pl.when(hi > lo)
    def _():
        x = x_ref[...].astype(BF16)
        hid = _silu(_dot(x, wg_ref[0])) * _dot(x, wu_ref[0])
        y = _dot(hid.astype(BF16), wd_ref[0])
        row = lax.broadcasted_iota(I32, (tm, 1), 0)
        keep = jnp.logical_and(row >= lo, row < hi)

        @pl.when(fresh)
        def _():
            o_ref[...] = jnp.where(keep, y, 0.0)

        @pl.when(jnp.logical_not(fresh))
        def _():
            o_ref[...] = jnp.where(keep, y, o_ref[...])


def _moe(items, xs, w_gate, w_up, w_down, tm):
    rows = xs.shape[0]
    n_items = items[0].shape[0]
    kern = functools.partial(_moe_kernel, tm=tm)
    wspec = lambda shape: pl.BlockSpec(
        (1,) + shape, lambda w, tile, exp, lo, hi: (exp[w], 0, 0))
    return pl.pallas_call(
        kern,
        grid_spec=pltpu.PrefetchScalarGridSpec(
            num_scalar_prefetch=4,
            grid=(n_items,),
            in_specs=[pl.BlockSpec((tm, D_MODEL), lambda w, tile, exp, lo, hi: (tile[w], 0)),
                      wspec((D_MODEL, D_EXPERT)), wspec((D_MODEL, D_EXPERT)),
                      wspec((D_EXPERT, D_MODEL))],
            out_specs=pl.BlockSpec((tm, D_MODEL), lambda w, tile, exp, lo, hi: (tile[w], 0))),
        out_shape=jax.ShapeDtypeStruct((rows, D_MODEL), F32),
        compiler_params=_params("arbitrary"),
        name="moe_experts",
    )(*items, xs, w_gate, w_up, w_down)


def _ple_kernel(pos0_ref, pos1_ref, hbf_ref, h_ref, p_ref, wgate_ref, wple_ref,
                wcol_ref, g_ref, b_ref, ys_ref, o_ref, obf_ref, buf_ref, sems, *, tm):
    base = pl.program_id(0) * tm

    def issue(r, carry):
        pltpu.make_async_copy(ys_ref.at[pl.ds(pos0_ref[base + r], 1), :],
                              buf_ref.at[0, pl.ds(r, 1), :], sems.at[0]).start()
        pltpu.make_async_copy(ys_ref.at[pl.ds(pos1_ref[base + r], 1), :],
                              buf_ref.at[1, pl.ds(r, 1), :], sems.at[1]).start()
        return carry

    lax.fori_loop(0, tm, issue, 0)
    gate = _sigmoid(_dot(hbf_ref[...], wgate_ref[...]))
    ple = gate * _dot(p_ref[...].astype(BF16), wple_ref[...])
    for k in range(2):
        pltpu.make_async_copy(ys_ref.at[pl.ds(0, tm), :], buf_ref.at[k], sems.at[k]).wait()
    wc = wcol_ref[...]
    moe = wc[:, 0:1] * buf_ref[0] + wc[:, 1:2] * buf_ref[1]
    out = _layer_norm(DEEPNORM_ALPHA * h_ref[...] + moe + ple, g_ref[...], b_ref[...])
    o_ref[...] = out
    obf_ref[...] = out.astype(BF16)


def _ple_combine_ln(pos0, pos1, h_bf, h, p_i, w_gate, w_ple, wcol, g, b, ys, tm):
    s = h.shape[0]
    kern = functools.partial(_ple_kernel, tm=tm)
    tile = lambda width: pl.BlockSpec((tm, width), lambda i, p0, p1: (i, 0))
    full = lambda shape: pl.BlockSpec(shape, lambda i, p0, p1: (0,) * len(shape))
    return pl.pallas_call(
        kern,
        grid_spec=pltpu.PrefetchScalarGridSpec(
            num_scalar_prefetch=2,
            grid=(s // tm,),
            in_specs=[tile(D_MODEL), tile(D_MODEL), tile(PLE_DIM),
                      full((D_MODEL, D_MODEL)), full((PLE_DIM, D_MODEL)),
                      tile(LANES), full((1, D_MODEL)), full((1, D_MODEL)),
                      pl.BlockSpec(memory_space=pl.ANY)],
            out_specs=[tile(D_MODEL), tile(D_MODEL)],
            scratch_shapes=[pltpu.VMEM((2, tm, D_MODEL), F32),
                            pltpu.SemaphoreType.DMA((2,))]),
        out_shape=[jax.ShapeDtypeStruct((s, D_MODEL), F32),
                   jax.ShapeDtypeStruct((s, D_MODEL), BF16)],
        compiler_params=_params("arbitrary"),
        name="ple_combine_ln",
    )(pos0, pos1, h_bf, h, p_i, w_gate, w_ple, wcol, g, b, ys)


def _mixer_layer(h, h_bf, w_main, w_small, b_small, conv_w, conv_b, a_row,
                 dskip_row, normw_row, w_att, w_ssd, ln_g, ln_b):
    s = h.shape[0]
    y = _in_proj(h_bf, w_main, min(s, 1024), 512)
    faug, dtv = _gates(h_bf, w_small, b_small, min(s, 256))
    att = _attention(y, faug, min(s, 512), min(s, 256), 2)
    ssd = _ssd(y, dtv, conv_w, conv_b, a_row, dskip_row, normw_row)
    return _out_proj_ln(att, ssd, w_att, w_ssd, h, ln_g, ln_b, min(s, 256))


def _channel_layer(h, h_bf, p_i, w_router2, bias_col, w_gate, w_up, w_down,
                   w_ple_gate, w_ple, ln_g, ln_b):
    s = h.shape[0]
    tm = min(s, 256)
    idx, wcol, cnt = _router(h, w_router2, bias_col, tm)
    counts = cnt[:, 0].astype(I32)
    offsets = jnp.cumsum(counts) - counts
    pos0 = offsets[idx[0]] + idx[2]
    pos1 = offsets[idx[1]] + idx[3]
    xs = _dispatch(pos0, pos1, h, tm)
    items = _work_items(counts, tm, 2 * s // tm)
    ys = _moe(items, xs, w_gate, w_up, w_down, tm)
    return _ple_combine_ln(pos0, pos1, h_bf, h, p_i, w_ple_gate, w_ple, wcol,
                           ln_g, ln_b, ys, tm)


def _prep_router(p):
    order = jnp.arange(N_EXPERTS).reshape(N_EXPERT_GROUPS, EXPERTS_PER_GROUP).T.reshape(-1)
    w = p["w_router"].astype(F32)[:, order]
    w = jnp.concatenate([w, jnp.zeros((D_MODEL, LANES - N_EXPERTS), F32)], axis=1)
    w_hi = w.astype(BF16)
    w_lo = (w - w_hi.astype(F32)).astype(BF16)
    bias_col = p["router_bias"].astype(F32)[order].reshape(N_EXPERTS, 1)
    return jnp.stack([w_hi, w_lo]), bias_col


def _prep_layer(p, i):
    w_in = p["w_in"][i]
    c0 = 3 * D_ATT
    c1 = c0 + ATT_HEADS
    c2 = c1 + D_SSD
    c3 = c2 + D_CONV
    w_main = jnp.concatenate([w_in[:, :c0], w_in[:, c1:c3]], axis=1).astype(BF16)
    pad_f = jnp.zeros((D_MODEL, LANES - ATT_HEADS), F32)
    pad_dt = jnp.zeros((D_MODEL, LANES - SSD_HEADS), F32)
    w_small = jnp.concatenate([w_in[:, c0:c1], pad_f, w_in[:, c3:], pad_dt],
                              axis=1).astype(BF16)
    b_small = jnp.concatenate([p["b_forget"][i], jnp.zeros((LANES - ATT_HEADS,), F32),
                               p["dt_bias"][i], jnp.zeros((LANES - SSD_HEADS,), F32)]
                              ).reshape(1, 2 * LANES)
    a_row = jnp.concatenate([-jnp.exp(p["a_log"][i].astype(F32)),
                             jnp.zeros((LANES - SSD_HEADS,), F32)]).reshape(1, LANES)
    dskip_row = jnp.repeat(p["d_skip"][i].astype(F32), SSD_HEAD_DIM).reshape(1, D_SSD)
    w_out = p["w_out"][i].astype(BF16)
    mixer = (w_main, w_small, b_small, p["conv_w"][i], p["conv_b"][i].reshape(1, D_CONV),
             a_row, dskip_row, p["ssd_norm_w"][i].reshape(1, D_SSD),
             w_out[:D_ATT], w_out[D_ATT:],
             p["ln1_g"][i].reshape(1, D_MODEL), p["ln1_b"][i].reshape(1, D_MODEL))
    channel = (p["w_gate"][i].astype(BF16), p["w_up"][i].astype(BF16),
               p["w_down"][i].astype(BF16), p["w_ple_gate"][i].astype(BF16),
               p["w_ple"][i].astype(BF16),
               p["ln2_g"][i].reshape(1, D_MODEL), p["ln2_b"][i].reshape(1, D_MODEL))
    return {"mixer": mixer, "channel": channel}


def kernel(x, p, w_in, b_forget, conv_w, conv_b, dt_bias, a_log, d_skip, ssd_norm_w,
           w_out, ln1_g, ln1_b, w_router, router_bias, w_gate, w_up, w_down, w_ple,
           w_ple_gate, ln2_g, ln2_b):
    params = dict(w_in=w_in, b_forget=b_forget, conv_w=conv_w, conv_b=conv_b,
                  dt_bias=dt_bias, a_log=a_log, d_skip=d_skip, ssd_norm_w=ssd_norm_w,
                  w_out=w_out, ln1_g=ln1_g, ln1_b=ln1_b, w_router=w_router,
                  router_bias=router_bias, w_gate=w_gate, w_up=w_up, w_down=w_down,
                  w_ple=w_ple, w_ple_gate=w_ple_gate, ln2_g=ln2_g, ln2_b=ln2_b)
    batch, s, _ = x.shape
    w_router2, bias_col = _prep_router(params)
    outs = []
    for bi in range(batch):
        h = x[bi]
        h_bf = h.astype(BF16)
        for i in range(w_in.shape[0]):
            prep = _prep_layer(params, i)
            h, h_bf = _mixer_layer(h, h_bf, *prep["mixer"])
            w_g, w_u, w_d, w_pg, w_pl, g2, b2 = prep["channel"]
            h, h_bf = _channel_layer(h, h_bf, p[i, bi], w_router2, bias_col,
                                     w_g, w_u, w_d, w_pg, w_pl, g2, b2)
        outs.append(h)
    return jnp.stack(outs)
```

```python
import functools
import math

import jax
import jax.numpy as jnp
from jax import lax
from jax.experimental import pallas as pl
from jax.experimental.pallas import tpu as pltpu

F32 = jnp.float32
BF16 = jnp.bfloat16
I32 = jnp.int32

D_MODEL = 2048
ATT_HEADS = 8
ATT_HEAD_DIM = 128
D_ATT = ATT_HEADS * ATT_HEAD_DIM
SSD_HEADS = 16
SSD_HEAD_DIM = 64
D_SSD = SSD_HEADS * SSD_HEAD_DIM
SSD_GROUPS = 4
HEADS_PER_GROUP = SSD_HEADS // SSD_GROUPS
SSD_STATE = 128
D_GROUP = D_SSD // SSD_GROUPS
CONV_WIDTH = 4
D_BC = SSD_GROUPS * SSD_STATE
D_CONV = D_SSD + 2 * D_BC
N_EXPERTS = 16
N_EXPERT_GROUPS = 4
EXPERTS_PER_GROUP = N_EXPERTS // N_EXPERT_GROUPS
D_EXPERT = 1024
PLE_DIM = 256
DEPTH = 2
DEEPNORM_ALPHA = (2 * DEPTH) ** 0.25
LN_EPS = 1e-5
RMS_EPS = 1e-5
LOG2_E = math.log2(math.e)

LANES = 128
SUBLANES = 8
VMEM_LIMIT_BYTES = 56 * 1024 * 1024

D_MAIN = 3 * D_ATT + D_SSD + D_CONV
COL_WIN = 3 * D_ATT
D_WIN = D_MAIN - COL_WIN
XBC_TAIL = ATT_HEADS

SSD_CHUNK = 128
CONV_HALO = SUBLANES


def _params(*sem):
    return pltpu.CompilerParams(dimension_semantics=sem,
                                vmem_limit_bytes=VMEM_LIMIT_BYTES)


def _softplus(u):
    return jnp.maximum(u, 0.0) + jnp.log1p(jnp.exp(-jnp.abs(u)))


def _sigmoid(u):
    return 1.0 / (1.0 + jnp.exp(-u))


def _silu(u):
    return u * _sigmoid(u)


def _split3(v):
    hi = v.astype(BF16)
    r1 = v - hi.astype(F32)
    mid = r1.astype(BF16)
    lo = (r1 - mid.astype(F32)).astype(BF16)
    return hi, mid, lo


def _dot(a, b):
    return jnp.dot(a, b, preferred_element_type=F32)


def _exact_left_dot(ones_bf16, v):
    hi, mid, lo = _split3(v)
    return _dot(ones_bf16, hi) + (_dot(ones_bf16, mid) + _dot(ones_bf16, lo))


def _proj_kernel(x_ref, w_ref, o_ref, w_bf, *, q_blocks):
    @pl.when(pl.program_id(1) == 0)
    def _():
        w_bf[...] = w_ref[0].astype(BF16)

    scale = jnp.where(pl.program_id(0) < q_blocks, LOG2_E * ATT_HEAD_DIM ** -0.5, 1.0)
    o_ref[...] = (_dot(x_ref[...], w_bf[...]) * scale).astype(o_ref.dtype)


def _in_proj(h_bf, w_in, layer, tm, tn):
    s = h_bf.shape[0]
    return pl.pallas_call(
        functools.partial(_proj_kernel, q_blocks=D_ATT // tn),
        grid=(D_MAIN // tn, s // tm),
        in_specs=[pl.BlockSpec((tm, D_MODEL), lambda j, i: (i, 0)),
                  pl.BlockSpec((1, D_MODEL, tn), lambda j, i: (layer, 0, j))],
        out_specs=pl.BlockSpec((tm, tn), lambda j, i: (i, j)),
        out_shape=jax.ShapeDtypeStruct((s, D_MAIN), BF16),
        scratch_shapes=[pltpu.VMEM((D_MODEL, tn), BF16)],
        compiler_params=_params("arbitrary", "arbitrary"),
        name="in_proj",
    )(h_bf, w_in)


def _gates_kernel(x_ref, w_ref, b_ref, tri_ref, sel_ref, ones_ref, faug_ref, dt_ref,
                  tail_ref, carry_ref):
    @pl.when(pl.program_id(0) == 0)
    def _():
        carry_ref[...] = jnp.zeros_like(carry_ref)

    v = _dot(x_ref[...], w_ref[...]) + b_ref[...]
    log_f = -_softplus(-v[:, :LANES])
    dt_ref[...] = _softplus(v[:, LANES:2 * LANES])
    tail_ref[...] = v[:, 2 * LANES:]
    fcum = _exact_left_dot(tri_ref[...], log_f) + carry_ref[...]
    carry_ref[...] = fcum[-1:, :]
    hi, mid, lo = _split3(fcum * LOG2_E)
    aug = (_dot(hi, sel_ref[0]) + _dot(mid, sel_ref[1]) + _dot(lo, sel_ref[2])
           + ones_ref[...])
    faug_ref[...] = aug.astype(BF16)


def _forget_layout():
    import numpy as np
    sel = np.zeros((3, LANES, 2 * D_ATT), np.float32)
    ones = np.zeros((1, 2 * D_ATT), np.float32)
    for h in range(ATT_HEADS):
        for k in range(3):
            sel[k, h, ATT_HEAD_DIM * h + k] = 1.0
            sel[k, h, D_ATT + ATT_HEAD_DIM * h + 3 + k] = -1.0
            ones[0, ATT_HEAD_DIM * h + 3 + k] = 1.0
            ones[0, D_ATT + ATT_HEAD_DIM * h + k] = 1.0
    return jnp.asarray(sel, BF16), jnp.asarray(ones, F32)


def _gates(h_bf, w_small, b_small, tm):
    s = h_bf.shape[0]
    tri = jnp.tril(jnp.ones((tm, tm), F32)).astype(BF16)
    sel, ones = _forget_layout()
    return pl.pallas_call(
        _gates_kernel,
        grid=(s // tm,),
        in_specs=[pl.BlockSpec((tm, D_MODEL), lambda i: (i, 0)),
                  pl.BlockSpec((D_MODEL, 3 * LANES), lambda i: (0, 0)),
                  pl.BlockSpec((1, 3 * LANES), lambda i: (0, 0)),
                  pl.BlockSpec((tm, tm), lambda i: (0, 0)),
                  pl.BlockSpec((3, LANES, 2 * D_ATT), lambda i: (0, 0, 0)),
                  pl.BlockSpec((1, 2 * D_ATT), lambda i: (0, 0))],
        out_specs=[pl.BlockSpec((tm, 2 * D_ATT), lambda i: (i, 0)),
                   pl.BlockSpec((tm, LANES), lambda i: (i, 0)),
                   pl.BlockSpec((tm, LANES), lambda i: (i, 0))],
        out_shape=[jax.ShapeDtypeStruct((s, 2 * D_ATT), BF16),
                   jax.ShapeDtypeStruct((s, LANES), F32),
                   jax.ShapeDtypeStruct((s, LANES), F32)],
        scratch_shapes=[pltpu.VMEM((1, LANES), F32)],
        compiler_params=_params("arbitrary"),
        name="gates",
    )(h_bf, w_small, b_small, tri, sel, ones)


def _attn_kernel(q_ref, fq_ref, k_ref, fk_ref, v_ref, o_ref, vt_ref, sa_ref, sb_ref,
                 *, tq, tk, nh):
    assert tq == 2 * tk
    i = pl.program_id(1)
    n_chunks = vt_ref.shape[1]
    dh = ATT_HEAD_DIM
    heads = range(nh)

    @pl.when(i == 0)
    def _():
        for hd in heads:
            for c in range(n_chunks):
                blk = v_ref[c * tk:(c + 1) * tk, hd * dh:(hd + 1) * dh]
                vt_ref[hd, c] = blk.astype(F32).T.astype(BF16)

    q_t = []
    for hd in heads:
        cols = slice(hd * dh, (hd + 1) * dh)
        q_aug = jnp.concatenate([q_ref[:, cols], fq_ref[:, cols]], axis=1)
        q_t.append(q_aug.astype(F32).T.astype(BF16))
    kv_pos = lax.broadcasted_iota(I32, (tk, tq), 0)
    q_pos = lax.broadcasted_iota(I32, (tk, tq), 1)

    def logits(hd, c):
        start = pl.multiple_of(c * tk, tk)
        cols = slice(hd * dh, (hd + 1) * dh)
        k_aug = jnp.concatenate([k_ref[pl.ds(start, tk), cols],
                                 fk_ref[pl.ds(start, tk), cols]], axis=1)
        return _dot(k_aug, q_t[hd])

    def update(st, hd, c, carry, diag_offset):
        m, l, acc = carry
        if diag_offset is not None:
            st = jnp.where(kv_pos + diag_offset <= q_pos, st, -jnp.inf)
        m_new = jnp.maximum(m, jnp.max(st, axis=0, keepdims=True))
        p = jnp.exp2(st - m_new)
        alpha = jnp.exp2(m - m_new)
        l = alpha * l + jnp.sum(p, axis=0, keepdims=True)
        acc = alpha * acc + _dot(vt_ref[hd, c], p.astype(BF16))
        return m_new, l, acc

    first = 2 * i
    for hd in heads:
        sa_ref[hd] = logits(hd, 0)

    def pair(t, carries):
        out = []
        for hd in heads:
            sb_ref[hd] = logits(hd, 2 * t + 1)
        for hd in heads:
            out.append(update(sa_ref[hd], hd, 2 * t, carries[hd], None))
        for hd in heads:
            sa_ref[hd] = logits(hd, 2 * t + 2)
        return tuple(update(sb_ref[hd], hd, 2 * t + 1, out[hd], None) for hd in heads)

    init = (jnp.full((1, tq), -jnp.inf, F32), jnp.zeros((1, tq), F32),
            jnp.zeros((dh, tq), F32))
    carries = lax.fori_loop(0, i, pair, tuple(init for _ in heads))
    for hd in heads:
        sb_ref[hd] = logits(hd, first + 1)
    for hd in heads:
        carry = update(sa_ref[hd], hd, first, carries[hd], 0)
        _, l, acc = update(sb_ref[hd], hd, first + 1, carry, tk)
        o_ref[:, hd * dh:(hd + 1) * dh] = (acc / l).T.astype(o_ref.dtype)


def _attention(y, faug, tq, tk, nh):
    s = y.shape[0]
    width = nh * ATT_HEAD_DIM
    groups = ATT_HEADS // nh
    kern = functools.partial(_attn_kernel, tq=tq, tk=tk, nh=nh)
    q_block = lambda g0: pl.BlockSpec((tq, width), lambda g, i: (i, g0 + g))
    kv_block = lambda g0: pl.BlockSpec((s, width), lambda g, i: (0, g0 + g))
    return pl.pallas_call(
        kern,
        grid=(groups, s // tq),
        in_specs=[q_block(0), q_block(0), kv_block(groups), kv_block(groups),
                  kv_block(2 * groups)],
        out_specs=pl.BlockSpec((tq, width), lambda g, i: (i, g)),
        out_shape=jax.ShapeDtypeStruct((s, D_ATT), BF16),
        scratch_shapes=[pltpu.VMEM((nh, s // tk, ATT_HEAD_DIM, tk), BF16),
                        pltpu.VMEM((nh, tk, tq), F32), pltpu.VMEM((nh, tk, tq), F32)],
        compiler_params=_params("arbitrary", "arbitrary"),
        name="fox_attention",
    )(y, faug, y, faug, y)


def _expand_heads(w, n_heads):
    length = w.shape[0]
    lane = lax.broadcasted_iota(I32, (length, LANES), 1)
    blocks = []
    for m in range(n_heads // 2):
        blocks.append(jnp.where(lane < SSD_HEAD_DIM, w[:, 2 * m:2 * m + 1],
                                w[:, 2 * m + 1:2 * m + 2]))
    return jnp.concatenate(blocks, axis=1)


def _ssd_kernel(win_ref, tail_ref, dt_ref, convw_ref, convb_ref, a_ref, dskip_ref,
                normw_ref, tri_ref, o_ref, ubuf_ref, state_ref):
    length = SSD_CHUNK

    @pl.when(pl.program_id(0) == 0)
    def _():
        ubuf_ref[0:CONV_HALO, :] = jnp.zeros((CONV_HALO, D_CONV), F32)
        state_ref[...] = jnp.zeros_like(state_ref)

    win = jnp.concatenate([win_ref[...].astype(F32), tail_ref[...]], axis=1)
    win = jnp.concatenate([win[:, ATT_HEADS:], win[:, :ATT_HEADS]], axis=1)
    z = win[:, :D_SSD]

    ubuf_ref[CONV_HALO:CONV_HALO + length, :] = win[:, D_SSD:D_SSD + D_CONV]
    conv = convb_ref[...]
    for k in range(CONV_WIDTH):
        off = CONV_HALO - (CONV_WIDTH - 1) + k
        conv = conv + convw_ref[k:k + 1, :] * ubuf_ref[off:off + length, :]
    tail = ubuf_ref[length:length + CONV_HALO, :]
    ubuf_ref[0:CONV_HALO, :] = tail
    xc = _silu(conv)
    xs = xc[:, :D_SSD]

    dt = dt_ref[...]
    da = dt * a_ref[...]
    acum = _exact_left_dot(tri_ref[...], da)
    acum_t = acum.T
    a_last = acum[length - 1:length, :]
    w_off = _expand_heads(jnp.exp(acum), SSD_HEADS)
    w_state = _expand_heads(jnp.exp(a_last - acum) * dt, SSD_HEADS)
    w_dt = _expand_heads(dt, SSD_HEADS)
    chunk_decay = _expand_heads(jnp.exp(a_last), SSD_HEADS)

    x_dt = (xs * w_dt).astype(BF16)
    x_state = (xs * w_state).astype(BF16)
    row = lax.broadcasted_iota(I32, (length, length), 0)
    col = lax.broadcasted_iota(I32, (length, length), 1)
    causal = col <= row
    grp_lane = lax.broadcasted_iota(I32, (length, D_GROUP), 1) // SSD_HEAD_DIM

    y_parts = []
    for g in range(SSD_GROUPS):
        b_g = xc[:, D_SSD + g * SSD_STATE:D_SSD + (g + 1) * SSD_STATE].astype(BF16)
        c_g = xc[:, D_SSD + D_BC + g * SSD_STATE:
                 D_SSD + D_BC + (g + 1) * SSD_STATE].astype(BF16)
        cb = lax.dot_general(c_g, b_g, (((1,), (1,)), ((), ())),
                             preferred_element_type=F32)
        cols = slice(g * D_GROUP, (g + 1) * D_GROUP)
        x_dt_g = x_dt[:, cols]
        m_blocks, x_blocks = [], []
        for r in range(HEADS_PER_GROUP):
            hd = g * HEADS_PER_GROUP + r
            seg = acum[:, hd:hd + 1] - acum_t[hd:hd + 1, :]
            decay = jnp.exp(jnp.where(causal, seg, -jnp.inf))
            m_blocks.append((cb * decay).astype(BF16))
            x_blocks.append(jnp.where(grp_lane == r, x_dt_g, jnp.zeros_like(x_dt_g)))
        y_diag = _dot(jnp.concatenate(m_blocks, axis=1),
                      jnp.concatenate(x_blocks, axis=0))
        st = state_ref[g]
        y_off = _dot(c_g, st.astype(BF16)) * w_off[:, cols]
        y_parts.append(y_diag + y_off)
        new_st = lax.dot_general(b_g, x_state[:, cols], (((0,), (0,)), ((), ())),
                                 preferred_element_type=F32)
        state_ref[g] = st * chunk_decay[:, cols] + new_st

    y = jnp.concatenate(y_parts, axis=1) + dskip_ref[...] * xs
    yg = y * _silu(z)
    outs = []
    for g in range(SSD_GROUPS):
        blk = yg[:, g * D_GROUP:(g + 1) * D_GROUP]
        ms = jnp.mean(blk * blk, axis=1, keepdims=True)
        outs.append(blk * lax.rsqrt(ms + RMS_EPS))
    o_ref[...] = (jnp.concatenate(outs, axis=1) * normw_ref[...]).astype(o_ref.dtype)


def _ssd(y, tail, dtv, conv_w, conv_b, a_row, dskip_row, normw_row):
    s = y.shape[0]
    length = SSD_CHUNK
    tri = jnp.tril(jnp.ones((length, length), F32)).astype(BF16)
    full = lambda shape: pl.BlockSpec(shape, lambda c: (0,) * len(shape))
    return pl.pallas_call(
        _ssd_kernel,
        grid=(s // length,),
        in_specs=[pl.BlockSpec((length, D_WIN), lambda c: (c, COL_WIN // D_WIN)),
                  pl.BlockSpec((length, LANES), lambda c: (c, 0)),
                  pl.BlockSpec((length, LANES), lambda c: (c, 0)),
                  full((CONV_WIDTH, D_CONV)), full((1, D_CONV)), full((1, LANES)),
                  full((1, D_SSD)), full((1, D_SSD)), full((length, length))],
        out_specs=pl.BlockSpec((length, D_SSD), lambda c: (c, 0)),
        out_shape=jax.ShapeDtypeStruct((s, D_SSD), BF16),
        scratch_shapes=[pltpu.VMEM((length + CONV_HALO, D_CONV), F32),
                        pltpu.VMEM((SSD_GROUPS, SSD_STATE, D_GROUP), F32)],
        compiler_params=_params("arbitrary"),
        name="ssd",
    )(y, tail, dtv, conv_w, conv_b, a_row, dskip_row, normw_row, tri)


def _layer_norm(xf, g, b):
    mu = jnp.mean(xf, axis=1, keepdims=True)
    xc = xf - mu
    var = jnp.mean(xc * xc, axis=1, keepdims=True)
    return xc * lax.rsqrt(var + LN_EPS) * g + b


def _resident_weight(shape, layer):
    return pl.BlockSpec((1,) + shape, lambda i, *_: (layer,) + (0,) * len(shape),
                        pipeline_mode=pl.Buffered(1))


def _out_proj_kernel(att_ref, ssd_ref, w_ref, h_ref, g_ref, b_ref, o_ref, obf_ref, w_bf):
    @pl.when(pl.program_id(0) == 0)
    def _():
        w_bf[...] = w_ref[0].astype(BF16)

    mix = _dot(att_ref[...], w_bf[:D_ATT, :]) + _dot(ssd_ref[...], w_bf[D_ATT:, :])
    out = _layer_norm(DEEPNORM_ALPHA * h_ref[...] + mix, g_ref[...], b_ref[...])
    o_ref[...] = out
    obf_ref[...] = out.astype(BF16)


def _out_proj_ln(att, ssd, w_out, layer, h, g, b, tm):
    s = h.shape[0]
    full = lambda shape: pl.BlockSpec(shape, lambda i: (0,) * len(shape))
    return pl.pallas_call(
        _out_proj_kernel,
        grid=(s // tm,),
        in_specs=[pl.BlockSpec((tm, D_ATT), lambda i: (i, 0)),
                  pl.BlockSpec((tm, D_SSD), lambda i: (i, 0)),
                  _resident_weight((D_MODEL, D_MODEL), layer),
                  pl.BlockSpec((tm, D_MODEL), lambda i: (i, 0)),
                  full((1, D_MODEL)), full((1, D_MODEL))],
        out_specs=[pl.BlockSpec((tm, D_MODEL), lambda i: (i, 0)),
                   pl.BlockSpec((tm, D_MODEL), lambda i: (i, 0))],
        out_shape=[jax.ShapeDtypeStruct((s, D_MODEL), F32),
                   jax.ShapeDtypeStruct((s, D_MODEL), BF16)],
        scratch_shapes=[pltpu.VMEM((D_MODEL, D_MODEL), BF16)],
        compiler_params=_params("arbitrary"),
        name="out_proj_ln",
    )(att, ssd, w_out, h, g, b)


def _router_kernel(h_ref, wr_ref, bias_ref, triu_ref, idx_ref, wcol_ref, cnt_ref,
                   carry_ref, *, tm):
    @pl.when(pl.program_id(0) == 0)
    def _():
        carry_ref[...] = jnp.zeros_like(carry_ref)

    h = h_ref[...]
    h_hi = h.astype(BF16)
    h_lo = (h - h_hi.astype(F32)).astype(BF16)
    logits = _dot(h_hi, wr_ref[0]) + (_dot(h_hi, wr_ref[1]) + _dot(h_lo, wr_ref[0]))
    lt = logits.T
    scores = _sigmoid(lt[:N_EXPERTS, :])
    biased = scores + bias_ref[...]
    npg = EXPERTS_PER_GROUP
    s_r = [scores[npg * r:npg * (r + 1), :] for r in range(npg)]
    b_r = [biased[npg * r:npg * (r + 1), :] for r in range(npg)]

    gs = None
    for a in range(npg):
        for b in range(a + 1, npg):
            pair = b_r[a] + b_r[b]
            gs = pair if gs is None else jnp.maximum(gs, pair)
    best = jnp.zeros((1, tm), I32)
    top = gs[0:1, :]
    for g in range(1, N_EXPERT_GROUPS):
        better = gs[g:g + 1, :] > top
        best = jnp.where(better, g, best)
        top = jnp.where(better, gs[g:g + 1, :], top)

    def pick_group(a):
        out = a[0:1, :]
        for g in range(1, N_EXPERT_GROUPS):
            out = jnp.where(best == g, a[g:g + 1, :], out)
        return out

    sb = [pick_group(b) for b in b_r]
    ss = [pick_group(s) for s in s_r]
    first = jnp.zeros((1, tm), I32)
    fmax = sb[0]
    for r in range(1, npg):
        better = sb[r] > fmax
        first = jnp.where(better, r, first)
        fmax = jnp.where(better, sb[r], fmax)
    second = jnp.zeros((1, tm), I32)
    smax = jnp.full((1, tm), -jnp.inf, F32)
    for r in range(npg):
        cand = jnp.where(first == r, -jnp.inf, sb[r])
        better = cand > smax
        second = jnp.where(better, r, second)
        smax = jnp.where(better, cand, smax)

    def pick_expert(which):
        out = ss[0]
        for r in range(1, npg):
            out = jnp.where(which == r, ss[r], out)
        return out

    a0 = pick_expert(first)
    a1 = pick_expert(second)
    denom = a0 + a1
    e0 = best * npg + first
    e1 = best * npg + second

    erow = lax.broadcasted_iota(I32, (N_EXPERTS, tm), 0)
    hit0 = erow == e0
    hit1 = erow == e1
    onehot = jnp.where(hit0, 1.0, 0.0) + jnp.where(hit1, 1.0, 0.0)
    cum = _dot(onehot.astype(BF16), triu_ref[...])
    carry = carry_ref[...][:, 0:1]
    before = cum - onehot + carry
    rank0 = jnp.sum(jnp.where(hit0, before, 0.0), axis=0, keepdims=True)
    rank1 = jnp.sum(jnp.where(hit1, before, 0.0), axis=0, keepdims=True)
    new_carry = carry_ref[...] + cum[:, tm - 1:tm]
    carry_ref[...] = new_carry
    cnt_ref[...] = new_carry

    zeros_i = jnp.zeros((SUBLANES - 4, tm), I32)
    idx_ref[...] = jnp.concatenate(
        [e0, e1, rank0.astype(I32), rank1.astype(I32), zeros_i], axis=0)
    wrows = jnp.concatenate([a0 / denom, a1 / denom,
                             jnp.zeros((LANES - 2, tm), F32)], axis=0)
    wcol_ref[...] = wrows.T


def _router(h, w_router2, bias_col, tm):
    s = h.shape[0]
    triu = jnp.triu(jnp.ones((tm, tm), F32)).astype(BF16)
    kern = functools.partial(_router_kernel, tm=tm)
    return pl.pallas_call(
        kern,
        grid=(s // tm,),
        in_specs=[pl.BlockSpec((tm, D_MODEL), lambda i: (i, 0)),
                  pl.BlockSpec((2, D_MODEL, LANES), lambda i: (0, 0, 0)),
                  pl.BlockSpec((N_EXPERTS, 1), lambda i: (0, 0)),
                  pl.BlockSpec((tm, tm), lambda i: (0, 0))],
        out_specs=[pl.BlockSpec((SUBLANES, tm), lambda i: (0, i)),
                   pl.BlockSpec((tm, LANES), lambda i: (i, 0)),
                   pl.BlockSpec((N_EXPERTS, LANES), lambda i: (0, 0))],
        out_shape=[jax.ShapeDtypeStruct((SUBLANES, s), I32),
                   jax.ShapeDtypeStruct((s, LANES), F32),
                   jax.ShapeDtypeStruct((N_EXPERTS, LANES), F32)],
        scratch_shapes=[pltpu.VMEM((N_EXPERTS, LANES), F32)],
        compiler_params=_params("arbitrary"),
        name="router",
    )(h, w_router2, bias_col, triu)


def _work_items(counts, tm, n_tiles):
    ends = jnp.cumsum(counts)
    starts = ends - counts
    first_tile = starts // tm
    last_tile = (ends - 1) // tm
    n_e = jnp.where(counts > 0, last_tile - first_tile + 1, 0)
    item_end = jnp.cumsum(n_e)
    item_start = item_end - n_e
    n_items = n_tiles + N_EXPERTS - 1
    w = jnp.arange(n_items, dtype=I32)
    total = item_end[-1]
    valid = w < total
    wc = jnp.minimum(w, total - 1)
    e_w = jnp.minimum(jnp.sum((item_end[None, :] <= wc[:, None]).astype(I32), axis=1),
                      N_EXPERTS - 1)
    tile_w = first_tile[e_w] + (wc - item_start[e_w])
    lo = jnp.clip(starts[e_w] - tile_w * tm, 0, tm)
    hi = jnp.clip(ends[e_w] - tile_w * tm, 0, tm)
    lo = jnp.where(valid, lo, 0)
    hi = jnp.where(valid, hi, 0)
    return tile_w.astype(I32), e_w, lo.astype(I32), hi.astype(I32)


def _dispatch_kernel(pos0_ref, pos1_ref, h_ref, xs_ref, sems, *, tm):
    base = pl.program_id(0) * tm

    def issue(r, carry):
        src = h_ref.at[pl.ds(r, 1), :]
        pltpu.make_async_copy(src, xs_ref.at[pl.ds(pos0_ref[base + r], 1), :],
                              sems.at[0]).start(priority=0)
        pltpu.make_async_copy(src, xs_ref.at[pl.ds(pos1_ref[base + r], 1), :],
                              sems.at[1]).start(priority=1)
        return carry

    lax.fori_loop(0, tm, issue, 0)
    for k in range(2):
        pltpu.make_async_copy(h_ref, xs_ref.at[pl.ds(0, tm), :], sems.at[k]).wait()


def _dispatch(pos0, pos1, h, tm):
    s = h.shape[0]
    kern = functools.partial(_dispatch_kernel, tm=tm)
    return pl.pallas_call(
        kern,
        grid_spec=pltpu.PrefetchScalarGridSpec(
            num_scalar_prefetch=2,
            grid=(s // tm,),
            in_specs=[pl.BlockSpec((tm, D_MODEL), lambda i, p0, p1: (i, 0))],
            out_specs=pl.BlockSpec(memory_space=pl.ANY),
            scratch_shapes=[pltpu.SemaphoreType.DMA((2,))]),
        out_shape=jax.ShapeDtypeStruct((2 * s, D_MODEL), F32),
        compiler_params=_params("arbitrary"),
        name="dispatch",
    )(pos0, pos1, h)


def _item_flags(tile_ref, exp_ref):
    w = pl.program_id(0)
    prev = jnp.maximum(w - 1, 0)
    fresh_tile = jnp.logical_or(w == 0, tile_ref[w] != tile_ref[prev])
    new_expert = jnp.logical_or(w == 0, exp_ref[w] != exp_ref[prev])
    return fresh_tile, new_expert


def _store_rows(o_ref, val, lo, hi, fresh_tile):
    row = lax.broadcasted_iota(I32, (val.shape[0], 1), 0)
    keep = jnp.logical_and(row >= lo, row < hi)

    @pl.when(fresh_tile)
    def _():
        o_ref[...] = jnp.where(keep, val, jnp.zeros_like(val))

    @pl.when(jnp.logical_not(fresh_tile))
    def _():
        o_ref[...] = jnp.where(keep, val, o_ref[...])


def _moe_up_kernel(tile_ref, exp_ref, lo_ref, hi_ref, x_ref, wg_ref, wu_ref, o_ref,
                   wg_bf, wu_bf):
    w = pl.program_id(0)
    fresh_tile, new_expert = _item_flags(tile_ref, exp_ref)

    @pl.when(new_expert)
    def _():
        wg_bf[...] = wg_ref[0, 0].astype(BF16)
        wu_bf[...] = wu_ref[0, 0].astype(BF16)

    @pl.when(hi_ref[w] > lo_ref[w])
    def _():
        x = x_ref[...].astype(BF16)
        hid = _silu(_dot(x, wg_bf[...])) * _dot(x, wu_bf[...])
        _store_rows(o_ref, hid.astype(BF16), lo_ref[w], hi_ref[w], fresh_tile)


def _moe_down_kernel(tile_ref, exp_ref, lo_ref, hi_ref, h_ref, wd_ref, o_ref, wd_bf):
    w = pl.program_id(0)
    fresh_tile, new_expert = _item_flags(tile_ref, exp_ref)

    @pl.when(new_expert)
    def _():
        wd_bf[...] = wd_ref[0, 0].astype(BF16)

    @pl.when(hi_ref[w] > lo_ref[w])
    def _():
        y = _dot(h_ref[...], wd_bf[...])
        _store_rows(o_ref, y, lo_ref[w], hi_ref[w], fresh_tile)


def _moe(items, xs, w_gate, w_up, w_down, layer, tm):
    rows = xs.shape[0]
    n_items = items[0].shape[0]
    row_tile = lambda width: pl.BlockSpec(
        (tm, width), lambda w, tile, exp, lo, hi: (tile[w], 0))
    expert = lambda shape: pl.BlockSpec(
        (1, 1) + shape, lambda w, tile, exp, lo, hi: (layer, exp[w], 0, 0))
    hid = pl.pallas_call(
        _moe_up_kernel,
        grid_spec=pltpu.PrefetchScalarGridSpec(
            num_scalar_prefetch=4,
            grid=(n_items,),
            in_specs=[row_tile(D_MODEL), expert((D_MODEL, D_EXPERT)),
                      expert((D_MODEL, D_EXPERT))],
            out_specs=row_tile(D_EXPERT),
            scratch_shapes=[pltpu.VMEM((D_MODEL, D_EXPERT), BF16),
                            pltpu.VMEM((D_MODEL, D_EXPERT), BF16)]),
        out_shape=jax.ShapeDtypeStruct((rows, D_EXPERT), BF16),
        compiler_params=_params("arbitrary"),
        name="moe_up",
    )(*items, xs, w_gate, w_up)
    return pl.pallas_call(
        _moe_down_kernel,
        grid_spec=pltpu.PrefetchScalarGridSpec(
            num_scalar_prefetch=4,
            grid=(n_items,),
            in_specs=[row_tile(D_EXPERT), expert((D_EXPERT, D_MODEL))],
            out_specs=row_tile(D_MODEL),
            scratch_shapes=[pltpu.VMEM((D_EXPERT, D_MODEL), BF16)]),
        out_shape=jax.ShapeDtypeStruct((rows, D_MODEL), F32),
        compiler_params=_params("arbitrary"),
        name="moe_down",
    )(*items, hid, w_down)


def _ple_kernel(pos0_ref, pos1_ref, hbf_ref, h_ref, p_ref, wgate_ref, wple_ref,
                wcol_ref, g_ref, b_ref, ys_ref, o_ref, obf_ref, buf_ref, sems,
                wgate_bf, wple_bf, *, tm):
    base = pl.program_id(0) * tm

    @pl.when(pl.program_id(0) == 0)
    def _():
        wgate_bf[...] = wgate_ref[0].astype(BF16)
        wple_bf[...] = wple_ref[0].astype(BF16)

    def issue(r, carry):
        for k, pos_ref in enumerate((pos0_ref, pos1_ref)):
            pltpu.make_async_copy(ys_ref.at[pl.ds(pos_ref[base + r], 1), :],
                                  buf_ref.at[k, pl.ds(r, 1), :],
                                  sems.at[k]).start(priority=k)
        return carry

    lax.fori_loop(0, tm, issue, 0)
    gate = _sigmoid(_dot(hbf_ref[...], wgate_bf[...]))
    ple = gate * _dot(p_ref[...].astype(BF16), wple_bf[...])
    for k in range(2):
        pltpu.make_async_copy(ys_ref.at[pl.ds(0, tm), :], buf_ref.at[k], sems.at[k]).wait()
    wc = wcol_ref[...]
    moe = wc[:, 0:1] * buf_ref[0] + wc[:, 1:2] * buf_ref[1]
    out = _layer_norm(DEEPNORM_ALPHA * h_ref[...] + moe + ple, g_ref[...], b_ref[...])
    o_ref[...] = out
    obf_ref[...] = out.astype(BF16)


def _ple_combine_ln(pos0, pos1, h_bf, h, p_i, w_ple_gate, w_ple, layer, wcol, g, b, ys, tm):
    s = h.shape[0]
    kern = functools.partial(_ple_kernel, tm=tm)
    tile = lambda width: pl.BlockSpec((tm, width), lambda i, p0, p1: (i, 0))
    full = lambda shape: pl.BlockSpec(shape, lambda i, p0, p1: (0,) * len(shape))
    return pl.pallas_call(
        kern,
        grid_spec=pltpu.PrefetchScalarGridSpec(
            num_scalar_prefetch=2,
            grid=(s // tm,),
            in_specs=[tile(D_MODEL), tile(D_MODEL), tile(PLE_DIM),
                      _resident_weight((D_MODEL, D_MODEL), layer),
                      _resident_weight((PLE_DIM, D_MODEL), layer),
                      tile(LANES), full((1, D_MODEL)), full((1, D_MODEL)),
                      pl.BlockSpec(memory_space=pl.ANY)],
            out_specs=[tile(D_MODEL), tile(D_MODEL)],
            scratch_shapes=[pltpu.VMEM((2, tm, D_MODEL), F32),
                            pltpu.SemaphoreType.DMA((2,)),
                            pltpu.VMEM((D_MODEL, D_MODEL), BF16),
                            pltpu.VMEM((PLE_DIM, D_MODEL), BF16)]),
        out_shape=[jax.ShapeDtypeStruct((s, D_MODEL), F32),
                   jax.ShapeDtypeStruct((s, D_MODEL), BF16)],
        compiler_params=_params("arbitrary"),
        name="ple_combine_ln",
    )(pos0, pos1, h_bf, h, p_i, w_ple_gate, w_ple, wcol, g, b, ys)


def _row(v, width):
    v = v.astype(F32)
    return jnp.concatenate([v, jnp.zeros((width - v.shape[0],), F32)]).reshape(1, width)


def _mixer_layer(h, h_bf, prm, layer):
    s = h.shape[0]
    w_in = prm["w_in"]
    c_f = 3 * D_ATT
    c_dt = D_MAIN + XBC_TAIL
    pad = lambda cols: jnp.concatenate(
        [cols, jnp.zeros((D_MODEL, LANES - cols.shape[1]), cols.dtype)], axis=1)
    w_small = jnp.concatenate(
        [pad(w_in[layer, :, c_f:c_f + ATT_HEADS]), pad(w_in[layer, :, c_dt:]),
         pad(w_in[layer, :, D_MAIN:c_dt])], axis=1).astype(BF16)
    b_small = jnp.concatenate([_row(prm["b_forget"][layer], LANES),
                               _row(prm["dt_bias"][layer], LANES),
                               jnp.zeros((1, LANES), F32)], axis=1)
    a_row = _row(-jnp.exp(prm["a_log"][layer].astype(F32)), LANES)
    dskip_row = jnp.repeat(prm["d_skip"][layer].astype(F32), SSD_HEAD_DIM).reshape(1, D_SSD)

    y = _in_proj(h_bf, w_in, layer, min(s, 1024), 512)
    faug, dtv, tail = _gates(h_bf, w_small, b_small, min(s, 256))
    att = _attention(y, faug, min(s, 512), min(s, 256), 2)
    ssd = _ssd(y, tail, dtv, prm["conv_w"][layer], _row(prm["conv_b"][layer], D_CONV),
               a_row, dskip_row, _row(prm["ssd_norm_w"][layer], D_SSD))
    return _out_proj_ln(att, ssd, prm["w_out"], layer, h,
                        _row(prm["ln1_g"][layer], D_MODEL),
                        _row(prm["ln1_b"][layer], D_MODEL), min(s, 256))


def _channel_layer(h, h_bf, p_i, w_router2, bias_col, prm, layer):
    s = h.shape[0]
    tm = min(s, 256)
    idx, wcol, cnt = _router(h, w_router2, bias_col, tm)
    counts = cnt[:, 0].astype(I32)
    offsets = jnp.cumsum(counts) - counts
    pos0 = offsets[idx[0]] + idx[2]
    pos1 = offsets[idx[1]] + idx[3]
    xs = _dispatch(pos0, pos1, h, tm)
    items = _work_items(counts, tm, 2 * s // tm)
    ys = _moe(items, xs, prm["w_gate"], prm["w_up"], prm["w_down"], layer, tm)
    return _ple_combine_ln(pos0, pos1, h_bf, h, p_i, prm["w_ple_gate"], prm["w_ple"],
                           layer, wcol, _row(prm["ln2_g"][layer], D_MODEL),
                           _row(prm["ln2_b"][layer], D_MODEL), ys, tm)


def _prep_router(p):
    order = jnp.arange(N_EXPERTS).reshape(N_EXPERT_GROUPS, EXPERTS_PER_GROUP).T.reshape(-1)
    w = p["w_router"].astype(F32)[:, order]
    w = jnp.concatenate([w, jnp.zeros((D_MODEL, LANES - N_EXPERTS), F32)], axis=1)
    w_hi = w.astype(BF16)
    w_lo = (w - w_hi.astype(F32)).astype(BF16)
    bias_col = p["router_bias"].astype(F32)[order].reshape(N_EXPERTS, 1)
    return jnp.stack([w_hi, w_lo]), bias_col


def kernel(x, p, w_in, b_forget, conv_w, conv_b, dt_bias, a_log, d_skip, ssd_norm_w,
           w_out, ln1_g, ln1_b, w_router, router_bias, w_gate, w_up, w_down, w_ple,
           w_ple_gate, ln2_g, ln2_b):
    prm = dict(w_in=w_in, b_forget=b_forget, conv_w=conv_w, conv_b=conv_b,
               dt_bias=dt_bias, a_log=a_log, d_skip=d_skip, ssd_norm_w=ssd_norm_w,
               w_out=w_out, ln1_g=ln1_g, ln1_b=ln1_b, w_router=w_router,
               router_bias=router_bias, w_gate=w_gate, w_up=w_up, w_down=w_down,
               w_ple=w_ple, w_ple_gate=w_ple_gate, ln2_g=ln2_g, ln2_b=ln2_b)
    batch = x.shape[0]
    w_router2, bias_col = _prep_router(prm)
    outs = []
    for bi in range(batch):
        h = x[bi]
        h_bf = h.astype(BF16)
        for layer in range(w_in.shape[0]):
            h, h_bf = _mixer_layer(h, h_bf, prm, layer)
            h, h_bf = _channel_layer(h, h_bf, p[layer, bi], w_router2, bias_col, prm, layer)
        outs.append(h)
    return jnp.stack(outs)
```

```python
import functools
import math

import jax
import jax.numpy as jnp
from jax import lax
from jax.experimental import pallas as pl
from jax.experimental.pallas import tpu as pltpu

F32 = jnp.float32
BF16 = jnp.bfloat16
I32 = jnp.int32

D_MODEL = 2048
ATT_HEADS = 8
ATT_HEAD_DIM = 128
D_ATT = ATT_HEADS * ATT_HEAD_DIM
SSD_HEADS = 16
SSD_HEAD_DIM = 64
D_SSD = SSD_HEADS * SSD_HEAD_DIM
SSD_GROUPS = 4
HEADS_PER_GROUP = SSD_HEADS // SSD_GROUPS
SSD_STATE = 128
D_GROUP = D_SSD // SSD_GROUPS
CONV_WIDTH = 4
D_BC = SSD_GROUPS * SSD_STATE
D_CONV = D_SSD + 2 * D_BC
N_EXPERTS = 16
N_EXPERT_GROUPS = 4
EXPERTS_PER_GROUP = N_EXPERTS // N_EXPERT_GROUPS
D_EXPERT = 1024
PLE_DIM = 256
DEPTH = 2
DEEPNORM_ALPHA = (2 * DEPTH) ** 0.25
LN_EPS = 1e-5
RMS_EPS = 1e-5
LOG2_E = math.log2(math.e)

LANES = 128
SUBLANES = 8
VMEM_LIMIT_BYTES = 56 * 1024 * 1024

D_MAIN = 3 * D_ATT + D_SSD + D_CONV
COL_WIN = 3 * D_ATT
D_WIN = D_MAIN - COL_WIN
XBC_TAIL = ATT_HEADS

SSD_CHUNK = 128
CONV_HALO = SUBLANES


def _params(*sem):
    return pltpu.CompilerParams(dimension_semantics=sem,
                                vmem_limit_bytes=VMEM_LIMIT_BYTES)


def _softplus(u):
    return jnp.maximum(u, 0.0) + jnp.log1p(jnp.exp(-jnp.abs(u)))


def _sigmoid(u):
    return 1.0 / (1.0 + jnp.exp(-u))


def _silu(u):
    return u * _sigmoid(u)


def _split3(v):
    hi = v.astype(BF16)
    r1 = v - hi.astype(F32)
    mid = r1.astype(BF16)
    lo = (r1 - mid.astype(F32)).astype(BF16)
    return hi, mid, lo


def _dot(a, b):
    return jnp.dot(a, b, preferred_element_type=F32)


def _exact_left_dot(ones_bf16, v):
    hi, mid, lo = _split3(v)
    return _dot(ones_bf16, hi) + (_dot(ones_bf16, mid) + _dot(ones_bf16, lo))


def _proj_kernel(x_ref, w_ref, o_ref, w_bf, *, q_blocks):
    @pl.when(pl.program_id(1) == 0)
    def _():
        w_bf[...] = w_ref[0].astype(BF16)

    scale = jnp.where(pl.program_id(0) < q_blocks, LOG2_E * ATT_HEAD_DIM ** -0.5, 1.0)
    o_ref[...] = (_dot(x_ref[...], w_bf[...]) * scale).astype(o_ref.dtype)


def _in_proj(h_bf, w_in, layer, tm, tn):
    s = h_bf.shape[0]
    return pl.pallas_call(
        functools.partial(_proj_kernel, q_blocks=D_ATT // tn),
        grid=(D_MAIN // tn, s // tm),
        in_specs=[pl.BlockSpec((tm, D_MODEL), lambda j, i: (i, 0)),
                  pl.BlockSpec((1, D_MODEL, tn), lambda j, i: (layer, 0, j))],
        out_specs=pl.BlockSpec((tm, tn), lambda j, i: (i, j)),
        out_shape=jax.ShapeDtypeStruct((s, D_MAIN), BF16),
        scratch_shapes=[pltpu.VMEM((D_MODEL, tn), BF16)],
        compiler_params=_params("arbitrary", "arbitrary"),
        name="in_proj",
    )(h_bf, w_in)


def _gates_kernel(x_ref, w_ref, b_ref, tri_ref, sel_ref, ones_ref, faug_ref, dt_ref,
                  tail_ref, carry_ref, w_bf):
    @pl.when(pl.program_id(0) == 0)
    def _():
        carry_ref[...] = jnp.zeros_like(carry_ref)
        w_bf[...] = w_ref[...].astype(BF16)

    v = _dot(x_ref[...], w_bf[...]) + b_ref[...]
    log_f = -_softplus(-v[:, :LANES])
    dt_ref[...] = _softplus(v[:, LANES:2 * LANES])
    tail_ref[...] = v[:, 2 * LANES:]
    fcum = _exact_left_dot(tri_ref[...], log_f) + carry_ref[...]
    carry_ref[...] = fcum[-1:, :]
    hi, mid, lo = _split3(fcum * LOG2_E)
    aug = (_dot(hi, sel_ref[0]) + _dot(mid, sel_ref[1]) + _dot(lo, sel_ref[2])
           + ones_ref[...])
    faug_ref[...] = aug.astype(BF16)


def _forget_layout():
    import numpy as np
    sel = np.zeros((3, LANES, 2 * D_ATT), np.float32)
    ones = np.zeros((1, 2 * D_ATT), np.float32)
    for h in range(ATT_HEADS):
        for k in range(3):
            sel[k, h, ATT_HEAD_DIM * h + k] = 1.0
            sel[k, h, D_ATT + ATT_HEAD_DIM * h + 3 + k] = -1.0
            ones[0, ATT_HEAD_DIM * h + 3 + k] = 1.0
            ones[0, D_ATT + ATT_HEAD_DIM * h + k] = 1.0
    return jnp.asarray(sel, BF16), jnp.asarray(ones, F32)


def _gates(h_bf, w_small, b_small, tm):
    s = h_bf.shape[0]
    tri = jnp.tril(jnp.ones((tm, tm), F32)).astype(BF16)
    sel, ones = _forget_layout()
    return pl.pallas_call(
        _gates_kernel,
        grid=(s // tm,),
        in_specs=[pl.BlockSpec((tm, D_MODEL), lambda i: (i, 0)),
                  pl.BlockSpec((D_MODEL, 3 * LANES), lambda i: (0, 0)),
                  pl.BlockSpec((1, 3 * LANES), lambda i: (0, 0)),
                  pl.BlockSpec((tm, tm), lambda i: (0, 0)),
                  pl.BlockSpec((3, LANES, 2 * D_ATT), lambda i: (0, 0, 0)),
                  pl.BlockSpec((1, 2 * D_ATT), lambda i: (0, 0))],
        out_specs=[pl.BlockSpec((tm, 2 * D_ATT), lambda i: (i, 0)),
                   pl.BlockSpec((tm, LANES), lambda i: (i, 0)),
                   pl.BlockSpec((tm, LANES), lambda i: (i, 0))],
        out_shape=[jax.ShapeDtypeStruct((s, 2 * D_ATT), BF16),
                   jax.ShapeDtypeStruct((s, LANES), F32),
                   jax.ShapeDtypeStruct((s, LANES), F32)],
        scratch_shapes=[pltpu.VMEM((1, LANES), F32),
                        pltpu.VMEM((D_MODEL, 3 * LANES), BF16)],
        compiler_params=_params("arbitrary"),
        name="gates",
    )(h_bf, w_small, b_small, tri, sel, ones)


def _attn_kernel(q_ref, fq_ref, k_ref, fk_ref, v_ref, o_ref, vt_ref, sa_ref, sb_ref,
                 *, tq, tk, nh):
    assert tq == 2 * tk
    i = pl.program_id(1)
    n_chunks = vt_ref.shape[1]
    dh = ATT_HEAD_DIM
    heads = range(nh)

    @pl.when(i == 0)
    def _():
        for hd in heads:
            for c in range(n_chunks):
                blk = v_ref[c * tk:(c + 1) * tk, hd * dh:(hd + 1) * dh]
                vt_ref[hd, c] = blk.astype(F32).T.astype(BF16)

    q_t = []
    for hd in heads:
        cols = slice(hd * dh, (hd + 1) * dh)
        q_aug = jnp.concatenate([q_ref[:, cols], fq_ref[:, cols]], axis=1)
        q_t.append(q_aug.astype(F32).T.astype(BF16))
    kv_pos = lax.broadcasted_iota(I32, (tk, tq), 0)
    q_pos = lax.broadcasted_iota(I32, (tk, tq), 1)

    def logits(hd, c):
        start = pl.multiple_of(c * tk, tk)
        cols = slice(hd * dh, (hd + 1) * dh)
        k_aug = jnp.concatenate([k_ref[pl.ds(start, tk), cols],
                                 fk_ref[pl.ds(start, tk), cols]], axis=1)
        return _dot(k_aug, q_t[hd])

    def update(st, hd, c, carry, diag_offset):
        m, l, acc = carry
        if diag_offset is not None:
            st = jnp.where(kv_pos + diag_offset <= q_pos, st, -jnp.inf)
        m_new = jnp.maximum(m, jnp.max(st, axis=0, keepdims=True))
        p = jnp.exp2(st - m_new)
        alpha = jnp.exp2(m - m_new)
        l = alpha * l + jnp.sum(p, axis=0, keepdims=True)
        acc = alpha * acc + _dot(vt_ref[hd, c], p.astype(BF16))
        return m_new, l, acc

    first = 2 * i
    for hd in heads:
        sa_ref[hd] = logits(hd, 0)

    def pair(t, carries):
        out = []
        for hd in heads:
            sb_ref[hd] = logits(hd, 2 * t + 1)
        for hd in heads:
            out.append(update(sa_ref[hd], hd, 2 * t, carries[hd], None))
        for hd in heads:
            sa_ref[hd] = logits(hd, 2 * t + 2)
        return tuple(update(sb_ref[hd], hd, 2 * t + 1, out[hd], None) for hd in heads)

    init = (jnp.full((1, tq), -jnp.inf, F32), jnp.zeros((1, tq), F32),
            jnp.zeros((dh, tq), F32))
    carries = lax.fori_loop(0, i, pair, tuple(init for _ in heads))
    for hd in heads:
        sb_ref[hd] = logits(hd, first + 1)
    for hd in heads:
        carry = update(sa_ref[hd], hd, first, carries[hd], 0)
        _, l, acc = update(sb_ref[hd], hd, first + 1, carry, tk)
        o_ref[:, hd * dh:(hd + 1) * dh] = (acc / l).T.astype(o_ref.dtype)


def _attention(y, faug, tq, tk, nh):
    s = y.shape[0]
    width = nh * ATT_HEAD_DIM
    groups = ATT_HEADS // nh
    kern = functools.partial(_attn_kernel, tq=tq, tk=tk, nh=nh)
    q_block = lambda g0: pl.BlockSpec((tq, width), lambda g, i: (i, g0 + g))
    kv_block = lambda g0: pl.BlockSpec((s, width), lambda g, i: (0, g0 + g))
    return pl.pallas_call(
        kern,
        grid=(groups, s // tq),
        in_specs=[q_block(0), q_block(0), kv_block(groups), kv_block(groups),
                  kv_block(2 * groups)],
        out_specs=pl.BlockSpec((tq, width), lambda g, i: (i, g)),
        out_shape=jax.ShapeDtypeStruct((s, D_ATT), BF16),
        scratch_shapes=[pltpu.VMEM((nh, s // tk, ATT_HEAD_DIM, tk), BF16),
                        pltpu.VMEM((nh, tk, tq), F32), pltpu.VMEM((nh, tk, tq), F32)],
        compiler_params=_params("arbitrary", "arbitrary"),
        name="fox_attention",
    )(y, faug, y, faug, y)


def _expand_heads(w, n_heads):
    length = w.shape[0]
    lane = lax.broadcasted_iota(I32, (length, LANES), 1)
    blocks = []
    for m in range(n_heads // 2):
        blocks.append(jnp.where(lane < SSD_HEAD_DIM, w[:, 2 * m:2 * m + 1],
                                w[:, 2 * m + 1:2 * m + 2]))
    return jnp.concatenate(blocks, axis=1)


def _ssd_kernel(win_ref, tail_ref, dt_ref, convw_ref, convb_ref, a_ref, dskip_ref,
                normw_ref, tri_ref, o_ref, ubuf_ref, state_ref):
    length = SSD_CHUNK

    @pl.when(pl.program_id(0) == 0)
    def _():
        ubuf_ref[0:CONV_HALO, :] = jnp.zeros((CONV_HALO, D_CONV), F32)
        state_ref[...] = jnp.zeros_like(state_ref)

    win = jnp.concatenate([win_ref[...].astype(F32), tail_ref[...]], axis=1)
    win = jnp.concatenate([win[:, ATT_HEADS:], win[:, :ATT_HEADS]], axis=1)
    z = win[:, :D_SSD]

    ubuf_ref[CONV_HALO:CONV_HALO + length, :] = win[:, D_SSD:D_SSD + D_CONV]
    conv = convb_ref[...]
    for k in range(CONV_WIDTH):
        off = CONV_HALO - (CONV_WIDTH - 1) + k
        conv = conv + convw_ref[k:k + 1, :] * ubuf_ref[off:off + length, :]
    tail = ubuf_ref[length:length + CONV_HALO, :]
    ubuf_ref[0:CONV_HALO, :] = tail
    xc = _silu(conv)
    xs = xc[:, :D_SSD]

    dt = dt_ref[...]
    da = dt * a_ref[...]
    acum = _exact_left_dot(tri_ref[...], da)
    acum_t = acum.T
    a_last = acum[length - 1:length, :]
    w_off = _expand_heads(jnp.exp(acum), SSD_HEADS)
    w_state = _expand_heads(jnp.exp(a_last - acum) * dt, SSD_HEADS)
    w_dt = _expand_heads(dt, SSD_HEADS)
    chunk_decay = _expand_heads(jnp.exp(a_last), SSD_HEADS)

    x_dt = (xs * w_dt).astype(BF16)
    x_state = (xs * w_state).astype(BF16)
    row = lax.broadcasted_iota(I32, (length, length), 0)
    col = lax.broadcasted_iota(I32, (length, length), 1)
    causal = col <= row
    grp_lane = lax.broadcasted_iota(I32, (length, D_GROUP), 1) // SSD_HEAD_DIM

    y_parts = []
    for g in range(SSD_GROUPS):
        b_g = xc[:, D_SSD + g * SSD_STATE:D_SSD + (g + 1) * SSD_STATE].astype(BF16)
        c_g = xc[:, D_SSD + D_BC + g * SSD_STATE:
                 D_SSD + D_BC + (g + 1) * SSD_STATE].astype(BF16)
        cb = lax.dot_general(c_g, b_g, (((1,), (1,)), ((), ())),
                             preferred_element_type=F32)
        cols = slice(g * D_GROUP, (g + 1) * D_GROUP)
        x_dt_g = x_dt[:, cols]
        m_blocks, x_blocks = [], []
        for r in range(HEADS_PER_GROUP):
            hd = g * HEADS_PER_GROUP + r
            seg = acum[:, hd:hd + 1] - acum_t[hd:hd + 1, :]
            decay = jnp.exp(jnp.where(causal, seg, -jnp.inf))
            m_blocks.append((cb * decay).astype(BF16))
            x_blocks.append(jnp.where(grp_lane == r, x_dt_g, jnp.zeros_like(x_dt_g)))
        y_diag = _dot(jnp.concatenate(m_blocks, axis=1),
                      jnp.concatenate(x_blocks, axis=0))
        st = state_ref[g]
        y_off = _dot(c_g, st.astype(BF16)) * w_off[:, cols]
        y_parts.append(y_diag + y_off)
        new_st = lax.dot_general(b_g, x_state[:, cols], (((0,), (0,)), ((), ())),
                                 preferred_element_type=F32)
        state_ref[g] = st * chunk_decay[:, cols] + new_st

    y = jnp.concatenate(y_parts, axis=1) + dskip_ref[...] * xs
    yg = y * _silu(z)
    outs = []
    for g in range(SSD_GROUPS):
        blk = yg[:, g * D_GROUP:(g + 1) * D_GROUP]
        ms = jnp.mean(blk * blk, axis=1, keepdims=True)
        outs.append(blk * lax.rsqrt(ms + RMS_EPS))
    o_ref[...] = (jnp.concatenate(outs, axis=1) * normw_ref[...]).astype(o_ref.dtype)


def _ssd(y, tail, dtv, conv_w, conv_b, a_row, dskip_row, normw_row):
    s = y.shape[0]
    length = SSD_CHUNK
    tri = jnp.tril(jnp.ones((length, length), F32)).astype(BF16)
    full = lambda shape: pl.BlockSpec(shape, lambda c: (0,) * len(shape))
    return pl.pallas_call(
        _ssd_kernel,
        grid=(s // length,),
        in_specs=[pl.BlockSpec((length, D_WIN), lambda c: (c, COL_WIN // D_WIN)),
                  pl.BlockSpec((length, LANES), lambda c: (c, 0)),
                  pl.BlockSpec((length, LANES), lambda c: (c, 0)),
                  full((CONV_WIDTH, D_CONV)), full((1, D_CONV)), full((1, LANES)),
                  full((1, D_SSD)), full((1, D_SSD)), full((length, length))],
        out_specs=pl.BlockSpec((length, D_SSD), lambda c: (c, 0)),
        out_shape=jax.ShapeDtypeStruct((s, D_SSD), BF16),
        scratch_shapes=[pltpu.VMEM((length + CONV_HALO, D_CONV), F32),
                        pltpu.VMEM((SSD_GROUPS, SSD_STATE, D_GROUP), F32)],
        compiler_params=_params("arbitrary"),
        name="ssd",
    )(y, tail, dtv, conv_w, conv_b, a_row, dskip_row, normw_row, tri)


def _layer_norm(xf, g, b):
    mu = jnp.mean(xf, axis=1, keepdims=True)
    xc = xf - mu
    var = jnp.mean(xc * xc, axis=1, keepdims=True)
    return xc * lax.rsqrt(var + LN_EPS) * g + b


def _resident_weight(shape, layer):
    return pl.BlockSpec((1,) + shape, lambda i, *_: (layer,) + (0,) * len(shape),
                        pipeline_mode=pl.Buffered(1))


def _out_proj_kernel(att_ref, ssd_ref, w_ref, h_ref, g_ref, b_ref, o_ref, obf_ref, w_bf):
    @pl.when(pl.program_id(0) == 0)
    def _():
        w_bf[...] = w_ref[0].astype(BF16)

    mix = _dot(att_ref[...], w_bf[:D_ATT, :]) + _dot(ssd_ref[...], w_bf[D_ATT:, :])
    out = _layer_norm(DEEPNORM_ALPHA * h_ref[...] + mix, g_ref[...], b_ref[...])
    o_ref[...] = out
    obf_ref[...] = out.astype(BF16)


def _out_proj_ln(att, ssd, w_out, layer, h, g, b, tm):
    s = h.shape[0]
    full = lambda shape: pl.BlockSpec(shape, lambda i: (0,) * len(shape))
    return pl.pallas_call(
        _out_proj_kernel,
        grid=(s // tm,),
        in_specs=[pl.BlockSpec((tm, D_ATT), lambda i: (i, 0)),
                  pl.BlockSpec((tm, D_SSD), lambda i: (i, 0)),
                  _resident_weight((D_MODEL, D_MODEL), layer),
                  pl.BlockSpec((tm, D_MODEL), lambda i: (i, 0)),
                  full((1, D_MODEL)), full((1, D_MODEL))],
        out_specs=[pl.BlockSpec((tm, D_MODEL), lambda i: (i, 0)),
                   pl.BlockSpec((tm, D_MODEL), lambda i: (i, 0))],
        out_shape=[jax.ShapeDtypeStruct((s, D_MODEL), F32),
                   jax.ShapeDtypeStruct((s, D_MODEL), BF16)],
        scratch_shapes=[pltpu.VMEM((D_MODEL, D_MODEL), BF16)],
        compiler_params=_params("arbitrary"),
        name="out_proj_ln",
    )(att, ssd, w_out, h, g, b)


def _router_kernel(h_ref, wr_ref, bias_ref, triu_ref, idx_ref, wcol_ref, cnt_ref,
                   carry_ref, *, tm):
    @pl.when(pl.program_id(0) == 0)
    def _():
        carry_ref[...] = jnp.zeros_like(carry_ref)

    h = h_ref[...]
    h_hi = h.astype(BF16)
    h_lo = (h - h_hi.astype(F32)).astype(BF16)
    logits = _dot(h_hi, wr_ref[0]) + (_dot(h_hi, wr_ref[1]) + _dot(h_lo, wr_ref[0]))
    lt = logits.T
    scores = _sigmoid(lt[:N_EXPERTS, :])
    biased = scores + bias_ref[...]
    npg = EXPERTS_PER_GROUP
    s_r = [scores[npg * r:npg * (r + 1), :] for r in range(npg)]
    b_r = [biased[npg * r:npg * (r + 1), :] for r in range(npg)]

    gs = None
    for a in range(npg):
        for b in range(a + 1, npg):
            pair = b_r[a] + b_r[b]
            gs = pair if gs is None else jnp.maximum(gs, pair)
    best = jnp.zeros((1, tm), I32)
    top = gs[0:1, :]
    for g in range(1, N_EXPERT_GROUPS):
        better = gs[g:g + 1, :] > top
        best = jnp.where(better, g, best)
        top = jnp.where(better, gs[g:g + 1, :], top)

    def pick_group(a):
        out = a[0:1, :]
        for g in range(1, N_EXPERT_GROUPS):
            out = jnp.where(best == g, a[g:g + 1, :], out)
        return out

    sb = [pick_group(b) for b in b_r]
    ss = [pick_group(s) for s in s_r]
    first = jnp.zeros((1, tm), I32)
    fmax = sb[0]
    for r in range(1, npg):
        better = sb[r] > fmax
        first = jnp.where(better, r, first)
        fmax = jnp.where(better, sb[r], fmax)
    second = jnp.zeros((1, tm), I32)
    smax = jnp.full((1, tm), -jnp.inf, F32)
    for r in range(npg):
        cand = jnp.where(first == r, -jnp.inf, sb[r])
        better = cand > smax
        second = jnp.where(better, r, second)
        smax = jnp.where(better, cand, smax)

    def pick_expert(which):
        out = ss[0]
        for r in range(1, npg):
            out = jnp.where(which == r, ss[r], out)
        return out

    a0 = pick_expert(first)
    a1 = pick_expert(second)
    denom = a0 + a1
    e0 = best * npg + first
    e1 = best * npg + second

    erow = lax.broadcasted_iota(I32, (N_EXPERTS, tm), 0)
    hit0 = erow == e0
    hit1 = erow == e1
    onehot = jnp.where(hit0, 1.0, 0.0) + jnp.where(hit1, 1.0, 0.0)
    cum = _dot(onehot.astype(BF16), triu_ref[...])
    carry = carry_ref[...][:, 0:1]
    before = cum - onehot + carry
    rank0 = jnp.sum(jnp.where(hit0, before, 0.0), axis=0, keepdims=True)
    rank1 = jnp.sum(jnp.where(hit1, before, 0.0), axis=0, keepdims=True)
    new_carry = carry_ref[...] + cum[:, tm - 1:tm]
    carry_ref[...] = new_carry
    cnt_ref[...] = new_carry

    zeros_i = jnp.zeros((SUBLANES - 4, tm), I32)
    idx_ref[...] = jnp.concatenate(
        [e0, e1, rank0.astype(I32), rank1.astype(I32), zeros_i], axis=0)
    wrows = jnp.concatenate([a0 / denom, a1 / denom,
                             jnp.zeros((LANES - 2, tm), F32)], axis=0)
    wcol_ref[...] = wrows.T


def _router(h, w_router2, bias_col, tm):
    s = h.shape[0]
    triu = jnp.triu(jnp.ones((tm, tm), F32)).astype(BF16)
    kern = functools.partial(_router_kernel, tm=tm)
    return pl.pallas_call(
        kern,
        grid=(s // tm,),
        in_specs=[pl.BlockSpec((tm, D_MODEL), lambda i: (i, 0)),
                  pl.BlockSpec((2, D_MODEL, LANES), lambda i: (0, 0, 0)),
                  pl.BlockSpec((N_EXPERTS, 1), lambda i: (0, 0)),
                  pl.BlockSpec((tm, tm), lambda i: (0, 0))],
        out_specs=[pl.BlockSpec((SUBLANES, tm), lambda i: (0, i)),
                   pl.BlockSpec((tm, LANES), lambda i: (i, 0)),
                   pl.BlockSpec((N_EXPERTS, LANES), lambda i: (0, 0))],
        out_shape=[jax.ShapeDtypeStruct((SUBLANES, s), I32),
                   jax.ShapeDtypeStruct((s, LANES), F32),
                   jax.ShapeDtypeStruct((N_EXPERTS, LANES), F32)],
        scratch_shapes=[pltpu.VMEM((N_EXPERTS, LANES), F32)],
        compiler_params=_params("arbitrary"),
        name="router",
    )(h, w_router2, bias_col, triu)


def _work_items(counts, tm, n_tiles):
    ends = jnp.cumsum(counts)
    starts = ends - counts
    first_tile = starts // tm
    last_tile = (ends - 1) // tm
    used = counts > 0
    n_e = jnp.where(used, last_tile - first_tile + 1, 0)
    item_end = jnp.cumsum(n_e)
    item_start = item_end - n_e
    n_items = n_tiles + N_EXPERTS - 1
    w = jnp.arange(n_items, dtype=I32)
    total = item_end[-1]
    valid = w < total
    wc = jnp.minimum(w, total - 1)
    e_w = jnp.minimum(jnp.sum((item_end[None, :] <= wc[:, None]).astype(I32), axis=1),
                      N_EXPERTS - 1)
    tile_w = first_tile[e_w] + (wc - item_start[e_w])
    lo = jnp.clip(starts[e_w] - tile_w * tm, 0, tm)
    hi = jnp.clip(ends[e_w] - tile_w * tm, 0, tm)
    lo = jnp.where(valid, lo, 0)
    hi = jnp.where(valid, hi, 0)
    ids = jnp.arange(N_EXPERTS, dtype=I32)
    later_used = jnp.logical_and(used[None, :], ids[None, :] > ids[:, None])
    next_e = jnp.min(jnp.where(later_used, ids[None, :], N_EXPERTS), axis=1)
    next_e = jnp.where(next_e == N_EXPERTS, -1, next_e)
    slot_e = (jnp.cumsum(used.astype(I32)) - 1) % 2
    return (tile_w.astype(I32), e_w, lo.astype(I32), hi.astype(I32),
            slot_e[e_w].astype(I32), next_e[e_w].astype(I32))


def _dispatch_kernel(pos0_ref, pos1_ref, h_ref, xs_ref, sems, *, tm):
    base = pl.program_id(0) * tm

    def issue(r, carry):
        src = h_ref.at[pl.ds(r, 1), :]
        pltpu.make_async_copy(src, xs_ref.at[pl.ds(pos0_ref[base + r], 1), :],
                              sems.at[0]).start(priority=0)
        pltpu.make_async_copy(src, xs_ref.at[pl.ds(pos1_ref[base + r], 1), :],
                              sems.at[1]).start(priority=1)
        return carry

    lax.fori_loop(0, tm, issue, 0, unroll=8)
    for k in range(2):
        pltpu.make_async_copy(h_ref, xs_ref.at[pl.ds(0, tm), :], sems.at[k]).wait()


def _dispatch(pos0, pos1, h, tm):
    s = h.shape[0]
    kern = functools.partial(_dispatch_kernel, tm=tm)
    return pl.pallas_call(
        kern,
        grid_spec=pltpu.PrefetchScalarGridSpec(
            num_scalar_prefetch=2,
            grid=(s // tm,),
            in_specs=[pl.BlockSpec((tm, D_MODEL), lambda i, p0, p1: (i, 0))],
            out_specs=pl.BlockSpec(memory_space=pl.ANY),
            scratch_shapes=[pltpu.SemaphoreType.DMA((2,))]),
        out_shape=jax.ShapeDtypeStruct((2 * s, D_MODEL), F32),
        compiler_params=_params("arbitrary"),
        name="dispatch",
    )(pos0, pos1, h)


def _item_flags(tile_ref, exp_ref):
    w = pl.program_id(0)
    prev = jnp.maximum(w - 1, 0)
    fresh_tile = jnp.logical_or(w == 0, tile_ref[w] != tile_ref[prev])
    new_expert = jnp.logical_or(w == 0, exp_ref[w] != exp_ref[prev])
    return fresh_tile, new_expert


def _store_rows(o_ref, val, lo, hi, fresh_tile):
    row = lax.broadcasted_iota(I32, (val.shape[0], 1), 0)
    keep = jnp.logical_and(row >= lo, row < hi)

    @pl.when(fresh_tile)
    def _():
        o_ref[...] = jnp.where(keep, val, jnp.zeros_like(val))

    @pl.when(jnp.logical_not(fresh_tile))
    def _():
        o_ref[...] = jnp.where(keep, val, o_ref[...])


def _expert_weights(exp_ref, slot_ref, next_ref, new_expert, layer, w_hbm, stage_ref,
                    w_bf, sems):
    w = pl.program_id(0)
    n_mats = len(w_hbm)

    def fetch(expert, slot):
        return [pltpu.make_async_copy(w_hbm[m].at[layer, expert], stage_ref.at[slot, m],
                                      sems.at[slot, m]) for m in range(n_mats)]

    @pl.when(w == 0)
    def _():
        for copy in fetch(exp_ref[0], slot_ref[0]):
            copy.start()

    @pl.when(new_expert)
    def _():
        slot = slot_ref[w]
        for m, copy in enumerate(fetch(exp_ref[w], slot)):
            copy.wait()
            w_bf[m] = stage_ref[slot, m].astype(BF16)

        @pl.when(next_ref[w] >= 0)
        def _():
            for copy in fetch(next_ref[w], 1 - slot):
                copy.start()


def _moe_up_kernel(tile_ref, exp_ref, lo_ref, hi_ref, slot_ref, next_ref, x_ref, wg_hbm,
                   wu_hbm, o_ref, stage_ref, w_bf, sems, *, layer):
    w = pl.program_id(0)
    fresh_tile, new_expert = _item_flags(tile_ref, exp_ref)
    _expert_weights(exp_ref, slot_ref, next_ref, new_expert, layer, (wg_hbm, wu_hbm),
                    stage_ref, w_bf, sems)

    @pl.when(hi_ref[w] > lo_ref[w])
    def _():
        x = x_ref[...].astype(BF16)
        hid = _silu(_dot(x, w_bf[0])) * _dot(x, w_bf[1])
        _store_rows(o_ref, hid.astype(BF16), lo_ref[w], hi_ref[w], fresh_tile)


def _moe_down_kernel(tile_ref, exp_ref, lo_ref, hi_ref, slot_ref, next_ref, h_ref, wd_hbm,
                     o_ref, stage_ref, w_bf, sems, *, layer):
    w = pl.program_id(0)
    fresh_tile, new_expert = _item_flags(tile_ref, exp_ref)
    _expert_weights(exp_ref, slot_ref, next_ref, new_expert, layer, (wd_hbm,),
                    stage_ref, w_bf, sems)

    @pl.when(hi_ref[w] > lo_ref[w])
    def _():
        y = _dot(h_ref[...], w_bf[0])
        _store_rows(o_ref, y, lo_ref[w], hi_ref[w], fresh_tile)


def _moe(items, xs, w_gate, w_up, w_down, layer, tm):
    rows = xs.shape[0]
    n_items = items[0].shape[0]
    row_tile = lambda width: pl.BlockSpec((tm, width), lambda w, tile, *_: (tile[w], 0))
    in_hbm = pl.BlockSpec(memory_space=pl.ANY)

    def weight_scratch(n_mats, shape):
        return [pltpu.VMEM((2, n_mats) + shape, F32), pltpu.VMEM((n_mats,) + shape, BF16),
                pltpu.SemaphoreType.DMA((2, n_mats))]

    hid = pl.pallas_call(
        functools.partial(_moe_up_kernel, layer=layer),
        grid_spec=pltpu.PrefetchScalarGridSpec(
            num_scalar_prefetch=len(items),
            grid=(n_items,),
            in_specs=[row_tile(D_MODEL), in_hbm, in_hbm],
            out_specs=row_tile(D_EXPERT),
            scratch_shapes=weight_scratch(2, (D_MODEL, D_EXPERT))),
        out_shape=jax.ShapeDtypeStruct((rows, D_EXPERT), BF16),
        compiler_params=_params("arbitrary"),
        name="moe_up",
    )(*items, xs, w_gate, w_up)
    return pl.pallas_call(
        functools.partial(_moe_down_kernel, layer=layer),
        grid_spec=pltpu.PrefetchScalarGridSpec(
            num_scalar_prefetch=len(items),
            grid=(n_items,),
            in_specs=[row_tile(D_EXPERT), in_hbm],
            out_specs=row_tile(D_MODEL),
            scratch_shapes=weight_scratch(1, (D_EXPERT, D_MODEL))),
        out_shape=jax.ShapeDtypeStruct((rows, D_MODEL), F32),
        compiler_params=_params("arbitrary"),
        name="moe_down",
    )(*items, hid, w_down)


def _ple_kernel(pos0_ref, pos1_ref, hbf_ref, h_ref, p_ref, wgate_ref, wple_ref,
                wcol_ref, g_ref, b_ref, ys_ref, o_ref, obf_ref, buf_ref, sems,
                wgate_bf, wple_bf, *, tm):
    base = pl.program_id(0) * tm

    @pl.when(pl.program_id(0) == 0)
    def _():
        wgate_bf[...] = wgate_ref[0].astype(BF16)
        wple_bf[...] = wple_ref[0].astype(BF16)

    def issue(r, carry):
        for k, pos_ref in enumerate((pos0_ref, pos1_ref)):
            pltpu.make_async_copy(ys_ref.at[pl.ds(pos_ref[base + r], 1), :],
                                  buf_ref.at[k, pl.ds(r, 1), :],
                                  sems.at[k]).start(priority=k)
        return carry

    lax.fori_loop(0, tm, issue, 0, unroll=8)
    gate = _sigmoid(_dot(hbf_ref[...], wgate_bf[...]))
    ple = gate * _dot(p_ref[...].astype(BF16), wple_bf[...])
    for k in range(2):
        pltpu.make_async_copy(ys_ref.at[pl.ds(0, tm), :], buf_ref.at[k], sems.at[k]).wait()
    wc = wcol_ref[...]
    moe = wc[:, 0:1] * buf_ref[0] + wc[:, 1:2] * buf_ref[1]
    out = _layer_norm(DEEPNORM_ALPHA * h_ref[...] + moe + ple, g_ref[...], b_ref[...])
    o_ref[...] = out
    obf_ref[...] = out.astype(BF16)


def _ple_combine_ln(pos0, pos1, h_bf, h, p_i, w_ple_gate, w_ple, layer, wcol, g, b, ys, tm):
    s = h.shape[0]
    kern = functools.partial(_ple_kernel, tm=tm)
    tile = lambda width: pl.BlockSpec((tm, width), lambda i, p0, p1: (i, 0))
    full = lambda shape: pl.BlockSpec(shape, lambda i, p0, p1: (0,) * len(shape))
    return pl.pallas_call(
        kern,
        grid_spec=pltpu.PrefetchScalarGridSpec(
            num_scalar_prefetch=2,
            grid=(s // tm,),
            in_specs=[tile(D_MODEL), tile(D_MODEL), tile(PLE_DIM),
                      _resident_weight((D_MODEL, D_MODEL), layer),
                      _resident_weight((PLE_DIM, D_MODEL), layer),
                      tile(LANES), full((1, D_MODEL)), full((1, D_MODEL)),
                      pl.BlockSpec(memory_space=pl.ANY)],
            out_specs=[tile(D_MODEL), tile(D_MODEL)],
            scratch_shapes=[pltpu.VMEM((2, tm, D_MODEL), F32),
                            pltpu.SemaphoreType.DMA((2,)),
                            pltpu.VMEM((D_MODEL, D_MODEL), BF16),
                            pltpu.VMEM((PLE_DIM, D_MODEL), BF16)]),
        out_shape=[jax.ShapeDtypeStruct((s, D_MODEL), F32),
                   jax.ShapeDtypeStruct((s, D_MODEL), BF16)],
        compiler_params=_params("arbitrary"),
        name="ple_combine_ln",
    )(pos0, pos1, h_bf, h, p_i, w_ple_gate, w_ple, wcol, g, b, ys)


def _row(v, width):
    v = v.astype(F32)
    return jnp.concatenate([v, jnp.zeros((width - v.shape[0],), F32)]).reshape(1, width)


def _mixer_layer(h, h_bf, prm, layer):
    s = h.shape[0]
    w_in = prm["w_in"]
    c_f = 3 * D_ATT
    c_dt = D_MAIN + XBC_TAIL
    pad = lambda cols: jnp.concatenate(
        [cols, jnp.zeros((D_MODEL, LANES - cols.shape[1]), cols.dtype)], axis=1)
    w_small = jnp.concatenate(
        [pad(w_in[layer, :, c_f:c_f + ATT_HEADS]), pad(w_in[layer, :, c_dt:]),
         pad(w_in[layer, :, D_MAIN:c_dt])], axis=1)
    b_small = jnp.concatenate([_row(prm["b_forget"][layer], LANES),
                               _row(prm["dt_bias"][layer], LANES),
                               jnp.zeros((1, LANES), F32)], axis=1)
    a_row = _row(-jnp.exp(prm["a_log"][layer].astype(F32)), LANES)
    dskip_row = jnp.repeat(prm["d_skip"][layer].astype(F32), SSD_HEAD_DIM).reshape(1, D_SSD)

    y = _in_proj(h_bf, w_in, layer, min(s, 1024), 512)
    faug, dtv, tail = _gates(h_bf, w_small, b_small, min(s, 256))
    att = _attention(y, faug, min(s, 512), min(s, 256), 2)
    ssd = _ssd(y, tail, dtv, prm["conv_w"][layer], _row(prm["conv_b"][layer], D_CONV),
               a_row, dskip_row, _row(prm["ssd_norm_w"][layer], D_SSD))
    return _out_proj_ln(att, ssd, prm["w_out"], layer, h,
                        _row(prm["ln1_g"][layer], D_MODEL),
                        _row(prm["ln1_b"][layer], D_MODEL), min(s, 256))


def _channel_layer(h, h_bf, p_i, w_router2, bias_col, prm, layer):
    s = h.shape[0]
    tm = min(s, 256)
    idx, wcol, cnt = _router(h, w_router2, bias_col, tm)
    counts = cnt[:, 0].astype(I32)
    offsets = jnp.cumsum(counts) - counts
    pos0 = offsets[idx[0]] + idx[2]
    pos1 = offsets[idx[1]] + idx[3]
    xs = _dispatch(pos0, pos1, h, tm)
    items = _work_items(counts, tm, 2 * s // tm)
    ys = _moe(items, xs, prm["w_gate"], prm["w_up"], prm["w_down"], layer, tm)
    return _ple_combine_ln(pos0, pos1, h_bf, h, p_i, prm["w_ple_gate"], prm["w_ple"],
                           layer, wcol, _row(prm["ln2_g"][layer], D_MODEL),
                           _row(prm["ln2_b"][layer], D_MODEL), ys, tm)


def _prep_router(p):
    order = jnp.arange(N_EXPERTS).reshape(N_EXPERT_GROUPS, EXPERTS_PER_GROUP).T.reshape(-1)
    w = p["w_router"].astype(F32)[:, order]
    w = jnp.concatenate([w, jnp.zeros((D_MODEL, LANES - N_EXPERTS), F32)], axis=1)
    w_hi = w.astype(BF16)
    w_lo = (w - w_hi.astype(F32)).astype(BF16)
    bias_col = p["router_bias"].astype(F32)[order].reshape(N_EXPERTS, 1)
    return jnp.stack([w_hi, w_lo]), bias_col


def kernel(x, p, w_in, b_forget, conv_w, conv_b, dt_bias, a_log, d_skip, ssd_norm_w,
           w_out, ln1_g, ln1_b, w_router, router_bias, w_gate, w_up, w_down, w_ple,
           w_ple_gate, ln2_g, ln2_b):
    prm = dict(w_in=w_in, b_forget=b_forget, conv_w=conv_w, conv_b=conv_b,
               dt_bias=dt_bias, a_log=a_log, d_skip=d_skip, ssd_norm_w=ssd_norm_w,
               w_out=w_out, ln1_g=ln1_g, ln1_b=ln1_b, w_router=w_router,
               router_bias=router_bias, w_gate=w_gate, w_up=w_up, w_down=w_down,
               w_ple=w_ple, w_ple_gate=w_ple_gate, ln2_g=ln2_g, ln2_b=ln2_b)
    batch = x.shape[0]
    w_router2, bias_col = _prep_router(prm)
    outs = []
    for bi in range(batch):
        h = x[bi]
        h_bf = h.astype(BF16)
        for layer in range(w_in.shape[0]):
            h, h_bf = _mixer_layer(h, h_bf, prm, layer)
            h, h_bf = _channel_layer(h, h_bf, p[layer, bi], w_router2, bias_col, prm, layer)
        outs.append(h)
    return jnp.stack(outs)
```

```python
import functools
import math

import jax
import jax.numpy as jnp
from jax import lax
from jax.experimental import pallas as pl
from jax.experimental.pallas import tpu as pltpu

F32 = jnp.float32
BF16 = jnp.bfloat16
I32 = jnp.int32

D_MODEL = 2048
ATT_HEADS = 8
ATT_HEAD_DIM = 128
D_ATT = ATT_HEADS * ATT_HEAD_DIM
SSD_HEADS = 16
SSD_HEAD_DIM = 64
D_SSD = SSD_HEADS * SSD_HEAD_DIM
SSD_GROUPS = 4
HEADS_PER_GROUP = SSD_HEADS // SSD_GROUPS
SSD_STATE = 128
D_GROUP = D_SSD // SSD_GROUPS
CONV_WIDTH = 4
D_BC = SSD_GROUPS * SSD_STATE
D_CONV = D_SSD + 2 * D_BC
N_EXPERTS = 16
N_EXPERT_GROUPS = 4
EXPERTS_PER_GROUP = N_EXPERTS // N_EXPERT_GROUPS
D_EXPERT = 1024
PLE_DIM = 256
DEPTH = 2
DEEPNORM_ALPHA = (2 * DEPTH) ** 0.25
LN_EPS = 1e-5
RMS_EPS = 1e-5
LOG2_E = math.log2(math.e)

LANES = 128
SUBLANES = 8
VMEM_LIMIT_BYTES = 56 * 1024 * 1024

D_MAIN = 3 * D_ATT + D_SSD + D_CONV
COL_WIN = 3 * D_ATT
D_WIN = D_MAIN - COL_WIN
XBC_TAIL = ATT_HEADS

SSD_CHUNK = 128
CONV_HALO = SUBLANES
SUM_ROWS = 2 * SUBLANES


def _params(*sem):
    return pltpu.CompilerParams(dimension_semantics=sem,
                                vmem_limit_bytes=VMEM_LIMIT_BYTES)


def _softplus(u):
    return jnp.maximum(u, 0.0) + jnp.log1p(jnp.exp(-jnp.abs(u)))


def _sigmoid(u):
    return 1.0 / (1.0 + jnp.exp(-u))


def _silu(u):
    return u * _sigmoid(u)


def _split3(v):
    hi = v.astype(BF16)
    r1 = v - hi.astype(F32)
    mid = r1.astype(BF16)
    lo = (r1 - mid.astype(F32)).astype(BF16)
    return hi, mid, lo


def _dot(a, b):
    return jnp.dot(a, b, preferred_element_type=F32)


def _exact_left_dot(ones_bf16, v):
    hi, mid, lo = _split3(v)
    return _dot(ones_bf16, hi) + (_dot(ones_bf16, mid) + _dot(ones_bf16, lo))


def _proj_kernel(x_ref, w_ref, o_ref, w_bf, *, q_blocks):
    @pl.when(pl.program_id(1) == 0)
    def _():
        w_bf[...] = w_ref[0].astype(BF16)

    scale = jnp.where(pl.program_id(0) < q_blocks, LOG2_E * ATT_HEAD_DIM ** -0.5, 1.0)
    o_ref[...] = (_dot(x_ref[...], w_bf[...]) * scale).astype(o_ref.dtype)


def _in_proj(h_bf, w_in, layer, tm, tn):
    s = h_bf.shape[0]
    return pl.pallas_call(
        functools.partial(_proj_kernel, q_blocks=D_ATT // tn),
        grid=(D_MAIN // tn, s // tm),
        in_specs=[pl.BlockSpec((tm, D_MODEL), lambda j, i: (i, 0)),
                  pl.BlockSpec((1, D_MODEL, tn), lambda j, i: (layer, 0, j))],
        out_specs=pl.BlockSpec((tm, tn), lambda j, i: (i, j)),
        out_shape=jax.ShapeDtypeStruct((s, D_MAIN), BF16),
        scratch_shapes=[pltpu.VMEM((D_MODEL, tn), BF16)],
        compiler_params=_params("arbitrary", "arbitrary"),
        name="in_proj",
    )(h_bf, w_in)


def _gates_kernel(x_ref, w_ref, b_ref, tri_ref, sel_ref, ones_ref, faug_ref, dt_ref,
                  tail_ref, carry_ref, w_bf):
    @pl.when(pl.program_id(0) == 0)
    def _():
        carry_ref[...] = jnp.zeros_like(carry_ref)
        w_bf[...] = w_ref[...].astype(BF16)

    v = _dot(x_ref[...], w_bf[...]) + b_ref[...]
    log_f = -_softplus(-v[:, :LANES])
    dt_ref[...] = _softplus(v[:, LANES:2 * LANES])
    tail_ref[...] = v[:, 2 * LANES:]
    fcum = _exact_left_dot(tri_ref[...], log_f) + carry_ref[...]
    carry_ref[...] = fcum[-1:, :]
    hi, mid, lo = _split3(fcum * LOG2_E)
    aug = (_dot(hi, sel_ref[0]) + _dot(mid, sel_ref[1]) + _dot(lo, sel_ref[2])
           + ones_ref[...])
    faug_ref[...] = aug.astype(BF16)


def _forget_layout():
    import numpy as np
    sel = np.zeros((3, LANES, 2 * D_ATT), np.float32)
    ones = np.zeros((1, 2 * D_ATT), np.float32)
    for h in range(ATT_HEADS):
        for k in range(3):
            sel[k, h, ATT_HEAD_DIM * h + k] = 1.0
            sel[k, h, D_ATT + ATT_HEAD_DIM * h + 3 + k] = -1.0
            ones[0, ATT_HEAD_DIM * h + 3 + k] = 1.0
            ones[0, D_ATT + ATT_HEAD_DIM * h + k] = 1.0
    return jnp.asarray(sel, BF16), jnp.asarray(ones, F32)


def _gates(h_bf, w_small, b_small, tm):
    s = h_bf.shape[0]
    tri = jnp.tril(jnp.ones((tm, tm), F32)).astype(BF16)
    sel, ones = _forget_layout()
    return pl.pallas_call(
        _gates_kernel,
        grid=(s // tm,),
        in_specs=[pl.BlockSpec((tm, D_MODEL), lambda i: (i, 0)),
                  pl.BlockSpec((D_MODEL, 3 * LANES), lambda i: (0, 0)),
                  pl.BlockSpec((1, 3 * LANES), lambda i: (0, 0)),
                  pl.BlockSpec((tm, tm), lambda i: (0, 0)),
                  pl.BlockSpec((3, LANES, 2 * D_ATT), lambda i: (0, 0, 0)),
                  pl.BlockSpec((1, 2 * D_ATT), lambda i: (0, 0))],
        out_specs=[pl.BlockSpec((tm, 2 * D_ATT), lambda i: (i, 0)),
                   pl.BlockSpec((tm, LANES), lambda i: (i, 0)),
                   pl.BlockSpec((tm, LANES), lambda i: (i, 0))],
        out_shape=[jax.ShapeDtypeStruct((s, 2 * D_ATT), BF16),
                   jax.ShapeDtypeStruct((s, LANES), F32),
                   jax.ShapeDtypeStruct((s, LANES), F32)],
        scratch_shapes=[pltpu.VMEM((1, LANES), F32),
                        pltpu.VMEM((D_MODEL, 3 * LANES), BF16)],
        compiler_params=_params("arbitrary"),
        name="gates",
    )(h_bf, w_small, b_small, tri, sel, ones)


def _attn_kernel(q_ref, fq_ref, k_ref, fk_ref, v_ref, o_ref, vt_ref, sa_ref, sb_ref,
                 *, tq, tk, nh):
    assert tq == 2 * tk
    i = pl.program_id(1)
    n_chunks = vt_ref.shape[1]
    dh = ATT_HEAD_DIM
    heads = range(nh)

    @pl.when(i == 0)
    def _():
        for hd in heads:
            for c in range(n_chunks):
                blk = v_ref[c * tk:(c + 1) * tk, hd * dh:(hd + 1) * dh]
                vt_ref[hd, c, :dh, :] = blk.astype(F32).T.astype(BF16)
                vt_ref[hd, c, dh:, :] = jnp.ones((SUM_ROWS, tk), BF16)

    q_t = []
    for hd in heads:
        cols = slice(hd * dh, (hd + 1) * dh)
        q_aug = jnp.concatenate([q_ref[:, cols], fq_ref[:, cols]], axis=1)
        q_t.append(q_aug.astype(F32).T.astype(BF16))
    kv_pos = lax.broadcasted_iota(I32, (tk, tq), 0)
    q_pos = lax.broadcasted_iota(I32, (tk, tq), 1)

    def logits(hd, c):
        start = pl.multiple_of(c * tk, tk)
        cols = slice(hd * dh, (hd + 1) * dh)
        k_aug = jnp.concatenate([k_ref[pl.ds(start, tk), cols],
                                 fk_ref[pl.ds(start, tk), cols]], axis=1)
        return _dot(k_aug, q_t[hd])

    def update(st, hd, c, carry, diag_offset):
        m, acc = carry
        if diag_offset is not None:
            st = jnp.where(kv_pos + diag_offset <= q_pos, st, -jnp.inf)
        m_new = jnp.maximum(m, jnp.max(st, axis=0, keepdims=True))
        p = jnp.exp2(st - m_new)
        alpha = jnp.exp2(m - m_new)
        acc = alpha * acc + _dot(vt_ref[hd, c], p.astype(BF16))
        return m_new, acc

    first = 2 * i
    for hd in heads:
        sa_ref[hd] = logits(hd, 0)

    def pair(t, carries):
        out = []
        for hd in heads:
            sb_ref[hd] = logits(hd, 2 * t + 1)
        for hd in heads:
            out.append(update(sa_ref[hd], hd, 2 * t, carries[hd], None))
        for hd in heads:
            sa_ref[hd] = logits(hd, 2 * t + 2)
        return tuple(update(sb_ref[hd], hd, 2 * t + 1, out[hd], None) for hd in heads)

    init = (jnp.full((1, tq), -jnp.inf, F32), jnp.zeros((dh + SUM_ROWS, tq), F32))
    carries = lax.fori_loop(0, i, pair, tuple(init for _ in heads))
    for hd in heads:
        sb_ref[hd] = logits(hd, first + 1)
    for hd in heads:
        carry = update(sa_ref[hd], hd, first, carries[hd], 0)
        _, acc = update(sb_ref[hd], hd, first + 1, carry, tk)
        out = acc[:dh, :] / acc[dh:dh + 1, :]
        o_ref[:, hd * dh:(hd + 1) * dh] = out.T.astype(o_ref.dtype)


def _attention(y, faug, tq, tk, nh):
    s = y.shape[0]
    width = nh * ATT_HEAD_DIM
    groups = ATT_HEADS // nh
    kern = functools.partial(_attn_kernel, tq=tq, tk=tk, nh=nh)
    q_block = lambda g0: pl.BlockSpec((tq, width), lambda g, i: (i, g0 + g))
    kv_block = lambda g0: pl.BlockSpec((s, width), lambda g, i: (0, g0 + g))
    return pl.pallas_call(
        kern,
        grid=(groups, s // tq),
        in_specs=[q_block(0), q_block(0), kv_block(groups), kv_block(groups),
                  kv_block(2 * groups)],
        out_specs=pl.BlockSpec((tq, width), lambda g, i: (i, g)),
        out_shape=jax.ShapeDtypeStruct((s, D_ATT), BF16),
        scratch_shapes=[pltpu.VMEM((nh, s // tk, ATT_HEAD_DIM + SUM_ROWS, tk), BF16),
                        pltpu.VMEM((nh, tk, tq), F32), pltpu.VMEM((nh, tk, tq), F32)],
        compiler_params=_params("arbitrary", "arbitrary"),
        name="fox_attention",
    )(y, faug, y, faug, y)


def _expand_heads(w, n_heads):
    length = w.shape[0]
    lane = lax.broadcasted_iota(I32, (length, LANES), 1)
    blocks = []
    for m in range(n_heads // 2):
        blocks.append(jnp.where(lane < SSD_HEAD_DIM, w[:, 2 * m:2 * m + 1],
                                w[:, 2 * m + 1:2 * m + 2]))
    return jnp.concatenate(blocks, axis=1)


def _ssd_kernel(win_ref, tail_ref, dt_ref, convw_ref, convb_ref, a_ref, dskip_ref,
                normw_ref, tri_ref, o_ref, ubuf_ref, state_ref):
    length = SSD_CHUNK

    @pl.when(pl.program_id(0) == 0)
    def _():
        ubuf_ref[0:CONV_HALO, :] = jnp.zeros((CONV_HALO, D_CONV), F32)
        state_ref[...] = jnp.zeros_like(state_ref)

    win = jnp.concatenate([win_ref[...].astype(F32), tail_ref[...]], axis=1)
    win = jnp.concatenate([win[:, ATT_HEADS:], win[:, :ATT_HEADS]], axis=1)
    z = win[:, :D_SSD]

    ubuf_ref[CONV_HALO:CONV_HALO + length, :] = win[:, D_SSD:D_SSD + D_CONV]
    conv = convb_ref[...]
    for k in range(CONV_WIDTH):
        off = CONV_HALO - (CONV_WIDTH - 1) + k
        conv = conv + convw_ref[k:k + 1, :] * ubuf_ref[off:off + length, :]
    tail = ubuf_ref[length:length + CONV_HALO, :]
    ubuf_ref[0:CONV_HALO, :] = tail
    xc = _silu(conv)
    xs = xc[:, :D_SSD]

    dt = dt_ref[...]
    da = dt * a_ref[...]
    acum = _exact_left_dot(tri_ref[...], da)
    acum_t = acum.T
    a_last = acum[length - 1:length, :]
    w_off = _expand_heads(jnp.exp(acum), SSD_HEADS)
    w_state = _expand_heads(jnp.exp(a_last - acum) * dt, SSD_HEADS)
    w_dt = _expand_heads(dt, SSD_HEADS)
    chunk_decay = _expand_heads(jnp.exp(a_last), SSD_HEADS)

    x_dt = (xs * w_dt).astype(BF16)
    x_state = (xs * w_state).astype(BF16)
    row = lax.broadcasted_iota(I32, (length, length), 0)
    col = lax.broadcasted_iota(I32, (length, length), 1)
    causal = col <= row
    grp_lane = lax.broadcasted_iota(I32, (length, D_GROUP), 1) // SSD_HEAD_DIM

    y_parts = []
    for g in range(SSD_GROUPS):
        b_g = xc[:, D_SSD + g * SSD_STATE:D_SSD + (g + 1) * SSD_STATE].astype(BF16)
        c_g = xc[:, D_SSD + D_BC + g * SSD_STATE:
                 D_SSD + D_BC + (g + 1) * SSD_STATE].astype(BF16)
        cb = lax.dot_general(c_g, b_g, (((1,), (1,)), ((), ())),
                             preferred_element_type=F32)
        cols = slice(g * D_GROUP, (g + 1) * D_GROUP)
        x_dt_g = x_dt[:, cols]
        m_blocks, x_blocks = [], []
        for r in range(HEADS_PER_GROUP):
            hd = g * HEADS_PER_GROUP + r
            seg = acum[:, hd:hd + 1] - acum_t[hd:hd + 1, :]
            decay = jnp.exp(jnp.where(causal, seg, -jnp.inf))
            m_blocks.append((cb * decay).astype(BF16))
            x_blocks.append(jnp.where(grp_lane == r, x_dt_g, jnp.zeros_like(x_dt_g)))
        y_diag = _dot(jnp.concatenate(m_blocks, axis=1),
                      jnp.concatenate(x_blocks, axis=0))
        st = state_ref[g]
        y_off = _dot(c_g, st.astype(BF16)) * w_off[:, cols]
        y_parts.append(y_diag + y_off)
        new_st = lax.dot_general(b_g, x_state[:, cols], (((0,), (0,)), ((), ())),
                                 preferred_element_type=F32)
        state_ref[g] = st * chunk_decay[:, cols] + new_st

    y = jnp.concatenate(y_parts, axis=1) + dskip_ref[...] * xs
    yg = y * _silu(z)
    outs = []
    for g in range(SSD_GROUPS):
        blk = yg[:, g * D_GROUP:(g + 1) * D_GROUP]
        ms = jnp.mean(blk * blk, axis=1, keepdims=True)
        outs.append(blk * lax.rsqrt(ms + RMS_EPS))
    o_ref[...] = (jnp.concatenate(outs, axis=1) * normw_ref[...]).astype(o_ref.dtype)


def _ssd(y, tail, dtv, conv_w, conv_b, a_row, dskip_row, normw_row):
    s = y.shape[0]
    length = SSD_CHUNK
    tri = jnp.tril(jnp.ones((length, length), F32)).astype(BF16)
    full = lambda shape: pl.BlockSpec(shape, lambda c: (0,) * len(shape))
    return pl.pallas_call(
        _ssd_kernel,
        grid=(s // length,),
        in_specs=[pl.BlockSpec((length, D_WIN), lambda c: (c, COL_WIN // D_WIN)),
                  pl.BlockSpec((length, LANES), lambda c: (c, 0)),
                  pl.BlockSpec((length, LANES), lambda c: (c, 0)),
                  full((CONV_WIDTH, D_CONV)), full((1, D_CONV)), full((1, LANES)),
                  full((1, D_SSD)), full((1, D_SSD)), full((length, length))],
        out_specs=pl.BlockSpec((length, D_SSD), lambda c: (c, 0)),
        out_shape=jax.ShapeDtypeStruct((s, D_SSD), BF16),
        scratch_shapes=[pltpu.VMEM((length + CONV_HALO, D_CONV), F32),
                        pltpu.VMEM((SSD_GROUPS, SSD_STATE, D_GROUP), F32)],
        compiler_params=_params("arbitrary"),
        name="ssd",
    )(y, tail, dtv, conv_w, conv_b, a_row, dskip_row, normw_row, tri)


def _layer_norm(xf, g, b):
    mu = jnp.mean(xf, axis=1, keepdims=True)
    xc = xf - mu
    var = jnp.mean(xc * xc, axis=1, keepdims=True)
    return xc * lax.rsqrt(var + LN_EPS) * g + b


def _resident_weight(shape, layer):
    return pl.BlockSpec((1,) + shape, lambda i, *_: (layer,) + (0,) * len(shape),
                        pipeline_mode=pl.Buffered(1))


def _out_proj_kernel(att_ref, ssd_ref, w_ref, h_ref, g_ref, b_ref, o_ref, obf_ref, w_bf):
    @pl.when(pl.program_id(0) == 0)
    def _():
        w_bf[...] = w_ref[0].astype(BF16)

    mix = _dot(att_ref[...], w_bf[:D_ATT, :]) + _dot(ssd_ref[...], w_bf[D_ATT:, :])
    out = _layer_norm(DEEPNORM_ALPHA * h_ref[...] + mix, g_ref[...], b_ref[...])
    o_ref[...] = out
    obf_ref[...] = out.astype(BF16)


def _out_proj_ln(att, ssd, w_out, layer, h, g, b, tm):
    s = h.shape[0]
    full = lambda shape: pl.BlockSpec(shape, lambda i: (0,) * len(shape))
    return pl.pallas_call(
        _out_proj_kernel,
        grid=(s // tm,),
        in_specs=[pl.BlockSpec((tm, D_ATT), lambda i: (i, 0)),
                  pl.BlockSpec((tm, D_SSD), lambda i: (i, 0)),
                  _resident_weight((D_MODEL, D_MODEL), layer),
                  pl.BlockSpec((tm, D_MODEL), lambda i: (i, 0)),
                  full((1, D_MODEL)), full((1, D_MODEL))],
        out_specs=[pl.BlockSpec((tm, D_MODEL), lambda i: (i, 0)),
                   pl.BlockSpec((tm, D_MODEL), lambda i: (i, 0))],
        out_shape=[jax.ShapeDtypeStruct((s, D_MODEL), F32),
                   jax.ShapeDtypeStruct((s, D_MODEL), BF16)],
        scratch_shapes=[pltpu.VMEM((D_MODEL, D_MODEL), BF16)],
        compiler_params=_params("arbitrary"),
        name="out_proj_ln",
    )(att, ssd, w_out, h, g, b)


def _router_kernel(h_ref, wr_ref, bias_ref, triu_ref, idx_ref, wcol_ref, cnt_ref,
                   carry_ref, *, tm):
    @pl.when(pl.program_id(0) == 0)
    def _():
        carry_ref[...] = jnp.zeros_like(carry_ref)

    h = h_ref[...]
    h_hi = h.astype(BF16)
    h_lo = (h - h_hi.astype(F32)).astype(BF16)
    logits = _dot(h_hi, wr_ref[0]) + (_dot(h_hi, wr_ref[1]) + _dot(h_lo, wr_ref[0]))
    lt = logits.T
    scores = _sigmoid(lt[:N_EXPERTS, :])
    biased = scores + bias_ref[...]
    npg = EXPERTS_PER_GROUP
    s_r = [scores[npg * r:npg * (r + 1), :] for r in range(npg)]
    b_r = [biased[npg * r:npg * (r + 1), :] for r in range(npg)]

    gs = None
    for a in range(npg):
        for b in range(a + 1, npg):
            pair = b_r[a] + b_r[b]
            gs = pair if gs is None else jnp.maximum(gs, pair)
    best = jnp.zeros((1, tm), I32)
    top = gs[0:1, :]
    for g in range(1, N_EXPERT_GROUPS):
        better = gs[g:g + 1, :] > top
        best = jnp.where(better, g, best)
        top = jnp.where(better, gs[g:g + 1, :], top)

    def pick_group(a):
        out = a[0:1, :]
        for g in range(1, N_EXPERT_GROUPS):
            out = jnp.where(best == g, a[g:g + 1, :], out)
        return out

    sb = [pick_group(b) for b in b_r]
    ss = [pick_group(s) for s in s_r]
    first = jnp.zeros((1, tm), I32)
    fmax = sb[0]
    for r in range(1, npg):
        better = sb[r] > fmax
        first = jnp.where(better, r, first)
        fmax = jnp.where(better, sb[r], fmax)
    second = jnp.zeros((1, tm), I32)
    smax = jnp.full((1, tm), -jnp.inf, F32)
    for r in range(npg):
        cand = jnp.where(first == r, -jnp.inf, sb[r])
        better = cand > smax
        second = jnp.where(better, r, second)
        smax = jnp.where(better, cand, smax)

    def pick_expert(which):
        out = ss[0]
        for r in range(1, npg):
            out = jnp.where(which == r, ss[r], out)
        return out

    a0 = pick_expert(first)
    a1 = pick_expert(second)
    denom = a0 + a1
    e0 = best * npg + first
    e1 = best * npg + second

    erow = lax.broadcasted_iota(I32, (N_EXPERTS, tm), 0)
    hit0 = erow == e0
    hit1 = erow == e1
    onehot = jnp.where(hit0, 1.0, 0.0) + jnp.where(hit1, 1.0, 0.0)
    cum = _dot(onehot.astype(BF16), triu_ref[...])
    carry = carry_ref[...][:, 0:1]
    before = cum - onehot + carry
    rank0 = jnp.sum(jnp.where(hit0, before, 0.0), axis=0, keepdims=True)
    rank1 = jnp.sum(jnp.where(hit1, before, 0.0), axis=0, keepdims=True)
    new_carry = carry_ref[...] + cum[:, tm - 1:tm]
    carry_ref[...] = new_carry
    cnt_ref[...] = new_carry

    zeros_i = jnp.zeros((SUBLANES - 4, tm), I32)
    idx_ref[...] = jnp.concatenate(
        [e0, e1, rank0.astype(I32), rank1.astype(I32), zeros_i], axis=0)
    wrows = jnp.concatenate([a0 / denom, a1 / denom,
                             jnp.zeros((LANES - 2, tm), F32)], axis=0)
    wcol_ref[...] = wrows.T


def _router(h, w_router2, bias_col, tm):
    s = h.shape[0]
    triu = jnp.triu(jnp.ones((tm, tm), F32)).astype(BF16)
    kern = functools.partial(_router_kernel, tm=tm)
    return pl.pallas_call(
        kern,
        grid=(s // tm,),
        in_specs=[pl.BlockSpec((tm, D_MODEL), lambda i: (i, 0)),
                  pl.BlockSpec((2, D_MODEL, LANES), lambda i: (0, 0, 0)),
                  pl.BlockSpec((N_EXPERTS, 1), lambda i: (0, 0)),
                  pl.BlockSpec((tm, tm), lambda i: (0, 0))],
        out_specs=[pl.BlockSpec((SUBLANES, tm), lambda i: (0, i)),
                   pl.BlockSpec((tm, LANES), lambda i: (i, 0)),
                   pl.BlockSpec((N_EXPERTS, LANES), lambda i: (0, 0))],
        out_shape=[jax.ShapeDtypeStruct((SUBLANES, s), I32),
                   jax.ShapeDtypeStruct((s, LANES), F32),
                   jax.ShapeDtypeStruct((N_EXPERTS, LANES), F32)],
        scratch_shapes=[pltpu.VMEM((N_EXPERTS, LANES), F32)],
        compiler_params=_params("arbitrary"),
        name="router",
    )(h, w_router2, bias_col, triu)


def _work_items(counts, tm, n_tiles):
    ends = jnp.cumsum(counts)
    starts = ends - counts
    first_tile = starts // tm
    last_tile = (ends - 1) // tm
    used = counts > 0
    n_e = jnp.where(used, last_tile - first_tile + 1, 0)
    item_end = jnp.cumsum(n_e)
    item_start = item_end - n_e
    n_items = n_tiles + N_EXPERTS - 1
    w = jnp.arange(n_items, dtype=I32)
    total = item_end[-1]
    valid = w < total
    wc = jnp.minimum(w, total - 1)
    e_w = jnp.minimum(jnp.sum((item_end[None, :] <= wc[:, None]).astype(I32), axis=1),
                      N_EXPERTS - 1)
    tile_w = first_tile[e_w] + (wc - item_start[e_w])
    lo = jnp.clip(starts[e_w] - tile_w * tm, 0, tm)
    hi = jnp.clip(ends[e_w] - tile_w * tm, 0, tm)
    lo = jnp.where(valid, lo, 0)
    hi = jnp.where(valid, hi, 0)
    ids = jnp.arange(N_EXPERTS, dtype=I32)
    later_used = jnp.logical_and(used[None, :], ids[None, :] > ids[:, None])
    next_e = jnp.min(jnp.where(later_used, ids[None, :], N_EXPERTS), axis=1)
    next_e = jnp.where(next_e == N_EXPERTS, -1, next_e)
    slot_e = (jnp.cumsum(used.astype(I32)) - 1) % 2
    return (tile_w.astype(I32), e_w, lo.astype(I32), hi.astype(I32),
            slot_e[e_w].astype(I32), next_e[e_w].astype(I32))


def _dispatch_kernel(pos0_ref, pos1_ref, h_ref, xs_ref, sems, *, tm):
    base = pl.program_id(0) * tm

    def issue(r, carry):
        src = h_ref.at[pl.ds(r, 1), :]
        pltpu.make_async_copy(src, xs_ref.at[pl.ds(pos0_ref[base + r], 1), :],
                              sems.at[0]).start(priority=0)
        pltpu.make_async_copy(src, xs_ref.at[pl.ds(pos1_ref[base + r], 1), :],
                              sems.at[1]).start(priority=1)
        return carry

    lax.fori_loop(0, tm, issue, 0, unroll=8)
    for k in range(2):
        pltpu.make_async_copy(h_ref, xs_ref.at[pl.ds(0, tm), :], sems.at[k]).wait()


def _dispatch(pos0, pos1, h, tm):
    s = h.shape[0]
    kern = functools.partial(_dispatch_kernel, tm=tm)
    return pl.pallas_call(
        kern,
        grid_spec=pltpu.PrefetchScalarGridSpec(
            num_scalar_prefetch=2,
            grid=(s // tm,),
            in_specs=[pl.BlockSpec((tm, D_MODEL), lambda i, p0, p1: (i, 0))],
            out_specs=pl.BlockSpec(memory_space=pl.ANY),
            scratch_shapes=[pltpu.SemaphoreType.DMA((2,))]),
        out_shape=jax.ShapeDtypeStruct((2 * s, D_MODEL), F32),
        compiler_params=_params("arbitrary"),
        name="dispatch",
    )(pos0, pos1, h)


def _item_flags(tile_ref, exp_ref):
    w = pl.program_id(0)
    prev = jnp.maximum(w - 1, 0)
    fresh_tile = jnp.logical_or(w == 0, tile_ref[w] != tile_ref[prev])
    new_expert = jnp.logical_or(w == 0, exp_ref[w] != exp_ref[prev])
    return fresh_tile, new_expert


def _store_rows(o_ref, val, lo, hi, fresh_tile):
    row = lax.broadcasted_iota(I32, (val.shape[0], 1), 0)
    keep = jnp.logical_and(row >= lo, row < hi)

    @pl.when(fresh_tile)
    def _():
        o_ref[...] = jnp.where(keep, val, jnp.zeros_like(val))

    @pl.when(jnp.logical_not(fresh_tile))
    def _():
        o_ref[...] = jnp.where(keep, val, o_ref[...])


def _expert_weights(exp_ref, slot_ref, next_ref, new_expert, layer, w_hbm, stage_ref,
                    w_bf, sems):
    w = pl.program_id(0)
    n_mats = len(w_hbm)

    def fetch(expert, slot):
        return [pltpu.make_async_copy(w_hbm[m].at[layer, expert], stage_ref.at[slot, m],
                                      sems.at[slot, m]) for m in range(n_mats)]

    @pl.when(w == 0)
    def _():
        for copy in fetch(exp_ref[0], slot_ref[0]):
            copy.start(priority=1)

    @pl.when(new_expert)
    def _():
        slot = slot_ref[w]
        for m, copy in enumerate(fetch(exp_ref[w], slot)):
            copy.wait()
            w_bf[m] = stage_ref[slot, m].astype(BF16)

        @pl.when(next_ref[w] >= 0)
        def _():
            for copy in fetch(next_ref[w], 1 - slot):
                copy.start(priority=1)


def _moe_up_kernel(tile_ref, exp_ref, lo_ref, hi_ref, slot_ref, next_ref, x_ref, wg_hbm,
                   wu_hbm, o_ref, stage_ref, w_bf, sems, *, layer):
    w = pl.program_id(0)
    fresh_tile, new_expert = _item_flags(tile_ref, exp_ref)
    _expert_weights(exp_ref, slot_ref, next_ref, new_expert, layer, (wg_hbm, wu_hbm),
                    stage_ref, w_bf, sems)

    @pl.when(hi_ref[w] > lo_ref[w])
    def _():
        x = x_ref[...].astype(BF16)
        hid = _silu(_dot(x, w_bf[0])) * _dot(x, w_bf[1])
        _store_rows(o_ref, hid.astype(BF16), lo_ref[w], hi_ref[w], fresh_tile)


def _moe_down_kernel(tile_ref, exp_ref, lo_ref, hi_ref, slot_ref, next_ref, h_ref, wd_hbm,
                     o_ref, stage_ref, w_bf, sems, *, layer):
    w = pl.program_id(0)
    fresh_tile, new_expert = _item_flags(tile_ref, exp_ref)
    _expert_weights(exp_ref, slot_ref, next_ref, new_expert, layer, (wd_hbm,),
                    stage_ref, w_bf, sems)

    @pl.when(hi_ref[w] > lo_ref[w])
    def _():
        y = _dot(h_ref[...], w_bf[0])
        _store_rows(o_ref, y, lo_ref[w], hi_ref[w], fresh_tile)


def _moe(items, xs, w_gate, w_up, w_down, layer, tm):
    rows = xs.shape[0]
    n_items = items[0].shape[0]
    row_tile = lambda width: pl.BlockSpec((tm, width), lambda w, tile, *_: (tile[w], 0))
    in_hbm = pl.BlockSpec(memory_space=pl.ANY)

    def weight_scratch(n_mats, shape):
        return [pltpu.VMEM((2, n_mats) + shape, F32), pltpu.VMEM((n_mats,) + shape, BF16),
                pltpu.SemaphoreType.DMA((2, n_mats))]

    hid = pl.pallas_call(
        functools.partial(_moe_up_kernel, layer=layer),
        grid_spec=pltpu.PrefetchScalarGridSpec(
            num_scalar_prefetch=len(items),
            grid=(n_items,),
            in_specs=[row_tile(D_MODEL), in_hbm, in_hbm],
            out_specs=row_tile(D_EXPERT),
            scratch_shapes=weight_scratch(2, (D_MODEL, D_EXPERT))),
        out_shape=jax.ShapeDtypeStruct((rows, D_EXPERT), BF16),
        compiler_params=_params("arbitrary"),
        name="moe_up",
    )(*items, xs, w_gate, w_up)
    return pl.pallas_call(
        functools.partial(_moe_down_kernel, layer=layer),
        grid_spec=pltpu.PrefetchScalarGridSpec(
            num_scalar_prefetch=len(items),
            grid=(n_items,),
            in_specs=[row_tile(D_EXPERT), in_hbm],
            out_specs=row_tile(D_MODEL),
            scratch_shapes=weight_scratch(1, (D_EXPERT, D_MODEL))),
        out_shape=jax.ShapeDtypeStruct((rows, D_MODEL), F32),
        compiler_params=_params("arbitrary"),
        name="moe_down",
    )(*items, hid, w_down)


def _ple_kernel(pos0_ref, pos1_ref, hbf_ref, h_ref, p_ref, wgate_ref, wple_ref,
                wcol_ref, g_ref, b_ref, ys_ref, o_ref, obf_ref, buf_ref, sems,
                wgate_bf, wple_bf, *, tm):
    base = pl.program_id(0) * tm

    @pl.when(pl.program_id(0) == 0)
    def _():
        wgate_bf[...] = wgate_ref[0].astype(BF16)
        wple_bf[...] = wple_ref[0].astype(BF16)

    def issue(r, carry):
        for k, pos_ref in enumerate((pos0_ref, pos1_ref)):
            pltpu.make_async_copy(ys_ref.at[pl.ds(pos_ref[base + r], 1), :],
                                  buf_ref.at[k, pl.ds(r, 1), :],
                                  sems.at[k]).start(priority=k)
        return carry

    lax.fori_loop(0, tm, issue, 0, unroll=8)
    gate = _sigmoid(_dot(hbf_ref[...], wgate_bf[...]))
    ple = gate * _dot(p_ref[...].astype(BF16), wple_bf[...])
    for k in range(2):
        pltpu.make_async_copy(ys_ref.at[pl.ds(0, tm), :], buf_ref.at[k], sems.at[k]).wait()
    wc = wcol_ref[...]
    moe = wc[:, 0:1] * buf_ref[0] + wc[:, 1:2] * buf_ref[1]
    out = _layer_norm(DEEPNORM_ALPHA * h_ref[...] + moe + ple, g_ref[...], b_ref[...])
    o_ref[...] = out
    obf_ref[...] = out.astype(BF16)


def _ple_combine_ln(pos0, pos1, h_bf, h, p_i, w_ple_gate, w_ple, layer, wcol, g, b, ys, tm):
    s = h.shape[0]
    kern = functools.partial(_ple_kernel, tm=tm)
    tile = lambda width: pl.BlockSpec((tm, width), lambda i, p0, p1: (i, 0))
    full = lambda shape: pl.BlockSpec(shape, lambda i, p0, p1: (0,) * len(shape))
    return pl.pallas_call(
        kern,
        grid_spec=pltpu.PrefetchScalarGridSpec(
            num_scalar_prefetch=2,
            grid=(s // tm,),
            in_specs=[tile(D_MODEL), tile(D_MODEL), tile(PLE_DIM),
                      _resident_weight((D_MODEL, D_MODEL), layer),
                      _resident_weight((PLE_DIM, D_MODEL), layer),
                      tile(LANES), full((1, D_MODEL)), full((1, D_MODEL)),
                      pl.BlockSpec(memory_space=pl.ANY)],
            out_specs=[tile(D_MODEL), tile(D_MODEL)],
            scratch_shapes=[pltpu.VMEM((2, tm, D_MODEL), F32),
                            pltpu.SemaphoreType.DMA((2,)),
                            pltpu.VMEM((D_MODEL, D_MODEL), BF16),
                            pltpu.VMEM((PLE_DIM, D_MODEL), BF16)]),
        out_shape=[jax.ShapeDtypeStruct((s, D_MODEL), F32),
                   jax.ShapeDtypeStruct((s, D_MODEL), BF16)],
        compiler_params=_params("arbitrary"),
        name="ple_combine_ln",
    )(pos0, pos1, h_bf, h, p_i, w_ple_gate, w_ple, wcol, g, b, ys)


def _row(v, width):
    v = v.astype(F32)
    return jnp.concatenate([v, jnp.zeros((width - v.shape[0],), F32)]).reshape(1, width)


def _mixer_layer(h, h_bf, prm, layer):
    s = h.shape[0]
    w_in = prm["w_in"]
    c_f = 3 * D_ATT
    c_dt = D_MAIN + XBC_TAIL
    pad = lambda cols: jnp.concatenate(
        [cols, jnp.zeros((D_MODEL, LANES - cols.shape[1]), cols.dtype)], axis=1)
    w_small = jnp.concatenate(
        [pad(w_in[layer, :, c_f:c_f + ATT_HEADS]), pad(w_in[layer, :, c_dt:]),
         pad(w_in[layer, :, D_MAIN:c_dt])], axis=1)
    b_small = jnp.concatenate([_row(prm["b_forget"][layer], LANES),
                               _row(prm["dt_bias"][layer], LANES),
                               jnp.zeros((1, LANES), F32)], axis=1)
    a_row = _row(-jnp.exp(prm["a_log"][layer].astype(F32)), LANES)
    dskip_row = jnp.repeat(prm["d_skip"][layer].astype(F32), SSD_HEAD_DIM).reshape(1, D_SSD)

    y = _in_proj(h_bf, w_in, layer, min(s, 1024), 1024)
    faug, dtv, tail = _gates(h_bf, w_small, b_small, min(s, 256))
    att = _attention(y, faug, min(s, 512), min(s, 256), 2)
    ssd = _ssd(y, tail, dtv, prm["conv_w"][layer], _row(prm["conv_b"][layer], D_CONV),
               a_row, dskip_row, _row(prm["ssd_norm_w"][layer], D_SSD))
    return _out_proj_ln(att, ssd, prm["w_out"], layer, h,
                        _row(prm["ln1_g"][layer], D_MODEL),
                        _row(prm["ln1_b"][layer], D_MODEL), min(s, 256))


def _channel_layer(h, h_bf, p_i, w_router2, bias_col, prm, layer):
    s = h.shape[0]
    tm = min(s, 256)
    idx, wcol, cnt = _router(h, w_router2, bias_col, tm)
    counts = cnt[:, 0].astype(I32)
    offsets = jnp.cumsum(counts) - counts
    pos0 = offsets[idx[0]] + idx[2]
    pos1 = offsets[idx[1]] + idx[3]
    xs = _dispatch(pos0, pos1, h, tm)
    items = _work_items(counts, tm, 2 * s // tm)
    ys = _moe(items, xs, prm["w_gate"], prm["w_up"], prm["w_down"], layer, tm)
    return _ple_combine_ln(pos0, pos1, h_bf, h, p_i, prm["w_ple_gate"], prm["w_ple"],
                           layer, wcol, _row(prm["ln2_g"][layer], D_MODEL),
                           _row(prm["ln2_b"][layer], D_MODEL), ys, tm)


def _prep_router(p):
    order = jnp.arange(N_EXPERTS).reshape(N_EXPERT_GROUPS, EXPERTS_PER_GROUP).T.reshape(-1)
    w = p["w_router"].astype(F32)[:, order]
    w = jnp.concatenate([w, jnp.zeros((D_MODEL, LANES - N_EXPERTS), F32)], axis=1)
    w_hi = w.astype(BF16)
    w_lo = (w - w_hi.astype(F32)).astype(BF16)
    bias_col = p["router_bias"].astype(F32)[order].reshape(N_EXPERTS, 1)
    return jnp.stack([w_hi, w_lo]), bias_col


def kernel(x, p, w_in, b_forget, conv_w, conv_b, dt_bias, a_log, d_skip, ssd_norm_w,
           w_out, ln1_g, ln1_b, w_router, router_bias, w_gate, w_up, w_down, w_ple,
           w_ple_gate, ln2_g, ln2_b):
    prm = dict(w_in=w_in, b_forget=b_forget, conv_w=conv_w, conv_b=conv_b,
               dt_bias=dt_bias, a_log=a_log, d_skip=d_skip, ssd_norm_w=ssd_norm_w,
               w_out=w_out, ln1_g=ln1_g, ln1_b=ln1_b, w_router=w_router,
               router_bias=router_bias, w_gate=w_gate, w_up=w_up, w_down=w_down,
               w_ple=w_ple, w_ple_gate=w_ple_gate, ln2_g=ln2_g, ln2_b=ln2_b)
    batch = x.shape[0]
    w_router2, bias_col = _prep_router(prm)
    outs = []
    for bi in range(batch):
        h = x[bi]
        h_bf = h.astype(BF16)
        for layer in range(w_in.shape[0]):
            h, h_bf = _mixer_layer(h, h_bf, prm, layer)
            h, h_bf = _channel_layer(h, h_bf, p[layer, bi], w_router2, bias_col, prm, layer)
        outs.append(h)
    return jnp.stack(outs)
```

```python
import functools
import math

import jax
import jax.numpy as jnp
from jax import lax
from jax.experimental import pallas as pl
from jax.experimental.pallas import tpu as pltpu

F32 = jnp.float32
BF16 = jnp.bfloat16
I32 = jnp.int32

D_MODEL = 2048
ATT_HEADS = 8
ATT_HEAD_DIM = 128
D_ATT = ATT_HEADS * ATT_HEAD_DIM
SSD_HEADS = 16
SSD_HEAD_DIM = 64
D_SSD = SSD_HEADS * SSD_HEAD_DIM
SSD_GROUPS = 4
HEADS_PER_GROUP = SSD_HEADS // SSD_GROUPS
SSD_STATE = 128
D_GROUP = D_SSD // SSD_GROUPS
CONV_WIDTH = 4
D_BC = SSD_GROUPS * SSD_STATE
D_CONV = D_SSD + 2 * D_BC
N_EXPERTS = 16
N_EXPERT_GROUPS = 4
EXPERTS_PER_GROUP = N_EXPERTS // N_EXPERT_GROUPS
D_EXPERT = 1024
PLE_DIM = 256
DEPTH = 2
DEEPNORM_ALPHA = (2 * DEPTH) ** 0.25
LN_EPS = 1e-5
RMS_EPS = 1e-5
LOG2_E = math.log2(math.e)

LANES = 128
SUBLANES = 8
VMEM_LIMIT_BYTES = 56 * 1024 * 1024

D_MAIN = 3 * D_ATT + D_SSD + D_CONV
COL_WIN = 3 * D_ATT
D_WIN = D_MAIN - COL_WIN
XBC_TAIL = ATT_HEADS

SSD_CHUNK = 128
CONV_HALO = SUBLANES
SUM_ROWS = 2 * SUBLANES


def _params(*sem):
    return pltpu.CompilerParams(dimension_semantics=sem,
                                vmem_limit_bytes=VMEM_LIMIT_BYTES)


def _softplus(u):
    return jnp.maximum(u, 0.0) + jnp.log1p(jnp.exp(-jnp.abs(u)))


def _sigmoid(u):
    return 1.0 / (1.0 + jnp.exp(-u))


def _silu(u):
    return u * _sigmoid(u)


def _split3(v):
    hi = v.astype(BF16)
    r1 = v - hi.astype(F32)
    mid = r1.astype(BF16)
    lo = (r1 - mid.astype(F32)).astype(BF16)
    return hi, mid, lo


def _dot(a, b):
    return jnp.dot(a, b, preferred_element_type=F32)


def _exact_left_dot(ones_bf16, v):
    hi, mid, lo = _split3(v)
    return _dot(ones_bf16, hi) + (_dot(ones_bf16, mid) + _dot(ones_bf16, lo))


def _proj_kernel(x_ref, w_ref, o_ref, w_bf, *, q_blocks):
    @pl.when(pl.program_id(1) == 0)
    def _():
        w_bf[...] = w_ref[0].astype(BF16)

    scale = jnp.where(pl.program_id(0) < q_blocks, LOG2_E * ATT_HEAD_DIM ** -0.5, 1.0)
    o_ref[...] = (_dot(x_ref[...], w_bf[...]) * scale).astype(o_ref.dtype)


def _in_proj(h_bf, w_in, layer, tm, tn):
    s = h_bf.shape[0]
    return pl.pallas_call(
        functools.partial(_proj_kernel, q_blocks=D_ATT // tn),
        grid=(D_MAIN // tn, s // tm),
        in_specs=[pl.BlockSpec((tm, D_MODEL), lambda j, i: (i, 0)),
                  pl.BlockSpec((1, D_MODEL, tn), lambda j, i: (layer, 0, j))],
        out_specs=pl.BlockSpec((tm, tn), lambda j, i: (i, j)),
        out_shape=jax.ShapeDtypeStruct((s, D_MAIN), BF16),
        scratch_shapes=[pltpu.VMEM((D_MODEL, tn), BF16)],
        compiler_params=_params("arbitrary", "arbitrary"),
        name="in_proj",
    )(h_bf, w_in)


def _gates_kernel(x_ref, w_ref, b_ref, tri_ref, sel_ref, ones_ref, faug_ref, dt_ref,
                  tail_ref, carry_ref, w_bf):
    @pl.when(pl.program_id(0) == 0)
    def _():
        carry_ref[...] = jnp.zeros_like(carry_ref)
        w_bf[...] = w_ref[...].astype(BF16)

    v = _dot(x_ref[...], w_bf[...]) + b_ref[...]
    log_f = -_softplus(-v[:, :LANES])
    dt_ref[...] = _softplus(v[:, LANES:2 * LANES])
    tail_ref[...] = v[:, 2 * LANES:]
    fcum = _exact_left_dot(tri_ref[...], log_f) + carry_ref[...]
    carry_ref[...] = fcum[-1:, :]
    hi, mid, lo = _split3(fcum * LOG2_E)
    aug = (_dot(hi, sel_ref[0]) + _dot(mid, sel_ref[1]) + _dot(lo, sel_ref[2])
           + ones_ref[...])
    faug_ref[...] = aug.astype(BF16)


def _forget_layout():
    import numpy as np
    sel = np.zeros((3, LANES, 2 * D_ATT), np.float32)
    ones = np.zeros((1, 2 * D_ATT), np.float32)
    for h in range(ATT_HEADS):
        for k in range(3):
            sel[k, h, ATT_HEAD_DIM * h + k] = 1.0
            sel[k, h, D_ATT + ATT_HEAD_DIM * h + 3 + k] = -1.0
            ones[0, ATT_HEAD_DIM * h + 3 + k] = 1.0
            ones[0, D_ATT + ATT_HEAD_DIM * h + k] = 1.0
    return jnp.asarray(sel, BF16), jnp.asarray(ones, F32)


def _gates(h_bf, w_small, b_small, tm):
    s = h_bf.shape[0]
    tri = jnp.tril(jnp.ones((tm, tm), F32)).astype(BF16)
    sel, ones = _forget_layout()
    return pl.pallas_call(
        _gates_kernel,
        grid=(s // tm,),
        in_specs=[pl.BlockSpec((tm, D_MODEL), lambda i: (i, 0)),
                  pl.BlockSpec((D_MODEL, 3 * LANES), lambda i: (0, 0)),
                  pl.BlockSpec((1, 3 * LANES), lambda i: (0, 0)),
                  pl.BlockSpec((tm, tm), lambda i: (0, 0)),
                  pl.BlockSpec((3, LANES, 2 * D_ATT), lambda i: (0, 0, 0)),
                  pl.BlockSpec((1, 2 * D_ATT), lambda i: (0, 0))],
        out_specs=[pl.BlockSpec((tm, 2 * D_ATT), lambda i: (i, 0)),
                   pl.BlockSpec((tm, LANES), lambda i: (i, 0)),
                   pl.BlockSpec((tm, LANES), lambda i: (i, 0))],
        out_shape=[jax.ShapeDtypeStruct((s, 2 * D_ATT), BF16),
                   jax.ShapeDtypeStruct((s, LANES), F32),
                   jax.ShapeDtypeStruct((s, LANES), F32)],
        scratch_shapes=[pltpu.VMEM((1, LANES), F32),
                        pltpu.VMEM((D_MODEL, 3 * LANES), BF16)],
        compiler_params=_params("arbitrary"),
        name="gates",
    )(h_bf, w_small, b_small, tri, sel, ones)


def _attn_kernel(q_ref, fq_ref, k_ref, fk_ref, v_ref, o_ref, vt_ref, sa_ref, sb_ref,
                 *, tq, tk, nh):
    assert tq == 2 * tk
    i = pl.program_id(1)
    n_chunks = vt_ref.shape[1]
    dh = ATT_HEAD_DIM
    heads = range(nh)

    @pl.when(i == 0)
    def _():
        for hd in heads:
            for c in range(n_chunks):
                blk = v_ref[c * tk:(c + 1) * tk, hd * dh:(hd + 1) * dh]
                vt_ref[hd, c, :dh, :] = blk.astype(F32).T.astype(BF16)
                vt_ref[hd, c, dh:, :] = jnp.ones((SUM_ROWS, tk), BF16)

    q_t = []
    for hd in heads:
        cols = slice(hd * dh, (hd + 1) * dh)
        q_aug = jnp.concatenate([q_ref[:, cols], fq_ref[:, cols]], axis=1)
        q_t.append(q_aug.astype(F32).T.astype(BF16))
    kv_pos = lax.broadcasted_iota(I32, (tk, tq), 0)
    q_pos = lax.broadcasted_iota(I32, (tk, tq), 1)

    def logits(hd, c):
        start = pl.multiple_of(c * tk, tk)
        cols = slice(hd * dh, (hd + 1) * dh)
        k_aug = jnp.concatenate([k_ref[pl.ds(start, tk), cols],
                                 fk_ref[pl.ds(start, tk), cols]], axis=1)
        return _dot(k_aug, q_t[hd])

    def update(st, hd, c, carry, diag_offset):
        m, acc = carry
        if diag_offset is not None:
            st = jnp.where(kv_pos + diag_offset <= q_pos, st, -jnp.inf)
        m_new = jnp.maximum(m, jnp.max(st, axis=0, keepdims=True))
        p = jnp.exp2(st - m_new)
        alpha = jnp.exp2(m - m_new)
        acc = alpha * acc + _dot(vt_ref[hd, c], p.astype(BF16))
        return m_new, acc

    first = 2 * i
    for hd in heads:
        sa_ref[hd] = logits(hd, 0)

    def pair(t, carries):
        out = []
        for hd in heads:
            sb_ref[hd] = logits(hd, 2 * t + 1)
        for hd in heads:
            out.append(update(sa_ref[hd], hd, 2 * t, carries[hd], None))
        for hd in heads:
            sa_ref[hd] = logits(hd, 2 * t + 2)
        return tuple(update(sb_ref[hd], hd, 2 * t + 1, out[hd], None) for hd in heads)

    init = (jnp.full((1, tq), -jnp.inf, F32), jnp.zeros((dh + SUM_ROWS, tq), F32))
    carries = lax.fori_loop(0, i, pair, tuple(init for _ in heads))
    for hd in heads:
        sb_ref[hd] = logits(hd, first + 1)
    for hd in heads:
        carry = update(sa_ref[hd], hd, first, carries[hd], 0)
        _, acc = update(sb_ref[hd], hd, first + 1, carry, tk)
        out = acc[:dh, :] / acc[dh:dh + 1, :]
        o_ref[:, hd * dh:(hd + 1) * dh] = out.T.astype(o_ref.dtype)


def _attention(y, faug, tq, tk, nh):
    s = y.shape[0]
    width = nh * ATT_HEAD_DIM
    groups = ATT_HEADS // nh
    kern = functools.partial(_attn_kernel, tq=tq, tk=tk, nh=nh)
    q_block = lambda g0: pl.BlockSpec((tq, width), lambda g, i: (i, g0 + g))
    kv_block = lambda g0: pl.BlockSpec((s, width), lambda g, i: (0, g0 + g))
    return pl.pallas_call(
        kern,
        grid=(groups, s // tq),
        in_specs=[q_block(0), q_block(0), kv_block(groups), kv_block(groups),
                  kv_block(2 * groups)],
        out_specs=pl.BlockSpec((tq, width), lambda g, i: (i, g)),
        out_shape=jax.ShapeDtypeStruct((s, D_ATT), BF16),
        scratch_shapes=[pltpu.VMEM((nh, s // tk, ATT_HEAD_DIM + SUM_ROWS, tk), BF16),
                        pltpu.VMEM((nh, tk, tq), F32), pltpu.VMEM((nh, tk, tq), F32)],
        compiler_params=_params("arbitrary", "arbitrary"),
        name="fox_attention",
    )(y, faug, y, faug, y)


def _expand_heads(w, n_heads):
    length = w.shape[0]
    lane = lax.broadcasted_iota(I32, (length, LANES), 1)
    blocks = []
    for m in range(n_heads // 2):
        blocks.append(jnp.where(lane < SSD_HEAD_DIM, w[:, 2 * m:2 * m + 1],
                                w[:, 2 * m + 1:2 * m + 2]))
    return jnp.concatenate(blocks, axis=1)


def _ssd_kernel(win_ref, tail_ref, dt_ref, convw_ref, convb_ref, a_ref, dskip_ref,
                normw_ref, tri_ref, o_ref, ubuf_ref, state_ref):
    length = SSD_CHUNK

    @pl.when(pl.program_id(0) == 0)
    def _():
        ubuf_ref[0:CONV_HALO, :] = jnp.zeros((CONV_HALO, D_CONV), F32)
        state_ref[...] = jnp.zeros_like(state_ref)

    win = jnp.concatenate([win_ref[...].astype(F32), tail_ref[...]], axis=1)
    win = jnp.concatenate([win[:, ATT_HEADS:], win[:, :ATT_HEADS]], axis=1)
    z = win[:, :D_SSD]

    ubuf_ref[CONV_HALO:CONV_HALO + length, :] = win[:, D_SSD:D_SSD + D_CONV]
    conv = convb_ref[...]
    for k in range(CONV_WIDTH):
        off = CONV_HALO - (CONV_WIDTH - 1) + k
        conv = conv + convw_ref[k:k + 1, :] * ubuf_ref[off:off + length, :]
    tail = ubuf_ref[length:length + CONV_HALO, :]
    ubuf_ref[0:CONV_HALO, :] = tail
    xc = _silu(conv)
    xs = xc[:, :D_SSD]

    dt = dt_ref[...]
    da = dt * a_ref[...]
    acum = _exact_left_dot(tri_ref[...], da)
    acum_t = acum.T
    a_last = acum[length - 1:length, :]
    w_off = _expand_heads(jnp.exp(acum), SSD_HEADS)
    w_state = _expand_heads(jnp.exp(a_last - acum) * dt, SSD_HEADS)
    w_dt = _expand_heads(dt, SSD_HEADS)
    chunk_decay = _expand_heads(jnp.exp(a_last), SSD_HEADS)

    x_dt = (xs * w_dt).astype(BF16)
    x_state = (xs * w_state).astype(BF16)
    row = lax.broadcasted_iota(I32, (length, length), 0)
    col = lax.broadcasted_iota(I32, (length, length), 1)
    causal = col <= row
    grp_lane = lax.broadcasted_iota(I32, (length, D_GROUP), 1) // SSD_HEAD_DIM

    y_parts = []
    for g in range(SSD_GROUPS):
        b_g = xc[:, D_SSD + g * SSD_STATE:D_SSD + (g + 1) * SSD_STATE].astype(BF16)
        c_g = xc[:, D_SSD + D_BC + g * SSD_STATE:
                 D_SSD + D_BC + (g + 1) * SSD_STATE].astype(BF16)
        cb = lax.dot_general(c_g, b_g, (((1,), (1,)), ((), ())),
                             preferred_element_type=F32)
        cols = slice(g * D_GROUP, (g + 1) * D_GROUP)
        x_dt_g = x_dt[:, cols]
        m_blocks, x_blocks = [], []
        for r in range(HEADS_PER_GROUP):
            hd = g * HEADS_PER_GROUP + r
            seg = acum[:, hd:hd + 1] - acum_t[hd:hd + 1, :]
            decay = jnp.exp(jnp.where(causal, seg, -jnp.inf))
            m_blocks.append((cb * decay).astype(BF16))
            x_blocks.append(jnp.where(grp_lane == r, x_dt_g, jnp.zeros_like(x_dt_g)))
        y_diag = _dot(jnp.concatenate(m_blocks, axis=1),
                      jnp.concatenate(x_blocks, axis=0))
        st = state_ref[g]
        y_off = _dot(c_g, st.astype(BF16)) * w_off[:, cols]
        y_parts.append(y_diag + y_off)
        new_st = lax.dot_general(b_g, x_state[:, cols], (((0,), (0,)), ((), ())),
                                 preferred_element_type=F32)
        state_ref[g] = st * chunk_decay[:, cols] + new_st

    y = jnp.concatenate(y_parts, axis=1) + dskip_ref[...] * xs
    yg = y * _silu(z)
    outs = []
    for g in range(SSD_GROUPS):
        blk = yg[:, g * D_GROUP:(g + 1) * D_GROUP]
        ms = jnp.mean(blk * blk, axis=1, keepdims=True)
        outs.append(blk * lax.rsqrt(ms + RMS_EPS))
    o_ref[...] = (jnp.concatenate(outs, axis=1) * normw_ref[...]).astype(o_ref.dtype)


def _ssd(y, tail, dtv, conv_w, conv_b, a_row, dskip_row, normw_row):
    s = y.shape[0]
    length = SSD_CHUNK
    tri = jnp.tril(jnp.ones((length, length), F32)).astype(BF16)
    full = lambda shape: pl.BlockSpec(shape, lambda c: (0,) * len(shape))
    return pl.pallas_call(
        _ssd_kernel,
        grid=(s // length,),
        in_specs=[pl.BlockSpec((length, D_WIN), lambda c: (c, COL_WIN // D_WIN)),
                  pl.BlockSpec((length, LANES), lambda c: (c, 0)),
                  pl.BlockSpec((length, LANES), lambda c: (c, 0)),
                  full((CONV_WIDTH, D_CONV)), full((1, D_CONV)), full((1, LANES)),
                  full((1, D_SSD)), full((1, D_SSD)), full((length, length))],
        out_specs=pl.BlockSpec((length, D_SSD), lambda c: (c, 0)),
        out_shape=jax.ShapeDtypeStruct((s, D_SSD), BF16),
        scratch_shapes=[pltpu.VMEM((length + CONV_HALO, D_CONV), F32),
                        pltpu.VMEM((SSD_GROUPS, SSD_STATE, D_GROUP), F32)],
        compiler_params=_params("arbitrary"),
        name="ssd",
    )(y, tail, dtv, conv_w, conv_b, a_row, dskip_row, normw_row, tri)


def _layer_norm(xf, g, b):
    mu = jnp.mean(xf, axis=1, keepdims=True)
    xc = xf - mu
    var = jnp.mean(xc * xc, axis=1, keepdims=True)
    return xc * lax.rsqrt(var + LN_EPS) * g + b


def _resident_weight(shape, layer):
    return pl.BlockSpec((1,) + shape, lambda i, *_: (layer,) + (0,) * len(shape),
                        pipeline_mode=pl.Buffered(1))


def _out_proj_kernel(att_ref, ssd_ref, w_ref, h_ref, g_ref, b_ref, wr_ref, rbias_ref,
                     triu_ref, o_ref, obf_ref, idx_ref, wcol_ref, cnt_ref, w_bf,
                     carry_ref):
    @pl.when(pl.program_id(0) == 0)
    def _():
        w_bf[...] = w_ref[0].astype(BF16)

    mix = _dot(att_ref[...], w_bf[:D_ATT, :]) + _dot(ssd_ref[...], w_bf[D_ATT:, :])
    out = _layer_norm(DEEPNORM_ALPHA * h_ref[...] + mix, g_ref[...], b_ref[...])
    o_ref[...] = out
    obf_ref[...] = out.astype(BF16)
    _route(out, wr_ref, rbias_ref, triu_ref, idx_ref, wcol_ref, cnt_ref, carry_ref)


def _out_proj_ln_route(att, ssd, w_out, layer, h, g, b, w_router2, bias_col, tm):
    s = h.shape[0]
    full = lambda shape: pl.BlockSpec(shape, lambda i: (0,) * len(shape))
    row_tile = lambda width: pl.BlockSpec((tm, width), lambda i: (i, 0))
    triu = jnp.triu(jnp.ones((tm, tm), F32)).astype(BF16)
    return pl.pallas_call(
        _out_proj_kernel,
        grid=(s // tm,),
        in_specs=[row_tile(D_ATT), row_tile(D_SSD),
                  _resident_weight((D_MODEL, D_MODEL), layer),
                  row_tile(D_MODEL), full((1, D_MODEL)), full((1, D_MODEL)),
                  full((2, D_MODEL, LANES)), full((N_EXPERTS, 1)), full((tm, tm))],
        out_specs=[row_tile(D_MODEL), row_tile(D_MODEL),
                   pl.BlockSpec((SUBLANES, tm), lambda i: (0, i)),
                   row_tile(LANES), full((N_EXPERTS, LANES))],
        out_shape=[jax.ShapeDtypeStruct((s, D_MODEL), F32),
                   jax.ShapeDtypeStruct((s, D_MODEL), BF16),
                   jax.ShapeDtypeStruct((SUBLANES, s), I32),
                   jax.ShapeDtypeStruct((s, LANES), F32),
                   jax.ShapeDtypeStruct((N_EXPERTS, LANES), F32)],
        scratch_shapes=[pltpu.VMEM((D_MODEL, D_MODEL), BF16),
                        pltpu.VMEM((N_EXPERTS, LANES), F32)],
        compiler_params=_params("arbitrary"),
        name="out_proj_ln_route",
    )(att, ssd, w_out, h, g, b, w_router2, bias_col, triu)


def _route(h, wr_ref, bias_ref, triu_ref, idx_ref, wcol_ref, cnt_ref, carry_ref):
    tm = h.shape[0]

    @pl.when(pl.program_id(0) == 0)
    def _():
        carry_ref[...] = jnp.zeros_like(carry_ref)

    h_hi = h.astype(BF16)
    h_lo = (h - h_hi.astype(F32)).astype(BF16)
    logits = _dot(h_hi, wr_ref[0]) + (_dot(h_hi, wr_ref[1]) + _dot(h_lo, wr_ref[0]))
    lt = logits.T
    scores = _sigmoid(lt[:N_EXPERTS, :])
    biased = scores + bias_ref[...]
    npg = EXPERTS_PER_GROUP
    s_r = [scores[npg * r:npg * (r + 1), :] for r in range(npg)]
    b_r = [biased[npg * r:npg * (r + 1), :] for r in range(npg)]

    gs = None
    for a in range(npg):
        for b in range(a + 1, npg):
            pair = b_r[a] + b_r[b]
            gs = pair if gs is None else jnp.maximum(gs, pair)
    best = jnp.zeros((1, tm), I32)
    top = gs[0:1, :]
    for g in range(1, N_EXPERT_GROUPS):
        better = gs[g:g + 1, :] > top
        best = jnp.where(better, g, best)
        top = jnp.where(better, gs[g:g + 1, :], top)

    def pick_group(a):
        out = a[0:1, :]
        for g in range(1, N_EXPERT_GROUPS):
            out = jnp.where(best == g, a[g:g + 1, :], out)
        return out

    sb = [pick_group(b) for b in b_r]
    ss = [pick_group(s) for s in s_r]
    first = jnp.zeros((1, tm), I32)
    fmax = sb[0]
    for r in range(1, npg):
        better = sb[r] > fmax
        first = jnp.where(better, r, first)
        fmax = jnp.where(better, sb[r], fmax)
    second = jnp.zeros((1, tm), I32)
    smax = jnp.full((1, tm), -jnp.inf, F32)
    for r in range(npg):
        cand = jnp.where(first == r, -jnp.inf, sb[r])
        better = cand > smax
        second = jnp.where(better, r, second)
        smax = jnp.where(better, cand, smax)

    def pick_expert(which):
        out = ss[0]
        for r in range(1, npg):
            out = jnp.where(which == r, ss[r], out)
        return out

    a0 = pick_expert(first)
    a1 = pick_expert(second)
    denom = a0 + a1
    e0 = best * npg + first
    e1 = best * npg + second

    erow = lax.broadcasted_iota(I32, (N_EXPERTS, tm), 0)
    hit0 = erow == e0
    hit1 = erow == e1
    onehot = jnp.where(hit0, 1.0, 0.0) + jnp.where(hit1, 1.0, 0.0)
    cum = _dot(onehot.astype(BF16), triu_ref[...])
    carry = carry_ref[...][:, 0:1]
    before = cum - onehot + carry
    rank0 = jnp.sum(jnp.where(hit0, before, 0.0), axis=0, keepdims=True)
    rank1 = jnp.sum(jnp.where(hit1, before, 0.0), axis=0, keepdims=True)
    new_carry = carry_ref[...] + cum[:, tm - 1:tm]
    carry_ref[...] = new_carry
    cnt_ref[...] = new_carry

    zeros_i = jnp.zeros((SUBLANES - 4, tm), I32)
    idx_ref[...] = jnp.concatenate(
        [e0, e1, rank0.astype(I32), rank1.astype(I32), zeros_i], axis=0)
    wrows = jnp.concatenate([a0 / denom, a1 / denom,
                             jnp.zeros((LANES - 2, tm), F32)], axis=0)
    wcol_ref[...] = wrows.T


def _plan_kernel(idx_ref, cnt_ref, pos_ref, items_ref, *, tm):
    ne = N_EXPERTS
    counts = cnt_ref[...][:, 0:1]

    def across(col):
        wide = jnp.concatenate([jnp.broadcast_to(col, (ne, LANES)),
                                jnp.zeros((LANES - ne, LANES), F32)], axis=0)
        return wide.T[:ne, :ne]

    r_id = lax.broadcasted_iota(I32, (ne, ne), 0)
    c_id = lax.broadcasted_iota(I32, (ne, ne), 1)

    def running_sum(col):
        return jnp.sum(jnp.where(c_id <= r_id, across(col), 0.0), axis=1, keepdims=True)

    ends = running_sum(counts)
    starts = ends - counts
    used = jnp.where(counts > 0, 1.0, 0.0)
    first_tile = jnp.floor(starts / tm)
    last_tile = jnp.floor((ends - 1.0) / tm)
    n_e = used * (last_tile - first_tile + 1.0)
    item_end = running_sum(n_e)
    item_start = item_end - n_e
    total = item_end[ne - 1:ne, :]
    later_used = jnp.logical_and(c_id > r_id, across(used) > 0)
    next_e = jnp.min(jnp.where(later_used, c_id.astype(F32), float(ne)), axis=1,
                     keepdims=True)
    next_e = jnp.where(next_e == ne, -1.0, next_e)
    ordinal = running_sum(used) - 1.0
    slot_e = ordinal - 2.0 * jnp.floor(ordinal / 2.0)

    w = lax.broadcasted_iota(I32, (1, LANES), 1).astype(F32)
    valid = w < total
    wc = jnp.minimum(w, total - 1.0)
    e_w = jnp.minimum(jnp.sum(jnp.where(item_end <= wc, 1.0, 0.0), axis=0, keepdims=True),
                      ne - 1.0)
    e_id = lax.broadcasted_iota(I32, (ne, LANES), 0).astype(F32)
    mine = e_id == e_w
    pick = lambda col: jnp.sum(jnp.where(mine, col, 0.0), axis=0, keepdims=True)
    tile_w = pick(first_tile) + (wc - pick(item_start))
    lo = jnp.where(valid, jnp.clip(pick(starts) - tile_w * tm, 0.0, tm), 0.0)
    hi = jnp.where(valid, jnp.clip(pick(ends) - tile_w * tm, 0.0, tm), 0.0)
    rows = [tile_w, e_w, lo, hi, pick(slot_e), pick(next_e)]
    rows += [jnp.zeros((1, LANES), F32)] * (SUBLANES - len(rows))
    items_ref[...] = jnp.concatenate(rows, axis=0).astype(I32)

    s = idx_ref.shape[1]
    tok_e = lax.broadcasted_iota(I32, (ne, s), 0)
    starts_i = starts.astype(I32)
    pos = []
    for k in range(2):
        base = jnp.sum(jnp.where(tok_e == idx_ref[k:k + 1, :], starts_i, 0), axis=0,
                       keepdims=True)
        pos.append(base + idx_ref[2 + k:3 + k, :])
    pos += [jnp.zeros((1, s), I32)] * (SUBLANES - len(pos))
    pos_ref[...] = jnp.concatenate(pos, axis=0)


def _plan(idx, cnt, tm):
    s = idx.shape[1]
    n_items = 2 * s // tm + N_EXPERTS - 1
    assert n_items <= LANES
    pos, items = pl.pallas_call(
        functools.partial(_plan_kernel, tm=tm),
        out_shape=[jax.ShapeDtypeStruct((SUBLANES, s), I32),
                   jax.ShapeDtypeStruct((SUBLANES, LANES), I32)],
        compiler_params=pltpu.CompilerParams(vmem_limit_bytes=VMEM_LIMIT_BYTES),
        name="plan",
    )(idx, cnt)
    return pos[0], pos[1], tuple(items[k, :n_items] for k in range(6))


def _dispatch_kernel(pos0_ref, pos1_ref, h_ref, xs_ref, sems, *, tm):
    base = pl.program_id(0) * tm

    def issue(r, carry):
        src = h_ref.at[pl.ds(r, 1), :]
        pltpu.make_async_copy(src, xs_ref.at[pl.ds(pos0_ref[base + r], 1), :],
                              sems.at[0]).start(priority=0)
        pltpu.make_async_copy(src, xs_ref.at[pl.ds(pos1_ref[base + r], 1), :],
                              sems.at[1]).start(priority=1)
        return carry

    lax.fori_loop(0, tm, issue, 0, unroll=8)
    for k in range(2):
        pltpu.make_async_copy(h_ref, xs_ref.at[pl.ds(0, tm), :], sems.at[k]).wait()


def _dispatch(pos0, pos1, h, tm):
    s = h.shape[0]
    kern = functools.partial(_dispatch_kernel, tm=tm)
    return pl.pallas_call(
        kern,
        grid_spec=pltpu.PrefetchScalarGridSpec(
            num_scalar_prefetch=2,
            grid=(s // tm,),
            in_specs=[pl.BlockSpec((tm, D_MODEL), lambda i, p0, p1: (i, 0))],
            out_specs=pl.BlockSpec(memory_space=pl.ANY),
            scratch_shapes=[pltpu.SemaphoreType.DMA((2,))]),
        out_shape=jax.ShapeDtypeStruct((2 * s, D_MODEL), F32),
        compiler_params=_params("arbitrary"),
        name="dispatch",
    )(pos0, pos1, h)


def _item_flags(tile_ref, exp_ref):
    w = pl.program_id(0)
    prev = jnp.maximum(w - 1, 0)
    fresh_tile = jnp.logical_or(w == 0, tile_ref[w] != tile_ref[prev])
    new_expert = jnp.logical_or(w == 0, exp_ref[w] != exp_ref[prev])
    return fresh_tile, new_expert


def _store_rows(o_ref, val, lo, hi, fresh_tile):
    row = lax.broadcasted_iota(I32, (val.shape[0], 1), 0)
    keep = jnp.logical_and(row >= lo, row < hi)

    @pl.when(fresh_tile)
    def _():
        o_ref[...] = jnp.where(keep, val, jnp.zeros_like(val))

    @pl.when(jnp.logical_not(fresh_tile))
    def _():
        o_ref[...] = jnp.where(keep, val, o_ref[...])


def _expert_weights(exp_ref, slot_ref, next_ref, new_expert, layer, w_hbm, stage_ref,
                    w_bf, sems):
    w = pl.program_id(0)
    n_mats = len(w_hbm)

    def fetch(expert, slot):
        return [pltpu.make_async_copy(w_hbm[m].at[layer, expert], stage_ref.at[slot, m],
                                      sems.at[slot, m]) for m in range(n_mats)]

    @pl.when(w == 0)
    def _():
        for copy in fetch(exp_ref[0], slot_ref[0]):
            copy.start(priority=1)

    @pl.when(new_expert)
    def _():
        slot = slot_ref[w]
        for m, copy in enumerate(fetch(exp_ref[w], slot)):
            copy.wait()
            w_bf[m] = stage_ref[slot, m].astype(BF16)

        @pl.when(next_ref[w] >= 0)
        def _():
            for copy in fetch(next_ref[w], 1 - slot):
                copy.start(priority=1)


def _moe_up_kernel(tile_ref, exp_ref, lo_ref, hi_ref, slot_ref, next_ref, x_ref, wg_hbm,
                   wu_hbm, o_ref, stage_ref, w_bf, sems, *, layer):
    w = pl.program_id(0)
    fresh_tile, new_expert = _item_flags(tile_ref, exp_ref)
    _expert_weights(exp_ref, slot_ref, next_ref, new_expert, layer, (wg_hbm, wu_hbm),
                    stage_ref, w_bf, sems)

    @pl.when(hi_ref[w] > lo_ref[w])
    def _():
        x = x_ref[...].astype(BF16)
        hid = _silu(_dot(x, w_bf[0])) * _dot(x, w_bf[1])
        _store_rows(o_ref, hid.astype(BF16), lo_ref[w], hi_ref[w], fresh_tile)


def _moe_down_kernel(tile_ref, exp_ref, lo_ref, hi_ref, slot_ref, next_ref, h_ref, wd_hbm,
                     o_ref, stage_ref, w_bf, sems, *, layer):
    w = pl.program_id(0)
    fresh_tile, new_expert = _item_flags(tile_ref, exp_ref)
    _expert_weights(exp_ref, slot_ref, next_ref, new_expert, layer, (wd_hbm,),
                    stage_ref, w_bf, sems)

    @pl.when(hi_ref[w] > lo_ref[w])
    def _():
        y = _dot(h_ref[...], w_bf[0])
        _store_rows(o_ref, y, lo_ref[w], hi_ref[w], fresh_tile)


def _moe(items, xs, w_gate, w_up, w_down, layer, tm):
    rows = xs.shape[0]
    n_items = items[0].shape[0]
    row_tile = lambda width: pl.BlockSpec((tm, width), lambda w, tile, *_: (tile[w], 0))
    in_hbm = pl.BlockSpec(memory_space=pl.ANY)

    def weight_scratch(n_mats, shape):
        return [pltpu.VMEM((2, n_mats) + shape, F32), pltpu.VMEM((n_mats,) + shape, BF16),
                pltpu.SemaphoreType.DMA((2, n_mats))]

    hid = pl.pallas_call(
        functools.partial(_moe_up_kernel, layer=layer),
        grid_spec=pltpu.PrefetchScalarGridSpec(
            num_scalar_prefetch=len(items),
            grid=(n_items,),
            in_specs=[row_tile(D_MODEL), in_hbm, in_hbm],
            out_specs=row_tile(D_EXPERT),
            scratch_shapes=weight_scratch(2, (D_MODEL, D_EXPERT))),
        out_shape=jax.ShapeDtypeStruct((rows, D_EXPERT), BF16),
        compiler_params=_params("arbitrary"),
        name="moe_up",
    )(*items, xs, w_gate, w_up)
    return pl.pallas_call(
        functools.partial(_moe_down_kernel, layer=layer),
        grid_spec=pltpu.PrefetchScalarGridSpec(
            num_scalar_prefetch=len(items),
            grid=(n_items,),
            in_specs=[row_tile(D_EXPERT), in_hbm],
            out_specs=row_tile(D_MODEL),
            scratch_shapes=weight_scratch(1, (D_EXPERT, D_MODEL))),
        out_shape=jax.ShapeDtypeStruct((rows, D_MODEL), F32),
        compiler_params=_params("arbitrary"),
        name="moe_down",
    )(*items, hid, w_down)


def _ple_kernel(pos0_ref, pos1_ref, hbf_ref, h_ref, p_ref, wgate_ref, wple_ref,
                wcol_ref, g_ref, b_ref, ys_ref, o_ref, obf_ref, buf_ref, sems,
                wgate_bf, wple_bf, *, tm):
    base = pl.program_id(0) * tm

    @pl.when(pl.program_id(0) == 0)
    def _():
        wgate_bf[...] = wgate_ref[0].astype(BF16)
        wple_bf[...] = wple_ref[0].astype(BF16)

    def issue(r, carry):
        for k, pos_ref in enumerate((pos0_ref, pos1_ref)):
            pltpu.make_async_copy(ys_ref.at[pl.ds(pos_ref[base + r], 1), :],
                                  buf_ref.at[k, pl.ds(r, 1), :],
                                  sems.at[k]).start(priority=k)
        return carry

    lax.fori_loop(0, tm, issue, 0, unroll=8)
    gate = _sigmoid(_dot(hbf_ref[...], wgate_bf[...]))
    ple = gate * _dot(p_ref[0, 0].astype(BF16), wple_bf[...])
    for k in range(2):
        pltpu.make_async_copy(ys_ref.at[pl.ds(0, tm), :], buf_ref.at[k], sems.at[k]).wait()
    wc = wcol_ref[...]
    moe = wc[:, 0:1] * buf_ref[0] + wc[:, 1:2] * buf_ref[1]
    out = _layer_norm(DEEPNORM_ALPHA * h_ref[...] + moe + ple, g_ref[...], b_ref[...])
    o_ref[...] = out
    obf_ref[...] = out.astype(BF16)


def _ple_combine_ln(pos0, pos1, h_bf, h, p_all, bi, w_ple_gate, w_ple, layer, wcol, g, b,
                    ys, tm):
    s = h.shape[0]
    kern = functools.partial(_ple_kernel, tm=tm)
    tile = lambda width: pl.BlockSpec((tm, width), lambda i, p0, p1: (i, 0))
    full = lambda shape: pl.BlockSpec(shape, lambda i, p0, p1: (0,) * len(shape))
    p_tile = pl.BlockSpec((1, 1, tm, PLE_DIM), lambda i, p0, p1: (layer, bi, i, 0))
    return pl.pallas_call(
        kern,
        grid_spec=pltpu.PrefetchScalarGridSpec(
            num_scalar_prefetch=2,
            grid=(s // tm,),
            in_specs=[tile(D_MODEL), tile(D_MODEL), p_tile,
                      _resident_weight((D_MODEL, D_MODEL), layer),
                      _resident_weight((PLE_DIM, D_MODEL), layer),
                      tile(LANES), full((1, D_MODEL)), full((1, D_MODEL)),
                      pl.BlockSpec(memory_space=pl.ANY)],
            out_specs=[tile(D_MODEL), tile(D_MODEL)],
            scratch_shapes=[pltpu.VMEM((2, tm, D_MODEL), F32),
                            pltpu.SemaphoreType.DMA((2,)),
                            pltpu.VMEM((D_MODEL, D_MODEL), BF16),
                            pltpu.VMEM((PLE_DIM, D_MODEL), BF16)]),
        out_shape=[jax.ShapeDtypeStruct((s, D_MODEL), F32),
                   jax.ShapeDtypeStruct((s, D_MODEL), BF16)],
        compiler_params=_params("arbitrary"),
        name="ple_combine_ln",
    )(pos0, pos1, h_bf, h, p_all, w_ple_gate, w_ple, wcol, g, b, ys)


def _row(v, width):
    v = v.astype(F32)
    return jnp.concatenate([v, jnp.zeros((width - v.shape[0],), F32)]).reshape(1, width)


def _mixer_layer(h, h_bf, prm, layer, w_router2, bias_col):
    s = h.shape[0]
    w_in = prm["w_in"]
    c_f = 3 * D_ATT
    c_dt = D_MAIN + XBC_TAIL
    pad = lambda cols: jnp.concatenate(
        [cols, jnp.zeros((D_MODEL, LANES - cols.shape[1]), cols.dtype)], axis=1)
    w_small = jnp.concatenate(
        [pad(w_in[layer, :, c_f:c_f + ATT_HEADS]), pad(w_in[layer, :, c_dt:]),
         pad(w_in[layer, :, D_MAIN:c_dt])], axis=1)
    b_small = jnp.concatenate([_row(prm["b_forget"][layer], LANES),
                               _row(prm["dt_bias"][layer], LANES),
                               jnp.zeros((1, LANES), F32)], axis=1)
    a_row = _row(-jnp.exp(prm["a_log"][layer].astype(F32)), LANES)
    dskip_row = jnp.repeat(prm["d_skip"][layer].astype(F32), SSD_HEAD_DIM).reshape(1, D_SSD)

    y = _in_proj(h_bf, w_in, layer, min(s, 1024), 1024)
    faug, dtv, tail = _gates(h_bf, w_small, b_small, min(s, 512))
    att = _attention(y, faug, min(s, 512), min(s, 256), 2)
    ssd = _ssd(y, tail, dtv, prm["conv_w"][layer], _row(prm["conv_b"][layer], D_CONV),
               a_row, dskip_row, _row(prm["ssd_norm_w"][layer], D_SSD))
    return _out_proj_ln_route(att, ssd, prm["w_out"], layer, h,
                              _row(prm["ln1_g"][layer], D_MODEL),
                              _row(prm["ln1_b"][layer], D_MODEL),
                              w_router2, bias_col, min(s, 256))


def _channel_layer(h, h_bf, routing, p_all, layer, bi, prm):
    s = h.shape[0]
    tm = min(s, 256)
    idx, wcol, cnt = routing
    pos0, pos1, items = _plan(idx, cnt, tm)
    xs = _dispatch(pos0, pos1, h, tm)
    ys = _moe(items, xs, prm["w_gate"], prm["w_up"], prm["w_down"], layer, tm)
    return _ple_combine_ln(pos0, pos1, h_bf, h, p_all, bi, prm["w_ple_gate"], prm["w_ple"],
                           layer, wcol, _row(prm["ln2_g"][layer], D_MODEL),
                           _row(prm["ln2_b"][layer], D_MODEL), ys, tm)


def _prep_router(p):
    order = jnp.arange(N_EXPERTS).reshape(N_EXPERT_GROUPS, EXPERTS_PER_GROUP).T.reshape(-1)
    w = p["w_router"].astype(F32)[:, order]
    w = jnp.concatenate([w, jnp.zeros((D_MODEL, LANES - N_EXPERTS), F32)], axis=1)
    w_hi = w.astype(BF16)
    w_lo = (w - w_hi.astype(F32)).astype(BF16)
    bias_col = p["router_bias"].astype(F32)[order].reshape(N_EXPERTS, 1)
    return jnp.stack([w_hi, w_lo]), bias_col


def kernel(x, p, w_in, b_forget, conv_w, conv_b, dt_bias, a_log, d_skip, ssd_norm_w,
           w_out, ln1_g, ln1_b, w_router, router_bias, w_gate, w_up, w_down, w_ple,
           w_ple_gate, ln2_g, ln2_b):
    prm = dict(w_in=w_in, b_forget=b_forget, conv_w=conv_w, conv_b=conv_b,
               dt_bias=dt_bias, a_log=a_log, d_skip=d_skip, ssd_norm_w=ssd_norm_w,
               w_out=w_out, ln1_g=ln1_g, ln1_b=ln1_b, w_router=w_router,
               router_bias=router_bias, w_gate=w_gate, w_up=w_up, w_down=w_down,
               w_ple=w_ple, w_ple_gate=w_ple_gate, ln2_g=ln2_g, ln2_b=ln2_b)
    batch = x.shape[0]
    w_router2, bias_col = _prep_router(prm)
    outs = []
    for bi in range(batch):
        h = x[bi]
        h_bf = h.astype(BF16)
        for layer in range(w_in.shape[0]):
            h, h_bf, *routing = _mixer_layer(h, h_bf, prm, layer, w_router2, bias_col)
            h, h_bf = _channel_layer(h, h_bf, routing, p, layer, bi, prm)
        outs.append(h)
    return jnp.stack(outs)
```

```python
import functools
import math

import jax
import jax.numpy as jnp
from jax import lax
from jax.experimental import pallas as pl
from jax.experimental.pallas import tpu as pltpu

F32 = jnp.float32
BF16 = jnp.bfloat16
I32 = jnp.int32

D_MODEL = 2048
ATT_HEADS = 8
ATT_HEAD_DIM = 128
D_ATT = ATT_HEADS * ATT_HEAD_DIM
SSD_HEADS = 16
SSD_HEAD_DIM = 64
D_SSD = SSD_HEADS * SSD_HEAD_DIM
SSD_GROUPS = 4
HEADS_PER_GROUP = SSD_HEADS // SSD_GROUPS
SSD_STATE = 128
D_GROUP = D_SSD // SSD_GROUPS
CONV_WIDTH = 4
D_BC = SSD_GROUPS * SSD_STATE
D_CONV = D_SSD + 2 * D_BC
N_EXPERTS = 16
N_EXPERT_GROUPS = 4
EXPERTS_PER_GROUP = N_EXPERTS // N_EXPERT_GROUPS
D_EXPERT = 1024
PLE_DIM = 256
DEPTH = 2
DEEPNORM_ALPHA = (2 * DEPTH) ** 0.25
LN_EPS = 1e-5
RMS_EPS = 1e-5
LOG2_E = math.log2(math.e)

LANES = 128
SUBLANES = 8
VMEM_LIMIT_BYTES = 56 * 1024 * 1024

D_MAIN = 3 * D_ATT + D_SSD + D_CONV
COL_WIN = 3 * D_ATT
D_WIN = D_MAIN - COL_WIN
XBC_TAIL = ATT_HEADS

SSD_CHUNK = 128
CONV_HALO = SUBLANES
SUM_ROWS = 2 * SUBLANES


def _params(*sem):
    return pltpu.CompilerParams(dimension_semantics=sem,
                                vmem_limit_bytes=VMEM_LIMIT_BYTES)


def _softplus(u):
    return jnp.maximum(u, 0.0) + jnp.log1p(jnp.exp(-jnp.abs(u)))


def _sigmoid(u):
    return 1.0 / (1.0 + jnp.exp(-u))


def _silu(u):
    return u * _sigmoid(u)


def _split3(v):
    hi = v.astype(BF16)
    r1 = v - hi.astype(F32)
    mid = r1.astype(BF16)
    lo = (r1 - mid.astype(F32)).astype(BF16)
    return hi, mid, lo


def _dot(a, b):
    return jnp.dot(a, b, preferred_element_type=F32)


def _exact_left_dot(ones_bf16, v):
    hi, mid, lo = _split3(v)
    return _dot(ones_bf16, hi) + (_dot(ones_bf16, mid) + _dot(ones_bf16, lo))


def _proj_kernel(x_ref, w_ref, o_ref, w_bf, *, q_blocks):
    @pl.when(pl.program_id(1) == 0)
    def _():
        w_bf[...] = w_ref[0].T.astype(BF16)

    scale = jnp.where(pl.program_id(0) < q_blocks, LOG2_E * ATT_HEAD_DIM ** -0.5, 1.0)
    o_ref[...] = (_dot(x_ref[...], w_bf[...]) * scale).astype(o_ref.dtype)


def _in_proj(h_bf, w_in_t, layer, tm, tn):
    s = h_bf.shape[0]
    return pl.pallas_call(
        functools.partial(_proj_kernel, q_blocks=D_ATT // tn),
        grid=(D_MAIN // tn, s // tm),
        in_specs=[pl.BlockSpec((tm, D_MODEL), lambda j, i: (i, 0)),
                  pl.BlockSpec((1, tn, D_MODEL), lambda j, i: (layer, j, 0))],
        out_specs=pl.BlockSpec((tm, tn), lambda j, i: (i, j)),
        out_shape=jax.ShapeDtypeStruct((s, D_MAIN), BF16),
        scratch_shapes=[pltpu.VMEM((D_MODEL, tn), BF16)],
        compiler_params=_params("arbitrary", "arbitrary"),
        name="in_proj",
    )(h_bf, w_in_t)


def _gates_kernel(x_ref, w_ref, b_ref, tri_ref, sel_ref, ones_ref, faug_ref, dt_ref,
                  tail_ref, carry_ref, w_bf):
    @pl.when(pl.program_id(0) == 0)
    def _():
        carry_ref[...] = jnp.zeros_like(carry_ref)
        w_bf[...] = w_ref[...].astype(BF16)

    v = _dot(x_ref[...], w_bf[...]) + b_ref[...]
    log_f = -_softplus(-v[:, :LANES])
    dt_ref[...] = _softplus(v[:, LANES:2 * LANES])
    tail_ref[...] = v[:, 2 * LANES:]
    fcum = _exact_left_dot(tri_ref[...], log_f) + carry_ref[...]
    carry_ref[...] = fcum[-1:, :]
    hi, mid, lo = _split3(fcum * LOG2_E)
    aug = (_dot(hi, sel_ref[0]) + _dot(mid, sel_ref[1]) + _dot(lo, sel_ref[2])
           + ones_ref[...])
    faug_ref[...] = aug.astype(BF16)


def _forget_layout():
    import numpy as np
    sel = np.zeros((3, LANES, 2 * D_ATT), np.float32)
    ones = np.zeros((1, 2 * D_ATT), np.float32)
    for h in range(ATT_HEADS):
        for k in range(3):
            sel[k, h, ATT_HEAD_DIM * h + k] = 1.0
            sel[k, h, D_ATT + ATT_HEAD_DIM * h + 3 + k] = -1.0
            ones[0, ATT_HEAD_DIM * h + 3 + k] = 1.0
            ones[0, D_ATT + ATT_HEAD_DIM * h + k] = 1.0
    return jnp.asarray(sel, BF16), jnp.asarray(ones, F32)


def _gates(h_bf, w_small, b_small, tm):
    s = h_bf.shape[0]
    tri = jnp.tril(jnp.ones((tm, tm), F32)).astype(BF16)
    sel, ones = _forget_layout()
    return pl.pallas_call(
        _gates_kernel,
        grid=(s // tm,),
        in_specs=[pl.BlockSpec((tm, D_MODEL), lambda i: (i, 0)),
                  pl.BlockSpec((D_MODEL, 3 * LANES), lambda i: (0, 0)),
                  pl.BlockSpec((1, 3 * LANES), lambda i: (0, 0)),
                  pl.BlockSpec((tm, tm), lambda i: (0, 0)),
                  pl.BlockSpec((3, LANES, 2 * D_ATT), lambda i: (0, 0, 0)),
                  pl.BlockSpec((1, 2 * D_ATT), lambda i: (0, 0))],
        out_specs=[pl.BlockSpec((tm, 2 * D_ATT), lambda i: (i, 0)),
                   pl.BlockSpec((tm, LANES), lambda i: (i, 0)),
                   pl.BlockSpec((tm, LANES), lambda i: (i, 0))],
        out_shape=[jax.ShapeDtypeStruct((s, 2 * D_ATT), BF16),
                   jax.ShapeDtypeStruct((s, LANES), F32),
                   jax.ShapeDtypeStruct((s, LANES), F32)],
        scratch_shapes=[pltpu.VMEM((1, LANES), F32),
                        pltpu.VMEM((D_MODEL, 3 * LANES), BF16)],
        compiler_params=_params("arbitrary"),
        name="gates",
    )(h_bf, w_small, b_small, tri, sel, ones)


def _attn_kernel(q_ref, fq_ref, k_ref, fk_ref, v_ref, o_ref, vt_ref, sa_ref, sb_ref,
                 *, tq, tk, nh):
    assert tq == 2 * tk
    i = pl.program_id(1)
    n_chunks = vt_ref.shape[1]
    dh = ATT_HEAD_DIM
    heads = range(nh)

    @pl.when(i == 0)
    def _():
        for hd in heads:
            for c in range(n_chunks):
                blk = v_ref[c * tk:(c + 1) * tk, hd * dh:(hd + 1) * dh]
                vt_ref[hd, c, :dh, :] = blk.astype(F32).T.astype(BF16)
                vt_ref[hd, c, dh:, :] = jnp.ones((SUM_ROWS, tk), BF16)

    q_t = []
    for hd in heads:
        cols = slice(hd * dh, (hd + 1) * dh)
        q_aug = jnp.concatenate([q_ref[:, cols], fq_ref[:, cols]], axis=1)
        q_t.append(q_aug.astype(F32).T.astype(BF16))
    kv_pos = lax.broadcasted_iota(I32, (tk, tq), 0)
    q_pos = lax.broadcasted_iota(I32, (tk, tq), 1)

    def logits(hd, c):
        start = pl.multiple_of(c * tk, tk)
        cols = slice(hd * dh, (hd + 1) * dh)
        k_aug = jnp.concatenate([k_ref[pl.ds(start, tk), cols],
                                 fk_ref[pl.ds(start, tk), cols]], axis=1)
        return _dot(k_aug, q_t[hd])

    def update(st, hd, c, carry, diag_offset):
        m, acc = carry
        if diag_offset is not None:
            st = jnp.where(kv_pos + diag_offset <= q_pos, st, -jnp.inf)
        m_new = jnp.maximum(m, jnp.max(st, axis=0, keepdims=True))
        p = jnp.exp2(st - m_new)
        alpha = jnp.exp2(m - m_new)
        acc = alpha * acc + _dot(vt_ref[hd, c], p.astype(BF16))
        return m_new, acc

    first = 2 * i
    for hd in heads:
        sa_ref[hd] = logits(hd, 0)

    def pair(t, carries):
        out = []
        for hd in heads:
            sb_ref[hd] = logits(hd, 2 * t + 1)
        for hd in heads:
            out.append(update(sa_ref[hd], hd, 2 * t, carries[hd], None))
        for hd in heads:
            sa_ref[hd] = logits(hd, 2 * t + 2)
        return tuple(update(sb_ref[hd], hd, 2 * t + 1, out[hd], None) for hd in heads)

    init = (jnp.full((1, tq), -jnp.inf, F32), jnp.zeros((dh + SUM_ROWS, tq), F32))
    carries = lax.fori_loop(0, i, pair, tuple(init for _ in heads))
    for hd in heads:
        sb_ref[hd] = logits(hd, first + 1)
    for hd in heads:
        carry = update(sa_ref[hd], hd, first, carries[hd], 0)
        _, acc = update(sb_ref[hd], hd, first + 1, carry, tk)
        out = acc[:dh, :] / acc[dh:dh + 1, :]
        o_ref[:, hd * dh:(hd + 1) * dh] = out.T.astype(o_ref.dtype)


def _attention(y, faug, tq, tk, nh):
    s = y.shape[0]
    width = nh * ATT_HEAD_DIM
    groups = ATT_HEADS // nh
    kern = functools.partial(_attn_kernel, tq=tq, tk=tk, nh=nh)
    q_block = lambda g0: pl.BlockSpec((tq, width), lambda g, i: (i, g0 + g))
    kv_block = lambda g0: pl.BlockSpec((s, width), lambda g, i: (0, g0 + g))
    return pl.pallas_call(
        kern,
        grid=(groups, s // tq),
        in_specs=[q_block(0), q_block(0), kv_block(groups), kv_block(groups),
                  kv_block(2 * groups)],
        out_specs=pl.BlockSpec((tq, width), lambda g, i: (i, g)),
        out_shape=jax.ShapeDtypeStruct((s, D_ATT), BF16),
        scratch_shapes=[pltpu.VMEM((nh, s // tk, ATT_HEAD_DIM + SUM_ROWS, tk), BF16),
                        pltpu.VMEM((nh, tk, tq), F32), pltpu.VMEM((nh, tk, tq), F32)],
        compiler_params=_params("arbitrary", "arbitrary"),
        name="fox_attention",
    )(y, faug, y, faug, y)


def _expand_heads(w, n_heads):
    length = w.shape[0]
    lane = lax.broadcasted_iota(I32, (length, LANES), 1)
    blocks = []
    for m in range(n_heads // 2):
        blocks.append(jnp.where(lane < SSD_HEAD_DIM, w[:, 2 * m:2 * m + 1],
                                w[:, 2 * m + 1:2 * m + 2]))
    return jnp.concatenate(blocks, axis=1)


def _ssd_kernel(win_ref, tail_ref, dt_ref, convw_ref, convb_ref, a_ref, dskip_ref,
                normw_ref, tri_ref, o_ref, ubuf_ref, state_ref):
    length = SSD_CHUNK

    @pl.when(pl.program_id(0) == 0)
    def _():
        ubuf_ref[0:CONV_HALO, :] = jnp.zeros((CONV_HALO, D_CONV), F32)
        state_ref[...] = jnp.zeros_like(state_ref)

    win = jnp.concatenate([win_ref[...].astype(F32), tail_ref[...]], axis=1)
    win = jnp.concatenate([win[:, ATT_HEADS:], win[:, :ATT_HEADS]], axis=1)
    z = win[:, :D_SSD]

    ubuf_ref[CONV_HALO:CONV_HALO + length, :] = win[:, D_SSD:D_SSD + D_CONV]
    conv = convb_ref[...]
    for k in range(CONV_WIDTH):
        off = CONV_HALO - (CONV_WIDTH - 1) + k
        conv = conv + convw_ref[k:k + 1, :] * ubuf_ref[off:off + length, :]
    tail = ubuf_ref[length:length + CONV_HALO, :]
    ubuf_ref[0:CONV_HALO, :] = tail
    xc = _silu(conv)
    xs = xc[:, :D_SSD]

    dt = dt_ref[...]
    da = dt * a_ref[...]
    acum = _exact_left_dot(tri_ref[...], da)
    acum_t = acum.T
    a_last = acum[length - 1:length, :]
    w_off = _expand_heads(jnp.exp(acum), SSD_HEADS)
    w_state = _expand_heads(jnp.exp(a_last - acum) * dt, SSD_HEADS)
    w_dt = _expand_heads(dt, SSD_HEADS)
    chunk_decay = _expand_heads(jnp.exp(a_last), SSD_HEADS)

    x_dt = (xs * w_dt).astype(BF16)
    x_state = (xs * w_state).astype(BF16)
    row = lax.broadcasted_iota(I32, (length, length), 0)
    col = lax.broadcasted_iota(I32, (length, length), 1)
    causal = col <= row
    grp_lane = lax.broadcasted_iota(I32, (length, D_GROUP), 1) // SSD_HEAD_DIM

    y_parts = []
    for g in range(SSD_GROUPS):
        b_g = xc[:, D_SSD + g * SSD_STATE:D_SSD + (g + 1) * SSD_STATE].astype(BF16)
        c_g = xc[:, D_SSD + D_BC + g * SSD_STATE:
                 D_SSD + D_BC + (g + 1) * SSD_STATE].astype(BF16)
        cb = lax.dot_general(c_g, b_g, (((1,), (1,)), ((), ())),
                             preferred_element_type=F32)
        cols = slice(g * D_GROUP, (g + 1) * D_GROUP)
        x_dt_g = x_dt[:, cols]
        m_blocks, x_blocks = [], []
        for r in range(HEADS_PER_GROUP):
            hd = g * HEADS_PER_GROUP + r
            seg = acum[:, hd:hd + 1] - acum_t[hd:hd + 1, :]
            decay = jnp.exp(jnp.where(causal, seg, -jnp.inf))
            m_blocks.append((cb * decay).astype(BF16))
            x_blocks.append(jnp.where(grp_lane == r, x_dt_g, jnp.zeros_like(x_dt_g)))
        y_diag = _dot(jnp.concatenate(m_blocks, axis=1),
                      jnp.concatenate(x_blocks, axis=0))
        st = state_ref[g]
        y_off = _dot(c_g, st.astype(BF16)) * w_off[:, cols]
        y_parts.append(y_diag + y_off)
        new_st = lax.dot_general(b_g, x_state[:, cols], (((0,), (0,)), ((), ())),
                                 preferred_element_type=F32)
        state_ref[g] = st * chunk_decay[:, cols] + new_st

    y = jnp.concatenate(y_parts, axis=1) + dskip_ref[...] * xs
    yg = y * _silu(z)
    outs = []
    for g in range(SSD_GROUPS):
        blk = yg[:, g * D_GROUP:(g + 1) * D_GROUP]
        ms = jnp.mean(blk * blk, axis=1, keepdims=True)
        outs.append(blk * lax.rsqrt(ms + RMS_EPS))
    o_ref[...] = (jnp.concatenate(outs, axis=1) * normw_ref[...]).astype(o_ref.dtype)


def _ssd(y, tail, dtv, conv_w, conv_b, a_row, dskip_row, normw_row):
    s = y.shape[0]
    length = SSD_CHUNK
    tri = jnp.tril(jnp.ones((length, length), F32)).astype(BF16)
    full = lambda shape: pl.BlockSpec(shape, lambda c: (0,) * len(shape))
    return pl.pallas_call(
        _ssd_kernel,
        grid=(s // length,),
        in_specs=[pl.BlockSpec((length, D_WIN), lambda c: (c, COL_WIN // D_WIN)),
                  pl.BlockSpec((length, LANES), lambda c: (c, 0)),
                  pl.BlockSpec((length, LANES), lambda c: (c, 0)),
                  full((CONV_WIDTH, D_CONV)), full((1, D_CONV)), full((1, LANES)),
                  full((1, D_SSD)), full((1, D_SSD)), full((length, length))],
        out_specs=pl.BlockSpec((length, D_SSD), lambda c: (c, 0)),
        out_shape=jax.ShapeDtypeStruct((s, D_SSD), BF16),
        scratch_shapes=[pltpu.VMEM((length + CONV_HALO, D_CONV), F32),
                        pltpu.VMEM((SSD_GROUPS, SSD_STATE, D_GROUP), F32)],
        compiler_params=_params("arbitrary"),
        name="ssd",
    )(y, tail, dtv, conv_w, conv_b, a_row, dskip_row, normw_row, tri)


def _layer_norm(xf, g, b):
    mu = jnp.mean(xf, axis=1, keepdims=True)
    xc = xf - mu
    var = jnp.mean(xc * xc, axis=1, keepdims=True)
    return xc * lax.rsqrt(var + LN_EPS) * g + b


def _resident_weight(shape, layer):
    return pl.BlockSpec((1,) + shape, lambda i, *_: (layer,) + (0,) * len(shape),
                        pipeline_mode=pl.Buffered(1))


def _out_proj_kernel(att_ref, ssd_ref, w_ref, h_ref, g_ref, b_ref, wr_ref, rbias_ref,
                     triu_ref, o_ref, obf_ref, idx_ref, wcol_ref, cnt_ref, w_bf,
                     carry_ref):
    @pl.when(pl.program_id(0) == 0)
    def _():
        w_bf[...] = w_ref[0].astype(BF16)

    mix = _dot(att_ref[...], w_bf[:D_ATT, :]) + _dot(ssd_ref[...], w_bf[D_ATT:, :])
    out = _layer_norm(DEEPNORM_ALPHA * h_ref[...] + mix, g_ref[...], b_ref[...])
    o_ref[...] = out
    obf_ref[...] = out.astype(BF16)
    _route(out, wr_ref, rbias_ref, triu_ref, idx_ref, wcol_ref, cnt_ref, carry_ref)


def _out_proj_ln_route(att, ssd, w_out, layer, h, g, b, w_router2, bias_col, tm):
    s = h.shape[0]
    full = lambda shape: pl.BlockSpec(shape, lambda i: (0,) * len(shape))
    row_tile = lambda width: pl.BlockSpec((tm, width), lambda i: (i, 0))
    triu = jnp.triu(jnp.ones((tm, tm), F32)).astype(BF16)
    return pl.pallas_call(
        _out_proj_kernel,
        grid=(s // tm,),
        in_specs=[row_tile(D_ATT), row_tile(D_SSD),
                  _resident_weight((D_MODEL, D_MODEL), layer),
                  row_tile(D_MODEL), full((1, D_MODEL)), full((1, D_MODEL)),
                  full((2, D_MODEL, LANES)), full((N_EXPERTS, 1)), full((tm, tm))],
        out_specs=[row_tile(D_MODEL), row_tile(D_MODEL),
                   pl.BlockSpec((SUBLANES, tm), lambda i: (0, i)),
                   row_tile(LANES), full((N_EXPERTS, LANES))],
        out_shape=[jax.ShapeDtypeStruct((s, D_MODEL), F32),
                   jax.ShapeDtypeStruct((s, D_MODEL), BF16),
                   jax.ShapeDtypeStruct((SUBLANES, s), I32),
                   jax.ShapeDtypeStruct((s, LANES), F32),
                   jax.ShapeDtypeStruct((N_EXPERTS, LANES), F32)],
        scratch_shapes=[pltpu.VMEM((D_MODEL, D_MODEL), BF16),
                        pltpu.VMEM((N_EXPERTS, LANES), F32)],
        compiler_params=_params("arbitrary"),
        name="out_proj_ln_route",
    )(att, ssd, w_out, h, g, b, w_router2, bias_col, triu)


def _route(h, wr_ref, bias_ref, triu_ref, idx_ref, wcol_ref, cnt_ref, carry_ref):
    tm = h.shape[0]

    @pl.when(pl.program_id(0) == 0)
    def _():
        carry_ref[...] = jnp.zeros_like(carry_ref)

    h_hi = h.astype(BF16)
    h_lo = (h - h_hi.astype(F32)).astype(BF16)
    logits = _dot(h_hi, wr_ref[0]) + (_dot(h_hi, wr_ref[1]) + _dot(h_lo, wr_ref[0]))
    lt = logits.T
    scores = _sigmoid(lt[:N_EXPERTS, :])
    biased = scores + bias_ref[...]
    npg = EXPERTS_PER_GROUP
    s_r = [scores[npg * r:npg * (r + 1), :] for r in range(npg)]
    b_r = [biased[npg * r:npg * (r + 1), :] for r in range(npg)]

    gs = None
    for a in range(npg):
        for b in range(a + 1, npg):
            pair = b_r[a] + b_r[b]
            gs = pair if gs is None else jnp.maximum(gs, pair)
    best = jnp.zeros((1, tm), I32)
    top = gs[0:1, :]
    for g in range(1, N_EXPERT_GROUPS):
        better = gs[g:g + 1, :] > top
        best = jnp.where(better, g, best)
        top = jnp.where(better, gs[g:g + 1, :], top)

    def pick_group(a):
        out = a[0:1, :]
        for g in range(1, N_EXPERT_GROUPS):
            out = jnp.where(best == g, a[g:g + 1, :], out)
        return out

    sb = [pick_group(b) for b in b_r]
    ss = [pick_group(s) for s in s_r]
    first = jnp.zeros((1, tm), I32)
    fmax = sb[0]
    for r in range(1, npg):
        better = sb[r] > fmax
        first = jnp.where(better, r, first)
        fmax = jnp.where(better, sb[r], fmax)
    second = jnp.zeros((1, tm), I32)
    smax = jnp.full((1, tm), -jnp.inf, F32)
    for r in range(npg):
        cand = jnp.where(first == r, -jnp.inf, sb[r])
        better = cand > smax
        second = jnp.where(better, r, second)
        smax = jnp.where(better, cand, smax)

    def pick_expert(which):
        out = ss[0]
        for r in range(1, npg):
            out = jnp.where(which == r, ss[r], out)
        return out

    a0 = pick_expert(first)
    a1 = pick_expert(second)
    denom = a0 + a1
    e0 = best * npg + first
    e1 = best * npg + second

    erow = lax.broadcasted_iota(I32, (N_EXPERTS, tm), 0)
    hit0 = erow == e0
    hit1 = erow == e1
    onehot = jnp.where(hit0, 1.0, 0.0) + jnp.where(hit1, 1.0, 0.0)
    cum = _dot(onehot.astype(BF16), triu_ref[...])
    carry = carry_ref[...][:, 0:1]
    before = cum - onehot + carry
    rank0 = jnp.sum(jnp.where(hit0, before, 0.0), axis=0, keepdims=True)
    rank1 = jnp.sum(jnp.where(hit1, before, 0.0), axis=0, keepdims=True)
    new_carry = carry_ref[...] + cum[:, tm - 1:tm]
    carry_ref[...] = new_carry
    cnt_ref[...] = new_carry

    zeros_i = jnp.zeros((SUBLANES - 4, tm), I32)
    idx_ref[...] = jnp.concatenate(
        [e0, e1, rank0.astype(I32), rank1.astype(I32), zeros_i], axis=0)
    wrows = jnp.concatenate([a0 / denom, a1 / denom,
                             jnp.zeros((LANES - 2, tm), F32)], axis=0)
    wcol_ref[...] = wrows.T


def _plan_kernel(idx_ref, cnt_ref, pos_ref, items_ref, *, tm):
    ne = N_EXPERTS
    counts = cnt_ref[...][:, 0:1]

    def across(col):
        wide = jnp.concatenate([jnp.broadcast_to(col, (ne, LANES)),
                                jnp.zeros((LANES - ne, LANES), F32)], axis=0)
        return wide.T[:ne, :ne]

    r_id = lax.broadcasted_iota(I32, (ne, ne), 0)
    c_id = lax.broadcasted_iota(I32, (ne, ne), 1)

    def running_sum(col):
        return jnp.sum(jnp.where(c_id <= r_id, across(col), 0.0), axis=1, keepdims=True)

    ends = running_sum(counts)
    starts = ends - counts
    used = jnp.where(counts > 0, 1.0, 0.0)
    first_tile = jnp.floor(starts / tm)
    last_tile = jnp.floor((ends - 1.0) / tm)
    n_e = used * (last_tile - first_tile + 1.0)
    item_end = running_sum(n_e)
    item_start = item_end - n_e
    total = item_end[ne - 1:ne, :]
    later_used = jnp.logical_and(c_id > r_id, across(used) > 0)
    next_e = jnp.min(jnp.where(later_used, c_id.astype(F32), float(ne)), axis=1,
                     keepdims=True)
    next_e = jnp.where(next_e == ne, -1.0, next_e)
    ordinal = running_sum(used) - 1.0
    slot_e = ordinal - 2.0 * jnp.floor(ordinal / 2.0)

    w = lax.broadcasted_iota(I32, (1, LANES), 1).astype(F32)
    valid = w < total
    wc = jnp.minimum(w, total - 1.0)
    e_w = jnp.minimum(jnp.sum(jnp.where(item_end <= wc, 1.0, 0.0), axis=0, keepdims=True),
                      ne - 1.0)
    e_id = lax.broadcasted_iota(I32, (ne, LANES), 0).astype(F32)
    mine = e_id == e_w
    pick = lambda col: jnp.sum(jnp.where(mine, col, 0.0), axis=0, keepdims=True)
    tile_w = pick(first_tile) + (wc - pick(item_start))
    lo = jnp.where(valid, jnp.clip(pick(starts) - tile_w * tm, 0.0, tm), 0.0)
    hi = jnp.where(valid, jnp.clip(pick(ends) - tile_w * tm, 0.0, tm), 0.0)
    rows = [tile_w, e_w, lo, hi, pick(slot_e), pick(next_e)]
    rows += [jnp.zeros((1, LANES), F32)] * (SUBLANES - len(rows))
    items_ref[...] = jnp.concatenate(rows, axis=0).astype(I32)

    s = idx_ref.shape[1]
    tok_e = lax.broadcasted_iota(I32, (ne, s), 0)
    starts_i = starts.astype(I32)
    pos = []
    for k in range(2):
        base = jnp.sum(jnp.where(tok_e == idx_ref[k:k + 1, :], starts_i, 0), axis=0,
                       keepdims=True)
        pos.append(base + idx_ref[2 + k:3 + k, :])
    pos += [jnp.zeros((1, s), I32)] * (SUBLANES - len(pos))
    pos_ref[...] = jnp.concatenate(pos, axis=0)


def _plan(idx, cnt, tm):
    s = idx.shape[1]
    n_items = 2 * s // tm + N_EXPERTS - 1
    assert n_items <= LANES
    pos, items = pl.pallas_call(
        functools.partial(_plan_kernel, tm=tm),
        out_shape=[jax.ShapeDtypeStruct((SUBLANES, s), I32),
                   jax.ShapeDtypeStruct((SUBLANES, LANES), I32)],
        compiler_params=pltpu.CompilerParams(vmem_limit_bytes=VMEM_LIMIT_BYTES),
        name="plan",
    )(idx, cnt)
    return pos[0], pos[1], tuple(items[k, :n_items] for k in range(6))


def _dispatch_kernel(pos0_ref, pos1_ref, h_ref, xs_ref, sems, *, tm):
    base = pl.program_id(0) * tm

    def issue(r, carry):
        src = h_ref.at[pl.ds(r, 1), :]
        pltpu.make_async_copy(src, xs_ref.at[pl.ds(pos0_ref[base + r], 1), :],
                              sems.at[0]).start(priority=0)
        pltpu.make_async_copy(src, xs_ref.at[pl.ds(pos1_ref[base + r], 1), :],
                              sems.at[1]).start(priority=1)
        return carry

    lax.fori_loop(0, tm, issue, 0, unroll=8)
    for k in range(2):
        pltpu.make_async_copy(h_ref, xs_ref.at[pl.ds(0, tm), :], sems.at[k]).wait()


def _dispatch(pos0, pos1, h, tm):
    s = h.shape[0]
    kern = functools.partial(_dispatch_kernel, tm=tm)
    return pl.pallas_call(
        kern,
        grid_spec=pltpu.PrefetchScalarGridSpec(
            num_scalar_prefetch=2,
            grid=(s // tm,),
            in_specs=[pl.BlockSpec((tm, D_MODEL), lambda i, p0, p1: (i, 0))],
            out_specs=pl.BlockSpec(memory_space=pl.ANY),
            scratch_shapes=[pltpu.SemaphoreType.DMA((2,))]),
        out_shape=jax.ShapeDtypeStruct((2 * s, D_MODEL), F32),
        compiler_params=_params("arbitrary"),
        name="dispatch",
    )(pos0, pos1, h)


def _item_flags(tile_ref, exp_ref):
    w = pl.program_id(0)
    prev = jnp.maximum(w - 1, 0)
    fresh_tile = jnp.logical_or(w == 0, tile_ref[w] != tile_ref[prev])
    new_expert = jnp.logical_or(w == 0, exp_ref[w] != exp_ref[prev])
    return fresh_tile, new_expert


def _store_rows(o_ref, val, lo, hi, fresh_tile):
    row = lax.broadcasted_iota(I32, (val.shape[0], 1), 0)
    keep = jnp.logical_and(row >= lo, row < hi)

    @pl.when(fresh_tile)
    def _():
        o_ref[...] = jnp.where(keep, val, jnp.zeros_like(val))

    @pl.when(jnp.logical_not(fresh_tile))
    def _():
        o_ref[...] = jnp.where(keep, val, o_ref[...])


def _expert_weights(exp_ref, slot_ref, next_ref, new_expert, layer, w_hbm, stage_ref,
                    w_bf, sems):
    w = pl.program_id(0)
    n_mats = len(w_hbm)

    def fetch(expert, slot):
        return [pltpu.make_async_copy(w_hbm[m].at[layer, expert], stage_ref.at[slot, m],
                                      sems.at[slot, m]) for m in range(n_mats)]

    @pl.when(w == 0)
    def _():
        for copy in fetch(exp_ref[0], slot_ref[0]):
            copy.start(priority=1)

    @pl.when(new_expert)
    def _():
        slot = slot_ref[w]
        for m, copy in enumerate(fetch(exp_ref[w], slot)):
            copy.wait()
            w_bf[m] = stage_ref[slot, m].astype(BF16)

        @pl.when(next_ref[w] >= 0)
        def _():
            for copy in fetch(next_ref[w], 1 - slot):
                copy.start(priority=1)


def _moe_up_kernel(tile_ref, exp_ref, lo_ref, hi_ref, slot_ref, next_ref, x_ref, wg_hbm,
                   wu_hbm, o_ref, stage_ref, w_bf, sems, *, layer):
    w = pl.program_id(0)
    fresh_tile, new_expert = _item_flags(tile_ref, exp_ref)
    _expert_weights(exp_ref, slot_ref, next_ref, new_expert, layer, (wg_hbm, wu_hbm),
                    stage_ref, w_bf, sems)

    @pl.when(hi_ref[w] > lo_ref[w])
    def _():
        x = x_ref[...].astype(BF16)
        hid = _silu(_dot(x, w_bf[0])) * _dot(x, w_bf[1])
        _store_rows(o_ref, hid.astype(BF16), lo_ref[w], hi_ref[w], fresh_tile)


def _moe_down_kernel(tile_ref, exp_ref, lo_ref, hi_ref, slot_ref, next_ref, h_ref, wd_hbm,
                     o_ref, stage_ref, w_bf, sems, *, layer):
    w = pl.program_id(0)
    fresh_tile, new_expert = _item_flags(tile_ref, exp_ref)
    _expert_weights(exp_ref, slot_ref, next_ref, new_expert, layer, (wd_hbm,),
                    stage_ref, w_bf, sems)

    @pl.when(hi_ref[w] > lo_ref[w])
    def _():
        y = _dot(h_ref[...], w_bf[0])
        _store_rows(o_ref, y, lo_ref[w], hi_ref[w], fresh_tile)


def _moe(items, xs, w_gate, w_up, w_down, layer, tm):
    rows = xs.shape[0]
    n_items = items[0].shape[0]
    row_tile = lambda width: pl.BlockSpec((tm, width), lambda w, tile, *_: (tile[w], 0))
    in_hbm = pl.BlockSpec(memory_space=pl.ANY)

    def weight_scratch(n_mats, shape):
        return [pltpu.VMEM((2, n_mats) + shape, F32), pltpu.VMEM((n_mats,) + shape, BF16),
                pltpu.SemaphoreType.DMA((2, n_mats))]

    hid = pl.pallas_call(
        functools.partial(_moe_up_kernel, layer=layer),
        grid_spec=pltpu.PrefetchScalarGridSpec(
            num_scalar_prefetch=len(items),
            grid=(n_items,),
            in_specs=[row_tile(D_MODEL), in_hbm, in_hbm],
            out_specs=row_tile(D_EXPERT),
            scratch_shapes=weight_scratch(2, (D_MODEL, D_EXPERT))),
        out_shape=jax.ShapeDtypeStruct((rows, D_EXPERT), BF16),
        compiler_params=_params("arbitrary"),
        name="moe_up",
    )(*items, xs, w_gate, w_up)
    return pl.pallas_call(
        functools.partial(_moe_down_kernel, layer=layer),
        grid_spec=pltpu.PrefetchScalarGridSpec(
            num_scalar_prefetch=len(items),
            grid=(n_items,),
            in_specs=[row_tile(D_EXPERT), in_hbm],
            out_specs=row_tile(D_MODEL),
            scratch_shapes=weight_scratch(1, (D_EXPERT, D_MODEL))),
        out_shape=jax.ShapeDtypeStruct((rows, D_MODEL), F32),
        compiler_params=_params("arbitrary"),
        name="moe_down",
    )(*items, hid, w_down)


def _ple_kernel(pos0_ref, pos1_ref, hbf_ref, h_ref, p_ref, wgate_ref, wple_ref,
                wcol_ref, g_ref, b_ref, ys_ref, o_ref, obf_ref, buf_ref, sems,
                wgate_bf, wple_bf, *, tm):
    base = pl.program_id(0) * tm

    @pl.when(pl.program_id(0) == 0)
    def _():
        wgate_bf[...] = wgate_ref[0].astype(BF16)
        wple_bf[...] = wple_ref[0].astype(BF16)

    def issue(r, carry):
        for k, pos_ref in enumerate((pos0_ref, pos1_ref)):
            pltpu.make_async_copy(ys_ref.at[pl.ds(pos_ref[base + r], 1), :],
                                  buf_ref.at[k, pl.ds(r, 1), :],
                                  sems.at[k]).start(priority=k)
        return carry

    lax.fori_loop(0, tm, issue, 0, unroll=8)
    gate = _sigmoid(_dot(hbf_ref[...], wgate_bf[...]))
    ple = gate * _dot(p_ref[0, 0].astype(BF16), wple_bf[...])
    for k in range(2):
        pltpu.make_async_copy(ys_ref.at[pl.ds(0, tm), :], buf_ref.at[k], sems.at[k]).wait()
    wc = wcol_ref[...]
    moe = wc[:, 0:1] * buf_ref[0] + wc[:, 1:2] * buf_ref[1]
    out = _layer_norm(DEEPNORM_ALPHA * h_ref[...] + moe + ple, g_ref[...], b_ref[...])
    o_ref[...] = out
    obf_ref[...] = out.astype(BF16)


def _ple_combine_ln(pos0, pos1, h_bf, h, p_all, bi, w_ple_gate, w_ple, layer, wcol, g, b,
                    ys, tm):
    s = h.shape[0]
    kern = functools.partial(_ple_kernel, tm=tm)
    tile = lambda width: pl.BlockSpec((tm, width), lambda i, p0, p1: (i, 0))
    full = lambda shape: pl.BlockSpec(shape, lambda i, p0, p1: (0,) * len(shape))
    p_tile = pl.BlockSpec((1, 1, tm, PLE_DIM), lambda i, p0, p1: (layer, bi, i, 0))
    return pl.pallas_call(
        kern,
        grid_spec=pltpu.PrefetchScalarGridSpec(
            num_scalar_prefetch=2,
            grid=(s // tm,),
            in_specs=[tile(D_MODEL), tile(D_MODEL), p_tile,
                      _resident_weight((D_MODEL, D_MODEL), layer),
                      _resident_weight((PLE_DIM, D_MODEL), layer),
                      tile(LANES), full((1, D_MODEL)), full((1, D_MODEL)),
                      pl.BlockSpec(memory_space=pl.ANY)],
            out_specs=[tile(D_MODEL), tile(D_MODEL)],
            scratch_shapes=[pltpu.VMEM((2, tm, D_MODEL), F32),
                            pltpu.SemaphoreType.DMA((2,)),
                            pltpu.VMEM((D_MODEL, D_MODEL), BF16),
                            pltpu.VMEM((PLE_DIM, D_MODEL), BF16)]),
        out_shape=[jax.ShapeDtypeStruct((s, D_MODEL), F32),
                   jax.ShapeDtypeStruct((s, D_MODEL), BF16)],
        compiler_params=_params("arbitrary"),
        name="ple_combine_ln",
    )(pos0, pos1, h_bf, h, p_all, w_ple_gate, w_ple, wcol, g, b, ys)


def _row(v, width):
    v = v.astype(F32)
    return jnp.concatenate([v, jnp.zeros((width - v.shape[0],), F32)]).reshape(1, width)


def _mixer_layer(h, h_bf, prm, layer, w_router2, bias_col):
    s = h.shape[0]
    w_in_t = jnp.swapaxes(prm["w_in"], 1, 2)
    c_f = 3 * D_ATT
    c_dt = D_MAIN + XBC_TAIL
    pad = lambda rows: jnp.concatenate(
        [rows, jnp.zeros((LANES - rows.shape[0], D_MODEL), rows.dtype)], axis=0)
    w_small = jnp.concatenate(
        [pad(w_in_t[layer, c_f:c_f + ATT_HEADS]), pad(w_in_t[layer, c_dt:]),
         pad(w_in_t[layer, D_MAIN:c_dt])], axis=0).T
    b_small = jnp.concatenate([_row(prm["b_forget"][layer], LANES),
                               _row(prm["dt_bias"][layer], LANES),
                               jnp.zeros((1, LANES), F32)], axis=1)
    a_row = _row(-jnp.exp(prm["a_log"][layer].astype(F32)), LANES)
    dskip_row = jnp.repeat(prm["d_skip"][layer].astype(F32), SSD_HEAD_DIM).reshape(1, D_SSD)

    y = _in_proj(h_bf, w_in_t, layer, min(s, 1024), 1024)
    faug, dtv, tail = _gates(h_bf, w_small, b_small, min(s, 512))
    att = _attention(y, faug, min(s, 512), min(s, 256), 2)
    ssd = _ssd(y, tail, dtv, prm["conv_w"][layer], _row(prm["conv_b"][layer], D_CONV),
               a_row, dskip_row, _row(prm["ssd_norm_w"][layer], D_SSD))
    return _out_proj_ln_route(att, ssd, prm["w_out"], layer, h,
                              _row(prm["ln1_g"][layer], D_MODEL),
                              _row(prm["ln1_b"][layer], D_MODEL),
                              w_router2, bias_col, min(s, 256))


def _channel_layer(h, h_bf, routing, p_all, layer, bi, prm):
    s = h.shape[0]
    tm = min(s, 256)
    idx, wcol, cnt = routing
    pos0, pos1, items = _plan(idx, cnt, tm)
    xs = _dispatch(pos0, pos1, h, tm)
    ys = _moe(items, xs, prm["w_gate"], prm["w_up"], prm["w_down"], layer, tm)
    return _ple_combine_ln(pos0, pos1, h_bf, h, p_all, bi, prm["w_ple_gate"], prm["w_ple"],
                           layer, wcol, _row(prm["ln2_g"][layer], D_MODEL),
                           _row(prm["ln2_b"][layer], D_MODEL), ys, tm)


def _prep_router(p):
    order = jnp.arange(N_EXPERTS).reshape(N_EXPERT_GROUPS, EXPERTS_PER_GROUP).T.reshape(-1)
    w = p["w_router"].astype(F32)[:, order]
    w = jnp.concatenate([w, jnp.zeros((D_MODEL, LANES - N_EXPERTS), F32)], axis=1)
    w_hi = w.astype(BF16)
    w_lo = (w - w_hi.astype(F32)).astype(BF16)
    bias_col = p["router_bias"].astype(F32)[order].reshape(N_EXPERTS, 1)
    return jnp.stack([w_hi, w_lo]), bias_col


def kernel(x, p, w_in, b_forget, conv_w, conv_b, dt_bias, a_log, d_skip, ssd_norm_w,
           w_out, ln1_g, ln1_b, w_router, router_bias, w_gate, w_up, w_down, w_ple,
           w_ple_gate, ln2_g, ln2_b):
    prm = dict(w_in=w_in, b_forget=b_forget, conv_w=conv_w, conv_b=conv_b,
               dt_bias=dt_bias, a_log=a_log, d_skip=d_skip, ssd_norm_w=ssd_norm_w,
               w_out=w_out, ln1_g=ln1_g, ln1_b=ln1_b, w_router=w_router,
               router_bias=router_bias, w_gate=w_gate, w_up=w_up, w_down=w_down,
               w_ple=w_ple, w_ple_gate=w_ple_gate, ln2_g=ln2_g, ln2_b=ln2_b)
    batch = x.shape[0]
    w_router2, bias_col = _prep_router(prm)
    outs = []
    for bi in range(batch):
        h = x[bi]
        h_bf = h.astype(BF16)
        for layer in range(w_in.shape[0]):
            h, h_bf, *routing = _mixer_layer(h, h_bf, prm, layer, w_router2, bias_col)
            h, h_bf = _channel_layer(h, h_bf, routing, p, layer, bi, prm)
        outs.append(h)
    return jnp.stack(outs)
```

```python
import functools
import math

import jax
import jax.numpy as jnp
from jax import lax
from jax.experimental import pallas as pl
from jax.experimental.pallas import tpu as pltpu

F32 = jnp.float32
BF16 = jnp.bfloat16
I32 = jnp.int32

D_MODEL = 2048
ATT_HEADS = 8
ATT_HEAD_DIM = 128
D_ATT = ATT_HEADS * ATT_HEAD_DIM
SSD_HEADS = 16
SSD_HEAD_DIM = 64
D_SSD = SSD_HEADS * SSD_HEAD_DIM
SSD_GROUPS = 4
HEADS_PER_GROUP = SSD_HEADS // SSD_GROUPS
SSD_STATE = 128
D_GROUP = D_SSD // SSD_GROUPS
CONV_WIDTH = 4
D_BC = SSD_GROUPS * SSD_STATE
D_CONV = D_SSD + 2 * D_BC
N_EXPERTS = 16
N_EXPERT_GROUPS = 4
EXPERTS_PER_GROUP = N_EXPERTS // N_EXPERT_GROUPS
D_EXPERT = 1024
PLE_DIM = 256
DEPTH = 2
DEEPNORM_ALPHA = (2 * DEPTH) ** 0.25
LN_EPS = 1e-5
RMS_EPS = 1e-5
LOG2_E = math.log2(math.e)

LANES = 128
SUBLANES = 8
VMEM_LIMIT_BYTES = 56 * 1024 * 1024

D_MAIN = 3 * D_ATT + D_SSD + D_CONV
COL_WIN = 3 * D_ATT
D_WIN = D_MAIN - COL_WIN
XBC_TAIL = ATT_HEADS

SSD_CHUNK = 128
CONV_HALO = SUBLANES
SUM_ROWS = 2 * SUBLANES
GATE_BLOCKS = 4


def _params(*sem):
    return pltpu.CompilerParams(dimension_semantics=sem,
                                vmem_limit_bytes=VMEM_LIMIT_BYTES)


def _softplus(u):
    return jnp.maximum(u, 0.0) + jnp.log1p(jnp.exp(-jnp.abs(u)))


def _sigmoid(u):
    return 1.0 / (1.0 + jnp.exp(-u))


def _silu(u):
    return u * _sigmoid(u)


def _split3(v):
    hi = v.astype(BF16)
    r1 = v - hi.astype(F32)
    mid = r1.astype(BF16)
    lo = (r1 - mid.astype(F32)).astype(BF16)
    return hi, mid, lo


def _dot(a, b):
    return jnp.dot(a, b, preferred_element_type=F32)


def _exact_left_dot(ones_bf16, v):
    hi, mid, lo = _split3(v)
    return _dot(ones_bf16, hi) + (_dot(ones_bf16, mid) + _dot(ones_bf16, lo))


def _proj_kernel(x_ref, w_ref, o_ref, w_bf, *, q_blocks):
    @pl.when(pl.program_id(1) == 0)
    def _():
        w_bf[...] = w_ref[0].astype(BF16)

    scale = jnp.where(pl.program_id(0) < q_blocks, LOG2_E * ATT_HEAD_DIM ** -0.5, 1.0)
    o_ref[...] = (_dot(x_ref[...], w_bf[...]) * scale).astype(o_ref.dtype)


def _in_proj(h_bf, w_in, layer, tm, tn):
    s = h_bf.shape[0]
    return pl.pallas_call(
        functools.partial(_proj_kernel, q_blocks=D_ATT // tn),
        grid=(D_MAIN // tn, s // tm),
        in_specs=[pl.BlockSpec((tm, D_MODEL), lambda j, i: (i, 0)),
                  pl.BlockSpec((1, D_MODEL, tn), lambda j, i: (layer, 0, j))],
        out_specs=pl.BlockSpec((tm, tn), lambda j, i: (i, j)),
        out_shape=jax.ShapeDtypeStruct((s, D_MAIN), BF16),
        scratch_shapes=[pltpu.VMEM((D_MODEL, tn), BF16)],
        compiler_params=_params("arbitrary", "arbitrary"),
        name="in_proj",
    )(h_bf, w_in)


def _gates_kernel(x_ref, w_ref, b_ref, tri_ref, sel_ref, ones_ref, faug_ref, dt_ref,
                  tail_ref, carry_ref, w_bf):
    @pl.when(pl.program_id(0) == 0)
    def _():
        carry_ref[...] = jnp.zeros_like(carry_ref)
        w_bf[...] = w_ref[...].astype(BF16)

    v = _dot(x_ref[...], w_bf[...]) + b_ref[...]
    log_f = -_softplus(-v[:, :LANES])
    dt_ref[...] = _softplus(v[:, LANES:2 * LANES])
    tail_ref[...] = v[:, 2 * LANES:]
    fcum = _exact_left_dot(tri_ref[...], log_f) + carry_ref[...]
    carry_ref[...] = fcum[-1:, :]
    hi, mid, lo = _split3(fcum * LOG2_E)
    aug = (_dot(hi, sel_ref[0]) + _dot(mid, sel_ref[1]) + _dot(lo, sel_ref[2])
           + ones_ref[...])
    faug_ref[...] = aug.astype(BF16)


def _forget_layout():
    import numpy as np
    sel = np.zeros((3, LANES, 2 * D_ATT), np.float32)
    ones = np.zeros((1, 2 * D_ATT), np.float32)
    for h in range(ATT_HEADS):
        for k in range(3):
            sel[k, h, ATT_HEAD_DIM * h + k] = 1.0
            sel[k, h, D_ATT + ATT_HEAD_DIM * h + 3 + k] = -1.0
            ones[0, ATT_HEAD_DIM * h + 3 + k] = 1.0
            ones[0, D_ATT + ATT_HEAD_DIM * h + k] = 1.0
    return jnp.asarray(sel, BF16), jnp.asarray(ones, F32)


def _gates(h_bf, w_small, b_small, tm):
    s = h_bf.shape[0]
    tri = jnp.tril(jnp.ones((tm, tm), F32)).astype(BF16)
    sel, ones = _forget_layout()
    return pl.pallas_call(
        _gates_kernel,
        grid=(s // tm,),
        in_specs=[pl.BlockSpec((tm, D_MODEL), lambda i: (i, 0)),
                  pl.BlockSpec((D_MODEL, 3 * LANES), lambda i: (0, 0)),
                  pl.BlockSpec((1, 3 * LANES), lambda i: (0, 0)),
                  pl.BlockSpec((tm, tm), lambda i: (0, 0)),
                  pl.BlockSpec((3, LANES, 2 * D_ATT), lambda i: (0, 0, 0)),
                  pl.BlockSpec((1, 2 * D_ATT), lambda i: (0, 0))],
        out_specs=[pl.BlockSpec((tm, 2 * D_ATT), lambda i: (i, 0)),
                   pl.BlockSpec((tm, LANES), lambda i: (i, 0)),
                   pl.BlockSpec((tm, LANES), lambda i: (i, 0))],
        out_shape=[jax.ShapeDtypeStruct((s, 2 * D_ATT), BF16),
                   jax.ShapeDtypeStruct((s, LANES), F32),
                   jax.ShapeDtypeStruct((s, LANES), F32)],
        scratch_shapes=[pltpu.VMEM((1, LANES), F32),
                        pltpu.VMEM((D_MODEL, 3 * LANES), BF16)],
        compiler_params=_params("arbitrary"),
        name="gates",
    )(h_bf, w_small, b_small, tri, sel, ones)


def _attn_kernel(q_ref, fq_ref, k_ref, fk_ref, v_ref, o_ref, vt_ref, sa_ref, sb_ref,
                 *, tq, tk, nh):
    assert tq == 2 * tk
    i = pl.program_id(1)
    n_chunks = vt_ref.shape[1]
    dh = ATT_HEAD_DIM
    heads = range(nh)

    @pl.when(i == 0)
    def _():
        for hd in heads:
            for c in range(n_chunks):
                blk = v_ref[c * tk:(c + 1) * tk, hd * dh:(hd + 1) * dh]
                vt_ref[hd, c, :dh, :] = blk.astype(F32).T.astype(BF16)
                vt_ref[hd, c, dh:, :] = jnp.ones((SUM_ROWS, tk), BF16)

    q_t = []
    for hd in heads:
        cols = slice(hd * dh, (hd + 1) * dh)
        q_aug = jnp.concatenate([q_ref[:, cols], fq_ref[:, cols]], axis=1)
        q_t.append(q_aug.astype(F32).T.astype(BF16))
    kv_pos = lax.broadcasted_iota(I32, (tk, tq), 0)
    q_pos = lax.broadcasted_iota(I32, (tk, tq), 1)

    def logits(hd, c):
        start = pl.multiple_of(c * tk, tk)
        cols = slice(hd * dh, (hd + 1) * dh)
        k_aug = jnp.concatenate([k_ref[pl.ds(start, tk), cols],
                                 fk_ref[pl.ds(start, tk), cols]], axis=1)
        return _dot(k_aug, q_t[hd])

    def update(st, hd, c, carry, diag_offset):
        m, acc = carry
        if diag_offset is not None:
            st = jnp.where(kv_pos + diag_offset <= q_pos, st, -jnp.inf)
        m_new = jnp.maximum(m, jnp.max(st, axis=0, keepdims=True))
        p = jnp.exp2(st - m_new)
        alpha = jnp.exp2(m - m_new)
        acc = alpha * acc + _dot(vt_ref[hd, c], p.astype(BF16))
        return m_new, acc

    first = 2 * i
    for hd in heads:
        sa_ref[hd] = logits(hd, 0)

    def pair(t, carries):
        out = []
        for hd in heads:
            sb_ref[hd] = logits(hd, 2 * t + 1)
        for hd in heads:
            out.append(update(sa_ref[hd], hd, 2 * t, carries[hd], None))
        for hd in heads:
            sa_ref[hd] = logits(hd, 2 * t + 2)
        return tuple(update(sb_ref[hd], hd, 2 * t + 1, out[hd], None) for hd in heads)

    init = (jnp.full((1, tq), -jnp.inf, F32), jnp.zeros((dh + SUM_ROWS, tq), F32))
    carries = lax.fori_loop(0, i, pair, tuple(init for _ in heads))
    for hd in heads:
        sb_ref[hd] = logits(hd, first + 1)
    for hd in heads:
        carry = update(sa_ref[hd], hd, first, carries[hd], 0)
        _, acc = update(sb_ref[hd], hd, first + 1, carry, tk)
        out = acc[:dh, :] / acc[dh:dh + 1, :]
        o_ref[:, hd * dh:(hd + 1) * dh] = out.T.astype(o_ref.dtype)


def _attention(y, faug, tq, tk, nh):
    s = y.shape[0]
    width = nh * ATT_HEAD_DIM
    groups = ATT_HEADS // nh
    kern = functools.partial(_attn_kernel, tq=tq, tk=tk, nh=nh)
    q_block = lambda g0: pl.BlockSpec((tq, width), lambda g, i: (i, g0 + g))
    kv_block = lambda g0: pl.BlockSpec((s, width), lambda g, i: (0, g0 + g))
    return pl.pallas_call(
        kern,
        grid=(groups, s // tq),
        in_specs=[q_block(0), q_block(0), kv_block(groups), kv_block(groups),
                  kv_block(2 * groups)],
        out_specs=pl.BlockSpec((tq, width), lambda g, i: (i, g)),
        out_shape=jax.ShapeDtypeStruct((s, D_ATT), BF16),
        scratch_shapes=[pltpu.VMEM((nh, s // tk, ATT_HEAD_DIM + SUM_ROWS, tk), BF16),
                        pltpu.VMEM((nh, tk, tq), F32), pltpu.VMEM((nh, tk, tq), F32)],
        compiler_params=_params("arbitrary", "arbitrary"),
        name="fox_attention",
    )(y, faug, y, faug, y)


def _expand_heads(w, n_heads):
    length = w.shape[0]
    lane = lax.broadcasted_iota(I32, (length, LANES), 1)
    blocks = []
    for m in range(n_heads // 2):
        blocks.append(jnp.where(lane < SSD_HEAD_DIM, w[:, 2 * m:2 * m + 1],
                                w[:, 2 * m + 1:2 * m + 2]))
    return jnp.concatenate(blocks, axis=1)


def _ssd_kernel(win_ref, tail_ref, dt_ref, convw_ref, convb_ref, a_ref, dskip_ref,
                normw_ref, tri_ref, o_ref, ubuf_ref, state_ref):
    length = SSD_CHUNK

    @pl.when(pl.program_id(0) == 0)
    def _():
        ubuf_ref[0:CONV_HALO, :] = jnp.zeros((CONV_HALO, D_CONV), F32)
        state_ref[...] = jnp.zeros_like(state_ref)

    win = jnp.concatenate([win_ref[...].astype(F32), tail_ref[...]], axis=1)
    win = jnp.concatenate([win[:, ATT_HEADS:], win[:, :ATT_HEADS]], axis=1)
    z = win[:, :D_SSD]

    ubuf_ref[CONV_HALO:CONV_HALO + length, :] = win[:, D_SSD:D_SSD + D_CONV]
    conv = convb_ref[...]
    for k in range(CONV_WIDTH):
        off = CONV_HALO - (CONV_WIDTH - 1) + k
        conv = conv + convw_ref[k:k + 1, :] * ubuf_ref[off:off + length, :]
    tail = ubuf_ref[length:length + CONV_HALO, :]
    ubuf_ref[0:CONV_HALO, :] = tail
    xc = _silu(conv)
    xs = xc[:, :D_SSD]

    dt = dt_ref[...]
    da = dt * a_ref[...]
    acum = _exact_left_dot(tri_ref[...], da)
    acum_t = acum.T
    a_last = acum[length - 1:length, :]
    w_off = _expand_heads(jnp.exp(acum), SSD_HEADS)
    w_state = _expand_heads(jnp.exp(a_last - acum) * dt, SSD_HEADS)
    w_dt = _expand_heads(dt, SSD_HEADS)
    chunk_decay = _expand_heads(jnp.exp(a_last), SSD_HEADS)

    x_dt = (xs * w_dt).astype(BF16)
    x_state = (xs * w_state).astype(BF16)
    row = lax.broadcasted_iota(I32, (length, length), 0)
    col = lax.broadcasted_iota(I32, (length, length), 1)
    causal = col <= row
    grp_lane = lax.broadcasted_iota(I32, (length, D_GROUP), 1) // SSD_HEAD_DIM

    y_parts = []
    for g in range(SSD_GROUPS):
        b_g = xc[:, D_SSD + g * SSD_STATE:D_SSD + (g + 1) * SSD_STATE].astype(BF16)
        c_g = xc[:, D_SSD + D_BC + g * SSD_STATE:
                 D_SSD + D_BC + (g + 1) * SSD_STATE].astype(BF16)
        cb = lax.dot_general(c_g, b_g, (((1,), (1,)), ((), ())),
                             preferred_element_type=F32)
        cols = slice(g * D_GROUP, (g + 1) * D_GROUP)
        x_dt_g = x_dt[:, cols]
        m_blocks, x_blocks = [], []
        for r in range(HEADS_PER_GROUP):
            hd = g * HEADS_PER_GROUP + r
            seg = acum[:, hd:hd + 1] - acum_t[hd:hd + 1, :]
            decay = jnp.exp(jnp.where(causal, seg, -jnp.inf))
            m_blocks.append((cb * decay).astype(BF16))
            x_blocks.append(jnp.where(grp_lane == r, x_dt_g, jnp.zeros_like(x_dt_g)))
        y_diag = _dot(jnp.concatenate(m_blocks, axis=1),
                      jnp.concatenate(x_blocks, axis=0))
        st = state_ref[g]
        y_off = _dot(c_g, st.astype(BF16)) * w_off[:, cols]
        y_parts.append(y_diag + y_off)
        new_st = lax.dot_general(b_g, x_state[:, cols], (((0,), (0,)), ((), ())),
                                 preferred_element_type=F32)
        state_ref[g] = st * chunk_decay[:, cols] + new_st

    y = jnp.concatenate(y_parts, axis=1) + dskip_ref[...] * xs
    yg = y * _silu(z)
    outs = []
    for g in range(SSD_GROUPS):
        blk = yg[:, g * D_GROUP:(g + 1) * D_GROUP]
        ms = jnp.mean(blk * blk, axis=1, keepdims=True)
        outs.append(blk * lax.rsqrt(ms + RMS_EPS))
    o_ref[...] = (jnp.concatenate(outs, axis=1) * normw_ref[...]).astype(o_ref.dtype)


def _ssd(y, tail, dtv, conv_w, conv_b, a_row, dskip_row, normw_row):
    s = y.shape[0]
    length = SSD_CHUNK
    tri = jnp.tril(jnp.ones((length, length), F32)).astype(BF16)
    full = lambda shape: pl.BlockSpec(shape, lambda c: (0,) * len(shape))
    return pl.pallas_call(
        _ssd_kernel,
        grid=(s // length,),
        in_specs=[pl.BlockSpec((length, D_WIN), lambda c: (c, COL_WIN // D_WIN)),
                  pl.BlockSpec((length, LANES), lambda c: (c, 0)),
                  pl.BlockSpec((length, LANES), lambda c: (c, 0)),
                  full((CONV_WIDTH, D_CONV)), full((1, D_CONV)), full((1, LANES)),
                  full((1, D_SSD)), full((1, D_SSD)), full((length, length))],
        out_specs=pl.BlockSpec((length, D_SSD), lambda c: (c, 0)),
        out_shape=jax.ShapeDtypeStruct((s, D_SSD), BF16),
        scratch_shapes=[pltpu.VMEM((length + CONV_HALO, D_CONV), F32),
                        pltpu.VMEM((SSD_GROUPS, SSD_STATE, D_GROUP), F32)],
        compiler_params=_params("arbitrary"),
        name="ssd",
    )(y, tail, dtv, conv_w, conv_b, a_row, dskip_row, normw_row, tri)


def _layer_norm(xf, g, b):
    mu = jnp.mean(xf, axis=1, keepdims=True)
    xc = xf - mu
    var = jnp.mean(xc * xc, axis=1, keepdims=True)
    return xc * lax.rsqrt(var + LN_EPS) * g + b


def _resident_weight(shape, layer):
    return pl.BlockSpec((1,) + shape, lambda i, *_: (layer,) + (0,) * len(shape),
                        pipeline_mode=pl.Buffered(1))


def _out_proj_kernel(att_ref, ssd_ref, w_ref, h_ref, g_ref, b_ref, wr_ref, rbias_ref,
                     triu_ref, o_ref, obf_ref, idx_ref, wcol_ref, cnt_ref, w_bf,
                     carry_ref):
    @pl.when(pl.program_id(0) == 0)
    def _():
        w_bf[...] = w_ref[0].astype(BF16)

    mix = _dot(att_ref[...], w_bf[:D_ATT, :]) + _dot(ssd_ref[...], w_bf[D_ATT:, :])
    out = _layer_norm(DEEPNORM_ALPHA * h_ref[...] + mix, g_ref[...], b_ref[...])
    o_ref[...] = out
    obf_ref[...] = out.astype(BF16)
    _route(out, wr_ref, rbias_ref, triu_ref, idx_ref, wcol_ref, cnt_ref, carry_ref)


def _out_proj_ln_route(att, ssd, w_out, layer, h, g, b, w_router2, bias_col, tm):
    s = h.shape[0]
    full = lambda shape: pl.BlockSpec(shape, lambda i: (0,) * len(shape))
    row_tile = lambda width: pl.BlockSpec((tm, width), lambda i: (i, 0))
    triu = jnp.triu(jnp.ones((tm, tm), F32)).astype(BF16)
    return pl.pallas_call(
        _out_proj_kernel,
        grid=(s // tm,),
        in_specs=[row_tile(D_ATT), row_tile(D_SSD),
                  _resident_weight((D_MODEL, D_MODEL), layer),
                  row_tile(D_MODEL), full((1, D_MODEL)), full((1, D_MODEL)),
                  full((2, D_MODEL, LANES)), full((N_EXPERTS, 1)), full((tm, tm))],
        out_specs=[row_tile(D_MODEL), row_tile(D_MODEL),
                   pl.BlockSpec((SUBLANES, tm), lambda i: (0, i)),
                   row_tile(LANES), full((N_EXPERTS, LANES))],
        out_shape=[jax.ShapeDtypeStruct((s, D_MODEL), F32),
                   jax.ShapeDtypeStruct((s, D_MODEL), BF16),
                   jax.ShapeDtypeStruct((SUBLANES, s), I32),
                   jax.ShapeDtypeStruct((s, LANES), F32),
                   jax.ShapeDtypeStruct((N_EXPERTS, LANES), F32)],
        scratch_shapes=[pltpu.VMEM((D_MODEL, D_MODEL), BF16),
                        pltpu.VMEM((N_EXPERTS, LANES), F32)],
        compiler_params=_params("arbitrary"),
        name="out_proj_ln_route",
    )(att, ssd, w_out, h, g, b, w_router2, bias_col, triu)


def _route(h, wr_ref, bias_ref, triu_ref, idx_ref, wcol_ref, cnt_ref, carry_ref):
    tm = h.shape[0]

    @pl.when(pl.program_id(0) == 0)
    def _():
        carry_ref[...] = jnp.zeros_like(carry_ref)

    h_hi = h.astype(BF16)
    h_lo = (h - h_hi.astype(F32)).astype(BF16)
    logits = _dot(h_hi, wr_ref[0]) + (_dot(h_hi, wr_ref[1]) + _dot(h_lo, wr_ref[0]))
    lt = logits.T
    scores = _sigmoid(lt[:N_EXPERTS, :])
    biased = scores + bias_ref[...]
    npg = EXPERTS_PER_GROUP
    s_r = [scores[npg * r:npg * (r + 1), :] for r in range(npg)]
    b_r = [biased[npg * r:npg * (r + 1), :] for r in range(npg)]

    gs = None
    for a in range(npg):
        for b in range(a + 1, npg):
            pair = b_r[a] + b_r[b]
            gs = pair if gs is None else jnp.maximum(gs, pair)
    best = jnp.zeros((1, tm), I32)
    top = gs[0:1, :]
    for g in range(1, N_EXPERT_GROUPS):
        better = gs[g:g + 1, :] > top
        best = jnp.where(better, g, best)
        top = jnp.where(better, gs[g:g + 1, :], top)

    def pick_group(a):
        out = a[0:1, :]
        for g in range(1, N_EXPERT_GROUPS):
            out = jnp.where(best == g, a[g:g + 1, :], out)
        return out

    sb = [pick_group(b) for b in b_r]
    ss = [pick_group(s) for s in s_r]
    first = jnp.zeros((1, tm), I32)
    fmax = sb[0]
    for r in range(1, npg):
        better = sb[r] > fmax
        first = jnp.where(better, r, first)
        fmax = jnp.where(better, sb[r], fmax)
    second = jnp.zeros((1, tm), I32)
    smax = jnp.full((1, tm), -jnp.inf, F32)
    for r in range(npg):
        cand = jnp.where(first == r, -jnp.inf, sb[r])
        better = cand > smax
        second = jnp.where(better, r, second)
        smax = jnp.where(better, cand, smax)

    def pick_expert(which):
        out = ss[0]
        for r in range(1, npg):
            out = jnp.where(which == r, ss[r], out)
        return out

    a0 = pick_expert(first)
    a1 = pick_expert(second)
    denom = a0 + a1
    e0 = best * npg + first
    e1 = best * npg + second

    erow = lax.broadcasted_iota(I32, (N_EXPERTS, tm), 0)
    hit0 = erow == e0
    hit1 = erow == e1
    onehot = jnp.where(hit0, 1.0, 0.0) + jnp.where(hit1, 1.0, 0.0)
    cum = _dot(onehot.astype(BF16), triu_ref[...])
    carry = carry_ref[...][:, 0:1]
    before = cum - onehot + carry
    rank0 = jnp.sum(jnp.where(hit0, before, 0.0), axis=0, keepdims=True)
    rank1 = jnp.sum(jnp.where(hit1, before, 0.0), axis=0, keepdims=True)
    new_carry = carry_ref[...] + cum[:, tm - 1:tm]
    carry_ref[...] = new_carry
    cnt_ref[...] = new_carry

    zeros_i = jnp.zeros((SUBLANES - 4, tm), I32)
    idx_ref[...] = jnp.concatenate(
        [e0, e1, rank0.astype(I32), rank1.astype(I32), zeros_i], axis=0)
    wrows = jnp.concatenate([a0 / denom, a1 / denom,
                             jnp.zeros((LANES - 2, tm), F32)], axis=0)
    wcol_ref[...] = wrows.T


def _plan_kernel(idx_ref, cnt_ref, pos_ref, items_ref, *, tm):
    ne = N_EXPERTS
    counts = cnt_ref[...][:, 0:1]

    def across(col):
        wide = jnp.concatenate([jnp.broadcast_to(col, (ne, LANES)),
                                jnp.zeros((LANES - ne, LANES), F32)], axis=0)
        return wide.T[:ne, :ne]

    r_id = lax.broadcasted_iota(I32, (ne, ne), 0)
    c_id = lax.broadcasted_iota(I32, (ne, ne), 1)

    def running_sum(col):
        return jnp.sum(jnp.where(c_id <= r_id, across(col), 0.0), axis=1, keepdims=True)

    ends = running_sum(counts)
    starts = ends - counts
    used = jnp.where(counts > 0, 1.0, 0.0)
    first_tile = jnp.floor(starts / tm)
    last_tile = jnp.floor((ends - 1.0) / tm)
    n_e = used * (last_tile - first_tile + 1.0)
    item_end = running_sum(n_e)
    item_start = item_end - n_e
    total = item_end[ne - 1:ne, :]
    later_used = jnp.logical_and(c_id > r_id, across(used) > 0)
    next_e = jnp.min(jnp.where(later_used, c_id.astype(F32), float(ne)), axis=1,
                     keepdims=True)
    next_e = jnp.where(next_e == ne, -1.0, next_e)
    ordinal = running_sum(used) - 1.0
    slot_e = ordinal - 2.0 * jnp.floor(ordinal / 2.0)

    w = lax.broadcasted_iota(I32, (1, LANES), 1).astype(F32)
    valid = w < total
    wc = jnp.minimum(w, total - 1.0)
    e_w = jnp.minimum(jnp.sum(jnp.where(item_end <= wc, 1.0, 0.0), axis=0, keepdims=True),
                      ne - 1.0)
    e_id = lax.broadcasted_iota(I32, (ne, LANES), 0).astype(F32)
    mine = e_id == e_w
    pick = lambda col: jnp.sum(jnp.where(mine, col, 0.0), axis=0, keepdims=True)
    tile_w = pick(first_tile) + (wc - pick(item_start))
    lo = jnp.where(valid, jnp.clip(pick(starts) - tile_w * tm, 0.0, tm), 0.0)
    hi = jnp.where(valid, jnp.clip(pick(ends) - tile_w * tm, 0.0, tm), 0.0)
    rows = [tile_w, e_w, lo, hi, pick(slot_e), pick(next_e)]
    rows += [jnp.zeros((1, LANES), F32)] * (SUBLANES - len(rows))
    items_ref[...] = jnp.concatenate(rows, axis=0).astype(I32)

    s = idx_ref.shape[1]
    tok_e = lax.broadcasted_iota(I32, (ne, s), 0)
    starts_i = starts.astype(I32)
    pos = []
    for k in range(2):
        base = jnp.sum(jnp.where(tok_e == idx_ref[k:k + 1, :], starts_i, 0), axis=0,
                       keepdims=True)
        pos.append(base + idx_ref[2 + k:3 + k, :])
    pos += [jnp.zeros((1, s), I32)] * (SUBLANES - len(pos))
    pos_ref[...] = jnp.concatenate(pos, axis=0)


def _plan(idx, cnt, tm):
    s = idx.shape[1]
    n_items = 2 * s // tm + N_EXPERTS - 1
    assert n_items <= LANES
    pos, items = pl.pallas_call(
        functools.partial(_plan_kernel, tm=tm),
        out_shape=[jax.ShapeDtypeStruct((SUBLANES, s), I32),
                   jax.ShapeDtypeStruct((SUBLANES, LANES), I32)],
        compiler_params=pltpu.CompilerParams(vmem_limit_bytes=VMEM_LIMIT_BYTES),
        name="plan",
    )(idx, cnt)
    return pos[0], pos[1], tuple(items[k, :n_items] for k in range(6))


def _dispatch_kernel(pos0_ref, pos1_ref, h_ref, xs_ref, sems, *, tm):
    base = pl.program_id(0) * tm

    def issue(r, carry):
        src = h_ref.at[pl.ds(r, 1), :]
        pltpu.make_async_copy(src, xs_ref.at[pl.ds(pos0_ref[base + r], 1), :],
                              sems.at[0]).start(priority=0)
        pltpu.make_async_copy(src, xs_ref.at[pl.ds(pos1_ref[base + r], 1), :],
                              sems.at[1]).start(priority=1)
        return carry

    lax.fori_loop(0, tm, issue, 0, unroll=8)
    for k in range(2):
        pltpu.make_async_copy(h_ref, xs_ref.at[pl.ds(0, tm), :], sems.at[k]).wait()


def _dispatch(pos0, pos1, h, tm):
    s = h.shape[0]
    kern = functools.partial(_dispatch_kernel, tm=tm)
    return pl.pallas_call(
        kern,
        grid_spec=pltpu.PrefetchScalarGridSpec(
            num_scalar_prefetch=2,
            grid=(s // tm,),
            in_specs=[pl.BlockSpec((tm, D_MODEL), lambda i, p0, p1: (i, 0))],
            out_specs=pl.BlockSpec(memory_space=pl.ANY),
            scratch_shapes=[pltpu.SemaphoreType.DMA((2,))]),
        out_shape=jax.ShapeDtypeStruct((2 * s, D_MODEL), F32),
        compiler_params=_params("arbitrary"),
        name="dispatch",
    )(pos0, pos1, h)


def _item_flags(tile_ref, exp_ref):
    w = pl.program_id(0)
    prev = jnp.maximum(w - 1, 0)
    fresh_tile = jnp.logical_or(w == 0, tile_ref[w] != tile_ref[prev])
    new_expert = jnp.logical_or(w == 0, exp_ref[w] != exp_ref[prev])
    return fresh_tile, new_expert


def _store_rows(o_ref, val, lo, hi, fresh_tile):
    row = lax.broadcasted_iota(I32, (val.shape[0], 1), 0)
    keep = jnp.logical_and(row >= lo, row < hi)

    @pl.when(fresh_tile)
    def _():
        o_ref[...] = jnp.where(keep, val, jnp.zeros_like(val))

    @pl.when(jnp.logical_not(fresh_tile))
    def _():
        o_ref[...] = jnp.where(keep, val, o_ref[...])


def _expert_weights(exp_ref, slot_ref, next_ref, new_expert, layer, w_hbm, stage_ref,
                    w_bf, sems):
    w = pl.program_id(0)
    n_mats = len(w_hbm)

    def fetch(expert, slot):
        return [pltpu.make_async_copy(w_hbm[m].at[layer, expert], stage_ref.at[slot, m],
                                      sems.at[slot, m]) for m in range(n_mats)]

    @pl.when(w == 0)
    def _():
        for copy in fetch(exp_ref[0], slot_ref[0]):
            copy.start(priority=1)

    @pl.when(new_expert)
    def _():
        slot = slot_ref[w]
        for m, copy in enumerate(fetch(exp_ref[w], slot)):
            copy.wait()
            w_bf[m] = stage_ref[slot, m].astype(BF16)

        @pl.when(next_ref[w] >= 0)
        def _():
            for copy in fetch(next_ref[w], 1 - slot):
                copy.start(priority=1)


def _moe_up_kernel(tile_ref, exp_ref, lo_ref, hi_ref, slot_ref, next_ref, x_ref, wg_hbm,
                   wu_hbm, o_ref, stage_ref, w_bf, sems, *, layer):
    w = pl.program_id(0)
    fresh_tile, new_expert = _item_flags(tile_ref, exp_ref)
    _expert_weights(exp_ref, slot_ref, next_ref, new_expert, layer, (wg_hbm, wu_hbm),
                    stage_ref, w_bf, sems)

    @pl.when(hi_ref[w] > lo_ref[w])
    def _():
        x = x_ref[...].astype(BF16)
        hid = _silu(_dot(x, w_bf[0])) * _dot(x, w_bf[1])
        _store_rows(o_ref, hid.astype(BF16), lo_ref[w], hi_ref[w], fresh_tile)


def _moe_down_kernel(tile_ref, exp_ref, lo_ref, hi_ref, slot_ref, next_ref, h_ref, wd_hbm,
                     o_ref, stage_ref, w_bf, sems, *, layer):
    w = pl.program_id(0)
    fresh_tile, new_expert = _item_flags(tile_ref, exp_ref)
    _expert_weights(exp_ref, slot_ref, next_ref, new_expert, layer, (wd_hbm,),
                    stage_ref, w_bf, sems)

    @pl.when(hi_ref[w] > lo_ref[w])
    def _():
        y = _dot(h_ref[...], w_bf[0])
        _store_rows(o_ref, y, lo_ref[w], hi_ref[w], fresh_tile)


def _moe(items, xs, w_gate, w_up, w_down, layer, tm):
    rows = xs.shape[0]
    n_items = items[0].shape[0]
    row_tile = lambda width: pl.BlockSpec((tm, width), lambda w, tile, *_: (tile[w], 0))
    in_hbm = pl.BlockSpec(memory_space=pl.ANY)

    def weight_scratch(n_mats, shape):
        return [pltpu.VMEM((2, n_mats) + shape, F32), pltpu.VMEM((n_mats,) + shape, BF16),
                pltpu.SemaphoreType.DMA((2, n_mats))]

    hid = pl.pallas_call(
        functools.partial(_moe_up_kernel, layer=layer),
        grid_spec=pltpu.PrefetchScalarGridSpec(
            num_scalar_prefetch=len(items),
            grid=(n_items,),
            in_specs=[row_tile(D_MODEL), in_hbm, in_hbm],
            out_specs=row_tile(D_EXPERT),
            scratch_shapes=weight_scratch(2, (D_MODEL, D_EXPERT))),
        out_shape=jax.ShapeDtypeStruct((rows, D_EXPERT), BF16),
        compiler_params=_params("arbitrary"),
        name="moe_up",
    )(*items, xs, w_gate, w_up)
    return pl.pallas_call(
        functools.partial(_moe_down_kernel, layer=layer),
        grid_spec=pltpu.PrefetchScalarGridSpec(
            num_scalar_prefetch=len(items),
            grid=(n_items,),
            in_specs=[row_tile(D_EXPERT), in_hbm],
            out_specs=row_tile(D_MODEL),
            scratch_shapes=weight_scratch(1, (D_EXPERT, D_MODEL))),
        out_shape=jax.ShapeDtypeStruct((rows, D_MODEL), F32),
        compiler_params=_params("arbitrary"),
        name="moe_down",
    )(*items, hid, w_down)


def _ple_kernel(pos0_ref, pos1_ref, hbf_ref, h_ref, p_ref, wgate_ref, wple_ref,
                wcol_ref, g_ref, b_ref, ys_ref, o_ref, obf_ref, buf_ref, sems,
                wgate_bf, wple_bf, *, tm):
    base = pl.program_id(0) * tm

    @pl.when(pl.program_id(0) == 0)
    def _():
        wgate_bf[...] = wgate_ref[0].astype(BF16)
        wple_bf[...] = wple_ref[0].astype(BF16)

    def issue(r):
        for k, pos_ref in enumerate((pos0_ref, pos1_ref)):
            pltpu.make_async_copy(ys_ref.at[pl.ds(pos_ref[base + r], 1), :],
                                  buf_ref.at[k, pl.ds(r, 1), :],
                                  sems.at[k]).start(priority=k)

    hbf = hbf_ref[...]
    rows_per_block = tm // GATE_BLOCKS
    width = D_MODEL // GATE_BLOCKS
    gate_parts = []
    for c in range(GATE_BLOCKS):
        gate_parts.append(_dot(hbf, wgate_bf[:, c * width:(c + 1) * width]))
        for r in range(c * rows_per_block, (c + 1) * rows_per_block):
            issue(r)
    gate = _sigmoid(jnp.concatenate(gate_parts, axis=1))
    ple = gate * _dot(p_ref[0, 0].astype(BF16), wple_bf[...])
    for k in range(2):
        pltpu.make_async_copy(ys_ref.at[pl.ds(0, tm), :], buf_ref.at[k], sems.at[k]).wait()
    wc = wcol_ref[...]
    moe = wc[:, 0:1] * buf_ref[0] + wc[:, 1:2] * buf_ref[1]
    out = _layer_norm(DEEPNORM_ALPHA * h_ref[...] + moe + ple, g_ref[...], b_ref[...])
    o_ref[...] = out
    obf_ref[...] = out.astype(BF16)


def _ple_combine_ln(pos0, pos1, h_bf, h, p_all, bi, w_ple_gate, w_ple, layer, wcol, g, b,
                    ys, tm):
    s = h.shape[0]
    kern = functools.partial(_ple_kernel, tm=tm)
    tile = lambda width: pl.BlockSpec((tm, width), lambda i, p0, p1: (i, 0))
    full = lambda shape: pl.BlockSpec(shape, lambda i, p0, p1: (0,) * len(shape))
    p_tile = pl.BlockSpec((1, 1, tm, PLE_DIM), lambda i, p0, p1: (layer, bi, i, 0))
    return pl.pallas_call(
        kern,
        grid_spec=pltpu.PrefetchScalarGridSpec(
            num_scalar_prefetch=2,
            grid=(s // tm,),
            in_specs=[tile(D_MODEL), tile(D_MODEL), p_tile,
                      _resident_weight((D_MODEL, D_MODEL), layer),
                      _resident_weight((PLE_DIM, D_MODEL), layer),
                      tile(LANES), full((1, D_MODEL)), full((1, D_MODEL)),
                      pl.BlockSpec(memory_space=pl.ANY)],
            out_specs=[tile(D_MODEL), tile(D_MODEL)],
            scratch_shapes=[pltpu.VMEM((2, tm, D_MODEL), F32),
                            pltpu.SemaphoreType.DMA((2,)),
                            pltpu.VMEM((D_MODEL, D_MODEL), BF16),
                            pltpu.VMEM((PLE_DIM, D_MODEL), BF16)]),
        out_shape=[jax.ShapeDtypeStruct((s, D_MODEL), F32),
                   jax.ShapeDtypeStruct((s, D_MODEL), BF16)],
        compiler_params=_params("arbitrary"),
        name="ple_combine_ln",
    )(pos0, pos1, h_bf, h, p_all, w_ple_gate, w_ple, wcol, g, b, ys)


def _row(v, width):
    v = v.astype(F32)
    return jnp.concatenate([v, jnp.zeros((width - v.shape[0],), F32)]).reshape(1, width)


def _mixer_layer(h, h_bf, prm, layer, w_router2, bias_col):
    s = h.shape[0]
    w_in = prm["w_in"]
    c_f = 3 * D_ATT
    c_dt = D_MAIN + XBC_TAIL
    pad = lambda cols: jnp.concatenate(
        [cols, jnp.zeros((D_MODEL, LANES - cols.shape[1]), cols.dtype)], axis=1)
    w_small = jnp.concatenate(
        [pad(w_in[layer, :, c_f:c_f + ATT_HEADS]), pad(w_in[layer, :, c_dt:]),
         pad(w_in[layer, :, D_MAIN:c_dt])], axis=1)
    b_small = jnp.concatenate([_row(prm["b_forget"][layer], LANES),
                               _row(prm["dt_bias"][layer], LANES),
                               jnp.zeros((1, LANES), F32)], axis=1)
    a_row = _row(-jnp.exp(prm["a_log"][layer].astype(F32)), LANES)
    dskip_row = jnp.repeat(prm["d_skip"][layer].astype(F32), SSD_HEAD_DIM).reshape(1, D_SSD)

    y = _in_proj(h_bf, w_in, layer, min(s, 1024), 1024)
    faug, dtv, tail = _gates(h_bf, w_small, b_small, min(s, 512))
    att = _attention(y, faug, min(s, 512), min(s, 256), 2)
    ssd = _ssd(y, tail, dtv, prm["conv_w"][layer], _row(prm["conv_b"][layer], D_CONV),
               a_row, dskip_row, _row(prm["ssd_norm_w"][layer], D_SSD))
    return _out_proj_ln_route(att, ssd, prm["w_out"], layer, h,
                              _row(prm["ln1_g"][layer], D_MODEL),
                              _row(prm["ln1_b"][layer], D_MODEL),
                              w_router2, bias_col, min(s, 256))


def _channel_layer(h, h_bf, routing, p_all, layer, bi, prm):
    s = h.shape[0]
    tm = min(s, 256)
    idx, wcol, cnt = routing
    pos0, pos1, items = _plan(idx, cnt, tm)
    xs = _dispatch(pos0, pos1, h, tm)
    ys = _moe(items, xs, prm["w_gate"], prm["w_up"], prm["w_down"], layer, tm)
    return _ple_combine_ln(pos0, pos1, h_bf, h, p_all, bi, prm["w_ple_gate"], prm["w_ple"],
                           layer, wcol, _row(prm["ln2_g"][layer], D_MODEL),
                           _row(prm["ln2_b"][layer], D_MODEL), ys, tm)


def _prep_router(p):
    order = jnp.arange(N_EXPERTS).reshape(N_EXPERT_GROUPS, EXPERTS_PER_GROUP).T.reshape(-1)
    w = p["w_router"].astype(F32)[:, order]
    w = jnp.concatenate([w, jnp.zeros((D_MODEL, LANES - N_EXPERTS), F32)], axis=1)
    w_hi = w.astype(BF16)
    w_lo = (w - w_hi.astype(F32)).astype(BF16)
    bias_col = p["router_bias"].astype(F32)[order].reshape(N_EXPERTS, 1)
    return jnp.stack([w_hi, w_lo]), bias_col


def kernel(x, p, w_in, b_forget, conv_w, conv_b, dt_bias, a_log, d_skip, ssd_norm_w,
           w_out, ln1_g, ln1_b, w_router, router_bias, w_gate, w_up, w_down, w_ple,
           w_ple_gate, ln2_g, ln2_b):
    prm = dict(w_in=w_in, b_forget=b_forget, conv_w=conv_w, conv_b=conv_b,
               dt_bias=dt_bias, a_log=a_log, d_skip=d_skip, ssd_norm_w=ssd_norm_w,
               w_out=w_out, ln1_g=ln1_g, ln1_b=ln1_b, w_router=w_router,
               router_bias=router_bias, w_gate=w_gate, w_up=w_up, w_down=w_down,
               w_ple=w_ple, w_ple_gate=w_ple_gate, ln2_g=ln2_g, ln2_b=ln2_b)
    batch = x.shape[0]
    w_router2, bias_col = _prep_router(prm)
    outs = []
    for bi in range(batch):
        h = x[bi]
        h_bf = h.astype(BF16)
        for layer in range(w_in.shape[0]):
            h, h_bf, *routing = _mixer_layer(h, h_bf, prm, layer, w_router2, bias_col)
            h, h_bf = _channel_layer(h, h_bf, routing, p, layer, bi, prm)
        outs.append(h)
    return jnp.stack(outs)
```

```python
import functools
import math

import jax
import jax.numpy as jnp
from jax import lax
from jax.experimental import pallas as pl
from jax.experimental.pallas import tpu as pltpu

F32 = jnp.float32
BF16 = jnp.bfloat16
I32 = jnp.int32

D_MODEL = 2048
ATT_HEADS = 8
ATT_HEAD_DIM = 128
D_ATT = ATT_HEADS * ATT_HEAD_DIM
SSD_HEADS = 16
SSD_HEAD_DIM = 64
D_SSD = SSD_HEADS * SSD_HEAD_DIM
SSD_GROUPS = 4
HEADS_PER_GROUP = SSD_HEADS // SSD_GROUPS
SSD_STATE = 128
D_GROUP = D_SSD // SSD_GROUPS
CONV_WIDTH = 4
D_BC = SSD_GROUPS * SSD_STATE
D_CONV = D_SSD + 2 * D_BC
N_EXPERTS = 16
N_EXPERT_GROUPS = 4
EXPERTS_PER_GROUP = N_EXPERTS // N_EXPERT_GROUPS
D_EXPERT = 1024
PLE_DIM = 256
DEPTH = 2
DEEPNORM_ALPHA = (2 * DEPTH) ** 0.25
LN_EPS = 1e-5
RMS_EPS = 1e-5
LOG2_E = math.log2(math.e)

LANES = 128
SUBLANES = 8
VMEM_LIMIT_BYTES = 56 * 1024 * 1024

D_MAIN = 3 * D_ATT + D_SSD + D_CONV
COL_WIN = 3 * D_ATT
D_WIN = D_MAIN - COL_WIN
XBC_TAIL = ATT_HEADS

SSD_CHUNK = 128
CONV_HALO = SUBLANES
SUM_ROWS = 2 * SUBLANES


def _params(*sem):
    return pltpu.CompilerParams(dimension_semantics=sem,
                                vmem_limit_bytes=VMEM_LIMIT_BYTES)


def _softplus(u):
    return jnp.maximum(u, 0.0) + jnp.log1p(jnp.exp(-jnp.abs(u)))


def _sigmoid(u):
    return 1.0 / (1.0 + jnp.exp(-u))


def _silu(u):
    return u * _sigmoid(u)


def _split3(v):
    hi = v.astype(BF16)
    r1 = v - hi.astype(F32)
    mid = r1.astype(BF16)
    lo = (r1 - mid.astype(F32)).astype(BF16)
    return hi, mid, lo


def _dot(a, b):
    return jnp.dot(a, b, preferred_element_type=F32)


def _exact_left_dot(ones_bf16, v):
    hi, mid, lo = _split3(v)
    return _dot(ones_bf16, hi) + (_dot(ones_bf16, mid) + _dot(ones_bf16, lo))


def _proj_kernel(x_ref, w_ref, o_ref, w_bf, *, q_blocks):
    @pl.when(pl.program_id(1) == 0)
    def _():
        w_bf[...] = w_ref[0].astype(BF16)

    scale = jnp.where(pl.program_id(0) < q_blocks, LOG2_E * ATT_HEAD_DIM ** -0.5, 1.0)
    o_ref[...] = (_dot(x_ref[...], w_bf[...]) * scale).astype(o_ref.dtype)


def _in_proj(h_bf, w_in, layer, tm, tn):
    s = h_bf.shape[0]
    return pl.pallas_call(
        functools.partial(_proj_kernel, q_blocks=D_ATT // tn),
        grid=(D_MAIN // tn, s // tm),
        in_specs=[pl.BlockSpec((tm, D_MODEL), lambda j, i: (i, 0)),
                  pl.BlockSpec((1, D_MODEL, tn), lambda j, i: (layer, 0, j))],
        out_specs=pl.BlockSpec((tm, tn), lambda j, i: (i, j)),
        out_shape=jax.ShapeDtypeStruct((s, D_MAIN), BF16),
        scratch_shapes=[pltpu.VMEM((D_MODEL, tn), BF16)],
        compiler_params=_params("arbitrary", "arbitrary"),
        name="in_proj",
    )(h_bf, w_in)


def _gates_kernel(x_ref, w_ref, b_ref, tri_ref, sel_ref, ones_ref, faug_ref, dt_ref,
                  tail_ref, carry_ref, w_bf):
    @pl.when(pl.program_id(0) == 0)
    def _():
        carry_ref[...] = jnp.zeros_like(carry_ref)
        w_bf[...] = w_ref[...].astype(BF16)

    v = _dot(x_ref[...], w_bf[...]) + b_ref[...]
    log_f = -_softplus(-v[:, :LANES])
    dt_ref[...] = _softplus(v[:, LANES:2 * LANES])
    tail_ref[...] = v[:, 2 * LANES:]
    fcum = _exact_left_dot(tri_ref[...], log_f) + carry_ref[...]
    carry_ref[...] = fcum[-1:, :]
    hi, mid, lo = _split3(fcum * LOG2_E)
    aug = (_dot(hi, sel_ref[0]) + _dot(mid, sel_ref[1]) + _dot(lo, sel_ref[2])
           + ones_ref[...])
    faug_ref[...] = aug.astype(BF16)


def _forget_layout():
    import numpy as np
    sel = np.zeros((3, LANES, 2 * D_ATT), np.float32)
    ones = np.zeros((1, 2 * D_ATT), np.float32)
    for h in range(ATT_HEADS):
        for k in range(3):
            sel[k, h, ATT_HEAD_DIM * h + k] = 1.0
            sel[k, h, D_ATT + ATT_HEAD_DIM * h + 3 + k] = -1.0
            ones[0, ATT_HEAD_DIM * h + 3 + k] = 1.0
            ones[0, D_ATT + ATT_HEAD_DIM * h + k] = 1.0
    return jnp.asarray(sel, BF16), jnp.asarray(ones, F32)


def _gates(h_bf, w_small, b_small, tm):
    s = h_bf.shape[0]
    tri = jnp.tril(jnp.ones((tm, tm), F32)).astype(BF16)
    sel, ones = _forget_layout()
    return pl.pallas_call(
        _gates_kernel,
        grid=(s // tm,),
        in_specs=[pl.BlockSpec((tm, D_MODEL), lambda i: (i, 0)),
                  pl.BlockSpec((D_MODEL, 3 * LANES), lambda i: (0, 0)),
                  pl.BlockSpec((1, 3 * LANES), lambda i: (0, 0)),
                  pl.BlockSpec((tm, tm), lambda i: (0, 0)),
                  pl.BlockSpec((3, LANES, 2 * D_ATT), lambda i: (0, 0, 0)),
                  pl.BlockSpec((1, 2 * D_ATT), lambda i: (0, 0))],
        out_specs=[pl.BlockSpec((tm, 2 * D_ATT), lambda i: (i, 0)),
                   pl.BlockSpec((tm, LANES), lambda i: (i, 0)),
                   pl.BlockSpec((tm, LANES), lambda i: (i, 0))],
        out_shape=[jax.ShapeDtypeStruct((s, 2 * D_ATT), BF16),
                   jax.ShapeDtypeStruct((s, LANES), F32),
                   jax.ShapeDtypeStruct((s, LANES), F32)],
        scratch_shapes=[pltpu.VMEM((1, LANES), F32),
                        pltpu.VMEM((D_MODEL, 3 * LANES), BF16)],
        compiler_params=_params("arbitrary"),
        name="gates",
    )(h_bf, w_small, b_small, tri, sel, ones)


def _attn_kernel(q_ref, fq_ref, k_ref, fk_ref, v_ref, o_ref, vt_ref, sa_ref, sb_ref,
                 *, tq, tk, nh):
    assert tq == 2 * tk
    i = pl.program_id(1)
    n_chunks = vt_ref.shape[1]
    dh = ATT_HEAD_DIM
    heads = range(nh)

    @pl.when(i == 0)
    def _():
        for hd in heads:
            for c in range(n_chunks):
                blk = v_ref[c * tk:(c + 1) * tk, hd * dh:(hd + 1) * dh]
                vt_ref[hd, c, :dh, :] = blk.astype(F32).T.astype(BF16)
                vt_ref[hd, c, dh:, :] = jnp.ones((SUM_ROWS, tk), BF16)

    q_t = []
    for hd in heads:
        cols = slice(hd * dh, (hd + 1) * dh)
        q_aug = jnp.concatenate([q_ref[:, cols], fq_ref[:, cols]], axis=1)
        q_t.append(q_aug.astype(F32).T.astype(BF16))
    kv_pos = lax.broadcasted_iota(I32, (tk, tq), 0)
    q_pos = lax.broadcasted_iota(I32, (tk, tq), 1)

    def logits(hd, c):
        start = pl.multiple_of(c * tk, tk)
        cols = slice(hd * dh, (hd + 1) * dh)
        k_aug = jnp.concatenate([k_ref[pl.ds(start, tk), cols],
                                 fk_ref[pl.ds(start, tk), cols]], axis=1)
        return _dot(k_aug, q_t[hd])

    def update(st, hd, c, carry, diag_offset):
        m, acc = carry
        if diag_offset is not None:
            st = jnp.where(kv_pos + diag_offset <= q_pos, st, -jnp.inf)
        m_new = jnp.maximum(m, jnp.max(st, axis=0, keepdims=True))
        p = jnp.exp2(st - m_new)
        alpha = jnp.exp2(m - m_new)
        acc = alpha * acc + _dot(vt_ref[hd, c], p.astype(BF16))
        return m_new, acc

    first = 2 * i
    for hd in heads:
        sa_ref[hd] = logits(hd, 0)

    def pair(t, carries):
        out = []
        for hd in heads:
            sb_ref[hd] = logits(hd, 2 * t + 1)
        for hd in heads:
            out.append(update(sa_ref[hd], hd, 2 * t, carries[hd], None))
        for hd in heads:
            sa_ref[hd] = logits(hd, 2 * t + 2)
        return tuple(update(sb_ref[hd], hd, 2 * t + 1, out[hd], None) for hd in heads)

    init = (jnp.full((1, tq), -jnp.inf, F32), jnp.zeros((dh + SUM_ROWS, tq), F32))
    carries = lax.fori_loop(0, i, pair, tuple(init for _ in heads))
    for hd in heads:
        sb_ref[hd] = logits(hd, first + 1)
    for hd in heads:
        carry = update(sa_ref[hd], hd, first, carries[hd], 0)
        _, acc = update(sb_ref[hd], hd, first + 1, carry, tk)
        out = acc[:dh, :] / acc[dh:dh + 1, :]
        o_ref[:, hd * dh:(hd + 1) * dh] = out.T.astype(o_ref.dtype)


def _attention(y, faug, tq, tk, nh):
    s = y.shape[0]
    width = nh * ATT_HEAD_DIM
    groups = ATT_HEADS // nh
    kern = functools.partial(_attn_kernel, tq=tq, tk=tk, nh=nh)
    q_block = lambda g0: pl.BlockSpec((tq, width), lambda g, i: (i, g0 + g))
    kv_block = lambda g0: pl.BlockSpec((s, width), lambda g, i: (0, g0 + g))
    return pl.pallas_call(
        kern,
        grid=(groups, s // tq),
        in_specs=[q_block(0), q_block(0), kv_block(groups), kv_block(groups),
                  kv_block(2 * groups)],
        out_specs=pl.BlockSpec((tq, width), lambda g, i: (i, g)),
        out_shape=jax.ShapeDtypeStruct((s, D_ATT), BF16),
        scratch_shapes=[pltpu.VMEM((nh, s // tk, ATT_HEAD_DIM + SUM_ROWS, tk), BF16),
                        pltpu.VMEM((nh, tk, tq), F32), pltpu.VMEM((nh, tk, tq), F32)],
        compiler_params=_params("arbitrary", "arbitrary"),
        name="fox_attention",
    )(y, faug, y, faug, y)


def _expand_heads(w, n_heads):
    length = w.shape[0]
    lane = lax.broadcasted_iota(I32, (length, LANES), 1)
    blocks = []
    for m in range(n_heads // 2):
        blocks.append(jnp.where(lane < SSD_HEAD_DIM, w[:, 2 * m:2 * m + 1],
                                w[:, 2 * m + 1:2 * m + 2]))
    return jnp.concatenate(blocks, axis=1)


def _ssd_kernel(win_ref, tail_ref, dt_ref, convw_ref, convb_ref, a_ref, dskip_ref,
                normw_ref, tri_ref, o_ref, ubuf_ref, state_ref):
    length = SSD_CHUNK

    @pl.when(pl.program_id(0) == 0)
    def _():
        ubuf_ref[0:CONV_HALO, :] = jnp.zeros((CONV_HALO, D_CONV), F32)
        state_ref[...] = jnp.zeros_like(state_ref)

    win = jnp.concatenate([win_ref[...].astype(F32), tail_ref[...]], axis=1)
    win = jnp.concatenate([win[:, ATT_HEADS:], win[:, :ATT_HEADS]], axis=1)
    z = win[:, :D_SSD]

    ubuf_ref[CONV_HALO:CONV_HALO + length, :] = win[:, D_SSD:D_SSD + D_CONV]
    conv = convb_ref[...]
    for k in range(CONV_WIDTH):
        off = CONV_HALO - (CONV_WIDTH - 1) + k
        conv = conv + convw_ref[k:k + 1, :] * ubuf_ref[off:off + length, :]
    tail = ubuf_ref[length:length + CONV_HALO, :]
    ubuf_ref[0:CONV_HALO, :] = tail
    xc = _silu(conv)
    xs = xc[:, :D_SSD]

    dt = dt_ref[...]
    da = dt * a_ref[...]
    acum = _exact_left_dot(tri_ref[...], da)
    acum_t = acum.T
    a_last = acum[length - 1:length, :]
    w_off = _expand_heads(jnp.exp(acum), SSD_HEADS)
    w_state = _expand_heads(jnp.exp(a_last - acum) * dt, SSD_HEADS)
    w_dt = _expand_heads(dt, SSD_HEADS)
    chunk_decay = _expand_heads(jnp.exp(a_last), SSD_HEADS)

    x_dt = (xs * w_dt).astype(BF16)
    x_state = (xs * w_state).astype(BF16)
    row = lax.broadcasted_iota(I32, (length, length), 0)
    col = lax.broadcasted_iota(I32, (length, length), 1)
    causal = col <= row
    grp_lane = lax.broadcasted_iota(I32, (length, D_GROUP), 1) // SSD_HEAD_DIM

    y_parts = []
    for g in range(SSD_GROUPS):
        b_g = xc[:, D_SSD + g * SSD_STATE:D_SSD + (g + 1) * SSD_STATE].astype(BF16)
        c_g = xc[:, D_SSD + D_BC + g * SSD_STATE:
                 D_SSD + D_BC + (g + 1) * SSD_STATE].astype(BF16)
        cb = lax.dot_general(c_g, b_g, (((1,), (1,)), ((), ())),
                             preferred_element_type=F32)
        cols = slice(g * D_GROUP, (g + 1) * D_GROUP)
        x_dt_g = x_dt[:, cols]
        m_blocks, x_blocks = [], []
        for r in range(HEADS_PER_GROUP):
            hd = g * HEADS_PER_GROUP + r
            seg = acum[:, hd:hd + 1] - acum_t[hd:hd + 1, :]
            decay = jnp.exp(jnp.where(causal, seg, -jnp.inf))
            m_blocks.append((cb * decay).astype(BF16))
            x_blocks.append(jnp.where(grp_lane == r, x_dt_g, jnp.zeros_like(x_dt_g)))
        y_diag = _dot(jnp.concatenate(m_blocks, axis=1),
                      jnp.concatenate(x_blocks, axis=0))
        st = state_ref[g]
        y_off = _dot(c_g, st.astype(BF16)) * w_off[:, cols]
        y_parts.append(y_diag + y_off)
        new_st = lax.dot_general(b_g, x_state[:, cols], (((0,), (0,)), ((), ())),
                                 preferred_element_type=F32)
        state_ref[g] = st * chunk_decay[:, cols] + new_st

    y = jnp.concatenate(y_parts, axis=1) + dskip_ref[...] * xs
    yg = y * _silu(z)
    outs = []
    for g in range(SSD_GROUPS):
        blk = yg[:, g * D_GROUP:(g + 1) * D_GROUP]
        ms = jnp.mean(blk * blk, axis=1, keepdims=True)
        outs.append(blk * lax.rsqrt(ms + RMS_EPS))
    o_ref[...] = (jnp.concatenate(outs, axis=1) * normw_ref[...]).astype(o_ref.dtype)


def _ssd(y, tail, dtv, conv_w, conv_b, a_row, dskip_row, normw_row):
    s = y.shape[0]
    length = SSD_CHUNK
    tri = jnp.tril(jnp.ones((length, length), F32)).astype(BF16)
    full = lambda shape: pl.BlockSpec(shape, lambda c: (0,) * len(shape))
    return pl.pallas_call(
        _ssd_kernel,
        grid=(s // length,),
        in_specs=[pl.BlockSpec((length, D_WIN), lambda c: (c, COL_WIN // D_WIN)),
                  pl.BlockSpec((length, LANES), lambda c: (c, 0)),
                  pl.BlockSpec((length, LANES), lambda c: (c, 0)),
                  full((CONV_WIDTH, D_CONV)), full((1, D_CONV)), full((1, LANES)),
                  full((1, D_SSD)), full((1, D_SSD)), full((length, length))],
        out_specs=pl.BlockSpec((length, D_SSD), lambda c: (c, 0)),
        out_shape=jax.ShapeDtypeStruct((s, D_SSD), BF16),
        scratch_shapes=[pltpu.VMEM((length + CONV_HALO, D_CONV), F32),
                        pltpu.VMEM((SSD_GROUPS, SSD_STATE, D_GROUP), F32)],
        compiler_params=_params("arbitrary"),
        name="ssd",
    )(y, tail, dtv, conv_w, conv_b, a_row, dskip_row, normw_row, tri)


def _layer_norm(xf, g, b):
    mu = jnp.mean(xf, axis=1, keepdims=True)
    xc = xf - mu
    var = jnp.mean(xc * xc, axis=1, keepdims=True)
    return xc * lax.rsqrt(var + LN_EPS) * g + b


def _resident_weight(shape, layer):
    return pl.BlockSpec((1,) + shape, lambda i, *_: (layer,) + (0,) * len(shape),
                        pipeline_mode=pl.Buffered(1))


def _out_proj_kernel(att_ref, ssd_ref, w_ref, h_ref, g_ref, b_ref, wr_ref, rbias_ref,
                     triu_ref, o_ref, obf_ref, idx_ref, wcol_ref, cnt_ref, w_bf,
                     carry_ref):
    @pl.when(pl.program_id(0) == 0)
    def _():
        w_bf[...] = w_ref[0].astype(BF16)

    mix = _dot(att_ref[...], w_bf[:D_ATT, :]) + _dot(ssd_ref[...], w_bf[D_ATT:, :])
    out = _layer_norm(DEEPNORM_ALPHA * h_ref[...] + mix, g_ref[...], b_ref[...])
    o_ref[...] = out
    obf_ref[...] = out.astype(BF16)
    _route(out, wr_ref, rbias_ref, triu_ref, idx_ref, wcol_ref, cnt_ref, carry_ref)


def _out_proj_ln_route(att, ssd, w_out, layer, h, g, b, w_router2, bias_col, tm):
    s = h.shape[0]
    full = lambda shape: pl.BlockSpec(shape, lambda i: (0,) * len(shape))
    row_tile = lambda width: pl.BlockSpec((tm, width), lambda i: (i, 0))
    triu = jnp.triu(jnp.ones((tm, tm), F32)).astype(BF16)
    return pl.pallas_call(
        _out_proj_kernel,
        grid=(s // tm,),
        in_specs=[row_tile(D_ATT), row_tile(D_SSD),
                  _resident_weight((D_MODEL, D_MODEL), layer),
                  row_tile(D_MODEL), full((1, D_MODEL)), full((1, D_MODEL)),
                  full((2, D_MODEL, LANES)), full((N_EXPERTS, 1)), full((tm, tm))],
        out_specs=[row_tile(D_MODEL), row_tile(D_MODEL),
                   pl.BlockSpec((SUBLANES, tm), lambda i: (0, i)),
                   row_tile(LANES), full((N_EXPERTS, LANES))],
        out_shape=[jax.ShapeDtypeStruct((s, D_MODEL), F32),
                   jax.ShapeDtypeStruct((s, D_MODEL), BF16),
                   jax.ShapeDtypeStruct((SUBLANES, s), I32),
                   jax.ShapeDtypeStruct((s, LANES), F32),
                   jax.ShapeDtypeStruct((N_EXPERTS, LANES), F32)],
        scratch_shapes=[pltpu.VMEM((D_MODEL, D_MODEL), BF16),
                        pltpu.VMEM((N_EXPERTS, LANES), F32)],
        compiler_params=_params("arbitrary"),
        name="out_proj_ln_route",
    )(att, ssd, w_out, h, g, b, w_router2, bias_col, triu)


def _route(h, wr_ref, bias_ref, triu_ref, idx_ref, wcol_ref, cnt_ref, carry_ref):
    tm = h.shape[0]

    @pl.when(pl.program_id(0) == 0)
    def _():
        carry_ref[...] = jnp.zeros_like(carry_ref)

    h_hi = h.astype(BF16)
    h_lo = (h - h_hi.astype(F32)).astype(BF16)
    logits = _dot(h_hi, wr_ref[0]) + (_dot(h_hi, wr_ref[1]) + _dot(h_lo, wr_ref[0]))
    lt = logits.T
    scores = _sigmoid(lt[:N_EXPERTS, :])
    biased = scores + bias_ref[...]
    npg = EXPERTS_PER_GROUP
    s_r = [scores[npg * r:npg * (r + 1), :] for r in range(npg)]
    b_r = [biased[npg * r:npg * (r + 1), :] for r in range(npg)]

    gs = None
    for a in range(npg):
        for b in range(a + 1, npg):
            pair = b_r[a] + b_r[b]
            gs = pair if gs is None else jnp.maximum(gs, pair)
    best = jnp.zeros((1, tm), I32)
    top = gs[0:1, :]
    for g in range(1, N_EXPERT_GROUPS):
        better = gs[g:g + 1, :] > top
        best = jnp.where(better, g, best)
        top = jnp.where(better, gs[g:g + 1, :], top)

    def pick_group(a):
        out = a[0:1, :]
        for g in range(1, N_EXPERT_GROUPS):
            out = jnp.where(best == g, a[g:g + 1, :], out)
        return out

    sb = [pick_group(b) for b in b_r]
    ss = [pick_group(s) for s in s_r]
    first = jnp.zeros((1, tm), I32)
    fmax = sb[0]
    for r in range(1, npg):
        better = sb[r] > fmax
        first = jnp.where(better, r, first)
        fmax = jnp.where(better, sb[r], fmax)
    second = jnp.zeros((1, tm), I32)
    smax = jnp.full((1, tm), -jnp.inf, F32)
    for r in range(npg):
        cand = jnp.where(first == r, -jnp.inf, sb[r])
        better = cand > smax
        second = jnp.where(better, r, second)
        smax = jnp.where(better, cand, smax)

    def pick_expert(which):
        out = ss[0]
        for r in range(1, npg):
            out = jnp.where(which == r, ss[r], out)
        return out

    a0 = pick_expert(first)
    a1 = pick_expert(second)
    denom = a0 + a1
    e0 = best * npg + first
    e1 = best * npg + second

    erow = lax.broadcasted_iota(I32, (N_EXPERTS, tm), 0)
    hit0 = erow == e0
    hit1 = erow == e1
    onehot = jnp.where(hit0, 1.0, 0.0) + jnp.where(hit1, 1.0, 0.0)
    cum = _dot(onehot.astype(BF16), triu_ref[...])
    carry = carry_ref[...][:, 0:1]
    before = cum - onehot + carry
    rank0 = jnp.sum(jnp.where(hit0, before, 0.0), axis=0, keepdims=True)
    rank1 = jnp.sum(jnp.where(hit1, before, 0.0), axis=0, keepdims=True)
    new_carry = carry_ref[...] + cum[:, tm - 1:tm]
    carry_ref[...] = new_carry
    cnt_ref[...] = new_carry

    zeros_i = jnp.zeros((SUBLANES - 4, tm), I32)
    idx_ref[...] = jnp.concatenate(
        [e0, e1, rank0.astype(I32), rank1.astype(I32), zeros_i], axis=0)
    wrows = jnp.concatenate([a0 / denom, a1 / denom,
                             jnp.zeros((LANES - 2, tm), F32)], axis=0)
    wcol_ref[...] = wrows.T


def _plan_kernel(idx_ref, cnt_ref, pos_ref, items_ref, *, tm):
    ne = N_EXPERTS
    counts = cnt_ref[...][:, 0:1]

    def across(col):
        wide = jnp.concatenate([jnp.broadcast_to(col, (ne, LANES)),
                                jnp.zeros((LANES - ne, LANES), F32)], axis=0)
        return wide.T[:ne, :ne]

    r_id = lax.broadcasted_iota(I32, (ne, ne), 0)
    c_id = lax.broadcasted_iota(I32, (ne, ne), 1)

    def running_sum(col):
        return jnp.sum(jnp.where(c_id <= r_id, across(col), 0.0), axis=1, keepdims=True)

    ends = running_sum(counts)
    starts = ends - counts
    used = jnp.where(counts > 0, 1.0, 0.0)
    first_tile = jnp.floor(starts / tm)
    last_tile = jnp.floor((ends - 1.0) / tm)
    n_e = used * (last_tile - first_tile + 1.0)
    item_end = running_sum(n_e)
    item_start = item_end - n_e
    total = item_end[ne - 1:ne, :]
    later_used = jnp.logical_and(c_id > r_id, across(used) > 0)
    next_e = jnp.min(jnp.where(later_used, c_id.astype(F32), float(ne)), axis=1,
                     keepdims=True)
    next_e = jnp.where(next_e == ne, -1.0, next_e)
    ordinal = running_sum(used) - 1.0
    slot_e = ordinal - 2.0 * jnp.floor(ordinal / 2.0)

    w = lax.broadcasted_iota(I32, (1, LANES), 1).astype(F32)
    valid = w < total
    wc = jnp.minimum(w, total - 1.0)
    e_w = jnp.minimum(jnp.sum(jnp.where(item_end <= wc, 1.0, 0.0), axis=0, keepdims=True),
                      ne - 1.0)
    e_id = lax.broadcasted_iota(I32, (ne, LANES), 0).astype(F32)
    mine = e_id == e_w
    pick = lambda col: jnp.sum(jnp.where(mine, col, 0.0), axis=0, keepdims=True)
    tile_w = pick(first_tile) + (wc - pick(item_start))
    lo = jnp.where(valid, jnp.clip(pick(starts) - tile_w * tm, 0.0, tm), 0.0)
    hi = jnp.where(valid, jnp.clip(pick(ends) - tile_w * tm, 0.0, tm), 0.0)
    rows = [tile_w, e_w, lo, hi, pick(slot_e), pick(next_e)]
    rows += [jnp.zeros((1, LANES), F32)] * (SUBLANES - len(rows))
    items_ref[...] = jnp.concatenate(rows, axis=0).astype(I32)

    s = idx_ref.shape[1]
    tok_e = lax.broadcasted_iota(I32, (ne, s), 0)
    starts_i = starts.astype(I32)
    pos = []
    for k in range(2):
        base = jnp.sum(jnp.where(tok_e == idx_ref[k:k + 1, :], starts_i, 0), axis=0,
                       keepdims=True)
        pos.append(base + idx_ref[2 + k:3 + k, :])
    pos += [jnp.zeros((1, s), I32)] * (SUBLANES - len(pos))
    pos_ref[...] = jnp.concatenate(pos, axis=0)


def _plan(idx, cnt, tm):
    s = idx.shape[1]
    n_items = 2 * s // tm + N_EXPERTS - 1
    assert n_items <= LANES
    pos, items = pl.pallas_call(
        functools.partial(_plan_kernel, tm=tm),
        out_shape=[jax.ShapeDtypeStruct((SUBLANES, s), I32),
                   jax.ShapeDtypeStruct((SUBLANES, LANES), I32)],
        compiler_params=pltpu.CompilerParams(vmem_limit_bytes=VMEM_LIMIT_BYTES),
        name="plan",
    )(idx, cnt)
    return pos[0], pos[1], tuple(items[k, :n_items] for k in range(6))


def _dispatch_kernel(pos0_ref, pos1_ref, h_ref, xs_ref, sems, *, tm):
    base = pl.program_id(0) * tm

    def issue(r, carry):
        src = h_ref.at[pl.ds(r, 1), :]
        pltpu.make_async_copy(src, xs_ref.at[pl.ds(pos0_ref[base + r], 1), :],
                              sems.at[0]).start(priority=0)
        pltpu.make_async_copy(src, xs_ref.at[pl.ds(pos1_ref[base + r], 1), :],
                              sems.at[1]).start(priority=1)
        return carry

    lax.fori_loop(0, tm, issue, 0, unroll=8)
    for k in range(2):
        pltpu.make_async_copy(h_ref, xs_ref.at[pl.ds(0, tm), :], sems.at[k]).wait()


def _dispatch(pos0, pos1, h, tm):
    s = h.shape[0]
    kern = functools.partial(_dispatch_kernel, tm=tm)
    return pl.pallas_call(
        kern,
        grid_spec=pltpu.PrefetchScalarGridSpec(
            num_scalar_prefetch=2,
            grid=(s // tm,),
            in_specs=[pl.BlockSpec((tm, D_MODEL), lambda i, p0, p1: (i, 0))],
            out_specs=pl.BlockSpec(memory_space=pl.ANY),
            scratch_shapes=[pltpu.SemaphoreType.DMA((2,))]),
        out_shape=jax.ShapeDtypeStruct((2 * s, D_MODEL), F32),
        compiler_params=_params("arbitrary"),
        name="dispatch",
    )(pos0, pos1, h)


def _item_flags(tile_ref, exp_ref):
    w = pl.program_id(0)
    prev = jnp.maximum(w - 1, 0)
    fresh_tile = jnp.logical_or(w == 0, tile_ref[w] != tile_ref[prev])
    new_expert = jnp.logical_or(w == 0, exp_ref[w] != exp_ref[prev])
    return fresh_tile, new_expert


def _store_rows(o_ref, val, lo, hi, fresh_tile):
    row = lax.broadcasted_iota(I32, (val.shape[0], 1), 0)
    keep = jnp.logical_and(row >= lo, row < hi)

    @pl.when(fresh_tile)
    def _():
        o_ref[...] = jnp.where(keep, val, jnp.zeros_like(val))

    @pl.when(jnp.logical_not(fresh_tile))
    def _():
        o_ref[...] = jnp.where(keep, val, o_ref[...])


def _expert_weights(exp_ref, slot_ref, next_ref, new_expert, layer, w_hbm, stage_ref,
                    w_bf, sems):
    w = pl.program_id(0)
    n_mats = len(w_hbm)

    def fetch(expert, slot):
        return [pltpu.make_async_copy(w_hbm[m].at[layer, expert], stage_ref.at[slot, m],
                                      sems.at[slot, m]) for m in range(n_mats)]

    @pl.when(w == 0)
    def _():
        for copy in fetch(exp_ref[0], slot_ref[0]):
            copy.start(priority=1)

    @pl.when(new_expert)
    def _():
        slot = slot_ref[w]
        for m, copy in enumerate(fetch(exp_ref[w], slot)):
            copy.wait()
            w_bf[m] = stage_ref[slot, m].astype(BF16)

        @pl.when(next_ref[w] >= 0)
        def _():
            for copy in fetch(next_ref[w], 1 - slot):
                copy.start(priority=1)


def _moe_up_kernel(tile_ref, exp_ref, lo_ref, hi_ref, slot_ref, next_ref, x_ref, wg_hbm,
                   wu_hbm, o_ref, stage_ref, w_bf, sems, *, layer):
    w = pl.program_id(0)
    fresh_tile, new_expert = _item_flags(tile_ref, exp_ref)
    _expert_weights(exp_ref, slot_ref, next_ref, new_expert, layer, (wg_hbm, wu_hbm),
                    stage_ref, w_bf, sems)

    @pl.when(hi_ref[w] > lo_ref[w])
    def _():
        x = x_ref[...].astype(BF16)
        hid = _silu(_dot(x, w_bf[0])) * _dot(x, w_bf[1])
        _store_rows(o_ref, hid.astype(BF16), lo_ref[w], hi_ref[w], fresh_tile)


def _moe_down_kernel(tile_ref, exp_ref, lo_ref, hi_ref, slot_ref, next_ref, h_ref, wd_hbm,
                     o_ref, stage_ref, w_bf, sems, *, layer):
    w = pl.program_id(0)
    fresh_tile, new_expert = _item_flags(tile_ref, exp_ref)
    _expert_weights(exp_ref, slot_ref, next_ref, new_expert, layer, (wd_hbm,),
                    stage_ref, w_bf, sems)

    @pl.when(hi_ref[w] > lo_ref[w])
    def _():
        y = _dot(h_ref[...], w_bf[0])
        _store_rows(o_ref, y, lo_ref[w], hi_ref[w], fresh_tile)


def _moe(items, xs, w_gate, w_up, w_down, layer, tm):
    rows = xs.shape[0]
    n_items = items[0].shape[0]
    row_tile = lambda width: pl.BlockSpec((tm, width), lambda w, tile, *_: (tile[w], 0))
    in_hbm = pl.BlockSpec(memory_space=pl.ANY)

    def weight_scratch(n_mats, shape):
        return [pltpu.VMEM((2, n_mats) + shape, F32), pltpu.VMEM((n_mats,) + shape, BF16),
                pltpu.SemaphoreType.DMA((2, n_mats))]

    hid = pl.pallas_call(
        functools.partial(_moe_up_kernel, layer=layer),
        grid_spec=pltpu.PrefetchScalarGridSpec(
            num_scalar_prefetch=len(items),
            grid=(n_items,),
            in_specs=[row_tile(D_MODEL), in_hbm, in_hbm],
            out_specs=row_tile(D_EXPERT),
            scratch_shapes=weight_scratch(2, (D_MODEL, D_EXPERT))),
        out_shape=jax.ShapeDtypeStruct((rows, D_EXPERT), BF16),
        compiler_params=_params("arbitrary"),
        name="moe_up",
    )(*items, xs, w_gate, w_up)
    return pl.pallas_call(
        functools.partial(_moe_down_kernel, layer=layer),
        grid_spec=pltpu.PrefetchScalarGridSpec(
            num_scalar_prefetch=len(items),
            grid=(n_items,),
            in_specs=[row_tile(D_EXPERT), in_hbm],
            out_specs=row_tile(D_MODEL),
            scratch_shapes=weight_scratch(1, (D_EXPERT, D_MODEL))),
        out_shape=jax.ShapeDtypeStruct((rows, D_MODEL), F32),
        compiler_params=_params("arbitrary"),
        name="moe_down",
    )(*items, hid, w_down)


def _ple_kernel(pos0_ref, pos1_ref, hbf_ref, h_ref, p_ref, wgate_ref, wple_ref,
                wcol_ref, g_ref, b_ref, ys_ref, o_ref, obf_ref, buf_ref, sems,
                wgate_bf, wple_bf, *, tm):
    base = pl.program_id(0) * tm

    @pl.when(pl.program_id(0) == 0)
    def _():
        wgate_bf[...] = wgate_ref[0].astype(BF16)
        wple_bf[...] = wple_ref[0].astype(BF16)

    def issue(r, carry):
        for k, pos_ref in enumerate((pos0_ref, pos1_ref)):
            pltpu.make_async_copy(ys_ref.at[pl.ds(pos_ref[base + r], 1), :],
                                  buf_ref.at[k, pl.ds(r, 1), :],
                                  sems.at[k]).start(priority=k)
        return carry

    lax.fori_loop(0, tm, issue, 0, unroll=8)
    gate = _sigmoid(_dot(hbf_ref[...], wgate_bf[...]))
    ple = gate * _dot(p_ref[0, 0].astype(BF16), wple_bf[...])
    for k in range(2):
        pltpu.make_async_copy(ys_ref.at[pl.ds(0, tm), :], buf_ref.at[k], sems.at[k]).wait()
    wc = wcol_ref[...]
    moe = wc[:, 0:1] * buf_ref[0] + wc[:, 1:2] * buf_ref[1]
    out = _layer_norm(DEEPNORM_ALPHA * h_ref[...] + moe + ple, g_ref[...], b_ref[...])
    o_ref[...] = out
    obf_ref[...] = out.astype(BF16)


def _ple_combine_ln(pos0, pos1, h_bf, h, p_all, bi, w_ple_gate, w_ple, layer, wcol, g, b,
                    ys, tm):
    s = h.shape[0]
    kern = functools.partial(_ple_kernel, tm=tm)
    tile = lambda width: pl.BlockSpec((tm, width), lambda i, p0, p1: (i, 0))
    full = lambda shape: pl.BlockSpec(shape, lambda i, p0, p1: (0,) * len(shape))
    p_tile = pl.BlockSpec((1, 1, tm, PLE_DIM), lambda i, p0, p1: (layer, bi, i, 0))
    return pl.pallas_call(
        kern,
        grid_spec=pltpu.PrefetchScalarGridSpec(
            num_scalar_prefetch=2,
            grid=(s // tm,),
            in_specs=[tile(D_MODEL), tile(D_MODEL), p_tile,
                      _resident_weight((D_MODEL, D_MODEL), layer),
                      _resident_weight((PLE_DIM, D_MODEL), layer),
                      tile(LANES), full((1, D_MODEL)), full((1, D_MODEL)),
                      pl.BlockSpec(memory_space=pl.ANY)],
            out_specs=[tile(D_MODEL), tile(D_MODEL)],
            scratch_shapes=[pltpu.VMEM((2, tm, D_MODEL), F32),
                            pltpu.SemaphoreType.DMA((2,)),
                            pltpu.VMEM((D_MODEL, D_MODEL), BF16),
                            pltpu.VMEM((PLE_DIM, D_MODEL), BF16)]),
        out_shape=[jax.ShapeDtypeStruct((s, D_MODEL), F32),
                   jax.ShapeDtypeStruct((s, D_MODEL), BF16)],
        compiler_params=_params("arbitrary"),
        name="ple_combine_ln",
    )(pos0, pos1, h_bf, h, p_all, w_ple_gate, w_ple, wcol, g, b, ys)


def _row(v, width):
    v = v.astype(F32)
    return jnp.concatenate([v, jnp.zeros((width - v.shape[0],), F32)]).reshape(1, width)


def _mixer_layer(h, h_bf, prm, layer, w_router2, bias_col):
    s = h.shape[0]
    w_in = prm["w_in"]
    c_f = 3 * D_ATT
    c_dt = D_MAIN + XBC_TAIL
    pad = lambda cols: jnp.concatenate(
        [cols, jnp.zeros((D_MODEL, LANES - cols.shape[1]), cols.dtype)], axis=1)
    w_small = jnp.concatenate(
        [pad(w_in[layer, :, c_f:c_f + ATT_HEADS]), pad(w_in[layer, :, c_dt:]),
         pad(w_in[layer, :, D_MAIN:c_dt])], axis=1)
    b_small = jnp.concatenate([_row(prm["b_forget"][layer], LANES),
                               _row(prm["dt_bias"][layer], LANES),
                               jnp.zeros((1, LANES), F32)], axis=1)
    a_row = _row(-jnp.exp(prm["a_log"][layer].astype(F32)), LANES)
    dskip_row = jnp.repeat(prm["d_skip"][layer].astype(F32), SSD_HEAD_DIM).reshape(1, D_SSD)

    y = _in_proj(h_bf, w_in, layer, min(s, 1024), 1024)
    faug, dtv, tail = _gates(h_bf, w_small, b_small, min(s, 512))
    att = _attention(y, faug, min(s, 1024), min(s, 512), 2)
    ssd = _ssd(y, tail, dtv, prm["conv_w"][layer], _row(prm["conv_b"][layer], D_CONV),
               a_row, dskip_row, _row(prm["ssd_norm_w"][layer], D_SSD))
    return _out_proj_ln_route(att, ssd, prm["w_out"], layer, h,
                              _row(prm["ln1_g"][layer], D_MODEL),
                              _row(prm["ln1_b"][layer], D_MODEL),
                              w_router2, bias_col, min(s, 256))


def _channel_layer(h, h_bf, routing, p_all, layer, bi, prm):
    s = h.shape[0]
    tm = min(s, 256)
    idx, wcol, cnt = routing
    pos0, pos1, items = _plan(idx, cnt, tm)
    xs = _dispatch(pos0, pos1, h, tm)
    ys = _moe(items, xs, prm["w_gate"], prm["w_up"], prm["w_down"], layer, tm)
    return _ple_combine_ln(pos0, pos1, h_bf, h, p_all, bi, prm["w_ple_gate"], prm["w_ple"],
                           layer, wcol, _row(prm["ln2_g"][layer], D_MODEL),
                           _row(prm["ln2_b"][layer], D_MODEL), ys, tm)


def _prep_router(p):
    order = jnp.arange(N_EXPERTS).reshape(N_EXPERT_GROUPS, EXPERTS_PER_GROUP).T.reshape(-1)
    w = p["w_router"].astype(F32)[:, order]
    w = jnp.concatenate([w, jnp.zeros((D_MODEL, LANES - N_EXPERTS), F32)], axis=1)
    w_hi = w.astype(BF16)
    w_lo = (w - w_hi.astype(F32)).astype(BF16)
    bias_col = p["router_bias"].astype(F32)[order].reshape(N_EXPERTS, 1)
    return jnp.stack([w_hi, w_lo]), bias_col


def kernel(x, p, w_in, b_forget, conv_w, conv_b, dt_bias, a_log, d_skip, ssd_norm_w,
           w_out, ln1_g, ln1_b, w_router, router_bias, w_gate, w_up, w_down, w_ple,
           w_ple_gate, ln2_g, ln2_b):
    prm = dict(w_in=w_in, b_forget=b_forget, conv_w=conv_w, conv_b=conv_b,
               dt_bias=dt_bias, a_log=a_log, d_skip=d_skip, ssd_norm_w=ssd_norm_w,
               w_out=w_out, ln1_g=ln1_g, ln1_b=ln1_b, w_router=w_router,
               router_bias=router_bias, w_gate=w_gate, w_up=w_up, w_down=w_down,
               w_ple=w_ple, w_ple_gate=w_ple_gate, ln2_g=ln2_g, ln2_b=ln2_b)
    batch = x.shape[0]
    w_router2, bias_col = _prep_router(prm)
    outs = []
    for bi in range(batch):
        h = x[bi]
        h_bf = h.astype(BF16)
        for layer in range(w_in.shape[0]):
            h, h_bf, *routing = _mixer_layer(h, h_bf, prm, layer, w_router2, bias_col)
            h, h_bf = _channel_layer(h, h_bf, routing, p, layer, bi, prm)
        outs.append(h)
    return jnp.stack(outs)
```

```python
import functools
import math

import jax
import jax.numpy as jnp
from jax import lax
from jax.experimental import pallas as pl
from jax.experimental.pallas import tpu as pltpu

F32 = jnp.float32
BF16 = jnp.bfloat16
I32 = jnp.int32

D_MODEL = 2048
ATT_HEADS = 8
ATT_HEAD_DIM = 128
D_ATT = ATT_HEADS * ATT_HEAD_DIM
SSD_HEADS = 16
SSD_HEAD_DIM = 64
D_SSD = SSD_HEADS * SSD_HEAD_DIM
SSD_GROUPS = 4
HEADS_PER_GROUP = SSD_HEADS // SSD_GROUPS
SSD_STATE = 128
D_GROUP = D_SSD // SSD_GROUPS
CONV_WIDTH = 4
D_BC = SSD_GROUPS * SSD_STATE
D_CONV = D_SSD + 2 * D_BC
N_EXPERTS = 16
N_EXPERT_GROUPS = 4
EXPERTS_PER_GROUP = N_EXPERTS // N_EXPERT_GROUPS
D_EXPERT = 1024
PLE_DIM = 256
DEPTH = 2
DEEPNORM_ALPHA = (2 * DEPTH) ** 0.25
LN_EPS = 1e-5
RMS_EPS = 1e-5
LOG2_E = math.log2(math.e)

LANES = 128
SUBLANES = 8
VMEM_LIMIT_BYTES = 56 * 1024 * 1024

D_MAIN = 3 * D_ATT + D_SSD + D_CONV
COL_WIN = 3 * D_ATT
D_WIN = D_MAIN - COL_WIN
XBC_TAIL = ATT_HEADS

SSD_CHUNK = 128
CONV_HALO = SUBLANES
SUM_ROWS = 2 * SUBLANES


def _params(*sem):
    return pltpu.CompilerParams(dimension_semantics=sem,
                                vmem_limit_bytes=VMEM_LIMIT_BYTES)


def _softplus(u):
    return jnp.maximum(u, 0.0) + jnp.log1p(jnp.exp(-jnp.abs(u)))


def _sigmoid(u):
    return 1.0 / (1.0 + jnp.exp(-u))


def _silu(u):
    return u * _sigmoid(u)


def _split3(v):
    hi = v.astype(BF16)
    r1 = v - hi.astype(F32)
    mid = r1.astype(BF16)
    lo = (r1 - mid.astype(F32)).astype(BF16)
    return hi, mid, lo


def _dot(a, b):
    return jnp.dot(a, b, preferred_element_type=F32)


def _exact_left_dot(ones_bf16, v):
    n = v.shape[1]
    r = _dot(ones_bf16, jnp.concatenate(_split3(v), axis=1))
    return r[:, :n] + (r[:, n:2 * n] + r[:, 2 * n:])


def _proj_kernel(x_ref, w_ref, o_ref, w_bf, *, q_blocks):
    @pl.when(pl.program_id(1) == 0)
    def _():
        w_bf[...] = w_ref[0].astype(BF16)

    scale = jnp.where(pl.program_id(0) < q_blocks, LOG2_E * ATT_HEAD_DIM ** -0.5, 1.0)
    o_ref[...] = (_dot(x_ref[...], w_bf[...]) * scale).astype(o_ref.dtype)


def _in_proj(h_bf, w_in, layer, tm, tn):
    s = h_bf.shape[0]
    return pl.pallas_call(
        functools.partial(_proj_kernel, q_blocks=D_ATT // tn),
        grid=(D_MAIN // tn, s // tm),
        in_specs=[pl.BlockSpec((tm, D_MODEL), lambda j, i: (i, 0)),
                  pl.BlockSpec((1, D_MODEL, tn), lambda j, i: (layer, 0, j))],
        out_specs=pl.BlockSpec((tm, tn), lambda j, i: (i, j)),
        out_shape=jax.ShapeDtypeStruct((s, D_MAIN), BF16),
        scratch_shapes=[pltpu.VMEM((D_MODEL, tn), BF16)],
        compiler_params=_params("arbitrary", "arbitrary"),
        name="in_proj",
    )(h_bf, w_in)


def _gates_kernel(x_ref, w_ref, b_ref, tri_ref, sel_ref, ones_ref, faug_ref, dt_ref,
                  tail_ref, carry_ref, w_bf):
    @pl.when(pl.program_id(0) == 0)
    def _():
        carry_ref[...] = jnp.zeros_like(carry_ref)
        w_bf[...] = w_ref[...].astype(BF16)

    v = _dot(x_ref[...], w_bf[...]) + b_ref[...]
    log_f = -_softplus(-v[:, :LANES])
    dt_ref[...] = _softplus(v[:, LANES:2 * LANES])
    tail_ref[...] = v[:, 2 * LANES:]
    fcum = _exact_left_dot(tri_ref[...], log_f) + carry_ref[...]
    carry_ref[...] = fcum[-1:, :]
    pieces = jnp.concatenate(_split3(fcum * LOG2_E), axis=1)
    aug = _dot(pieces, sel_ref[...]) + ones_ref[...]
    faug_ref[...] = aug.astype(BF16)


def _forget_layout():
    import numpy as np
    sel = np.zeros((3, LANES, 2 * D_ATT), np.float32)
    ones = np.zeros((1, 2 * D_ATT), np.float32)
    for h in range(ATT_HEADS):
        for k in range(3):
            sel[k, h, ATT_HEAD_DIM * h + k] = 1.0
            sel[k, h, D_ATT + ATT_HEAD_DIM * h + 3 + k] = -1.0
            ones[0, ATT_HEAD_DIM * h + 3 + k] = 1.0
            ones[0, D_ATT + ATT_HEAD_DIM * h + k] = 1.0
    return jnp.asarray(sel.reshape(3 * LANES, 2 * D_ATT), BF16), jnp.asarray(ones, F32)


def _gates(h_bf, w_small, b_small, tm):
    s = h_bf.shape[0]
    tri = jnp.tril(jnp.ones((tm, tm), F32)).astype(BF16)
    sel, ones = _forget_layout()
    return pl.pallas_call(
        _gates_kernel,
        grid=(s // tm,),
        in_specs=[pl.BlockSpec((tm, D_MODEL), lambda i: (i, 0)),
                  pl.BlockSpec((D_MODEL, 3 * LANES), lambda i: (0, 0)),
                  pl.BlockSpec((1, 3 * LANES), lambda i: (0, 0)),
                  pl.BlockSpec((tm, tm), lambda i: (0, 0)),
                  pl.BlockSpec((3 * LANES, 2 * D_ATT), lambda i: (0, 0)),
                  pl.BlockSpec((1, 2 * D_ATT), lambda i: (0, 0))],
        out_specs=[pl.BlockSpec((tm, 2 * D_ATT), lambda i: (i, 0)),
                   pl.BlockSpec((tm, LANES), lambda i: (i, 0)),
                   pl.BlockSpec((tm, LANES), lambda i: (i, 0))],
        out_shape=[jax.ShapeDtypeStruct((s, 2 * D_ATT), BF16),
                   jax.ShapeDtypeStruct((s, LANES), F32),
                   jax.ShapeDtypeStruct((s, LANES), F32)],
        scratch_shapes=[pltpu.VMEM((1, LANES), F32),
                        pltpu.VMEM((D_MODEL, 3 * LANES), BF16)],
        compiler_params=_params("arbitrary"),
        name="gates",
    )(h_bf, w_small, b_small, tri, sel, ones)


def _attn_kernel(q_ref, fq_ref, k_ref, fk_ref, v_ref, o_ref, vt_ref, sa_ref, sb_ref,
                 *, tq, tk, nh):
    assert tq == 2 * tk
    i = pl.program_id(1)
    n_chunks = vt_ref.shape[1]
    dh = ATT_HEAD_DIM
    heads = range(nh)

    @pl.when(i == 0)
    def _():
        for hd in heads:
            for c in range(n_chunks):
                blk = v_ref[c * tk:(c + 1) * tk, hd * dh:(hd + 1) * dh]
                vt_ref[hd, c, :dh, :] = blk.astype(F32).T.astype(BF16)
                vt_ref[hd, c, dh:, :] = jnp.ones((SUM_ROWS, tk), BF16)

    q_t = []
    for hd in heads:
        cols = slice(hd * dh, (hd + 1) * dh)
        q_aug = jnp.concatenate([q_ref[:, cols], fq_ref[:, cols]], axis=1)
        q_t.append(q_aug.astype(F32).T.astype(BF16))
    kv_pos = lax.broadcasted_iota(I32, (tk, tq), 0)
    q_pos = lax.broadcasted_iota(I32, (tk, tq), 1)

    def logits(hd, c):
        start = pl.multiple_of(c * tk, tk)
        cols = slice(hd * dh, (hd + 1) * dh)
        k_aug = jnp.concatenate([k_ref[pl.ds(start, tk), cols],
                                 fk_ref[pl.ds(start, tk), cols]], axis=1)
        return _dot(k_aug, q_t[hd])

    def update(st, hd, c, carry, diag_offset):
        m, acc = carry
        if diag_offset is not None:
            st = jnp.where(kv_pos + diag_offset <= q_pos, st, -jnp.inf)
        m_new = jnp.maximum(m, jnp.max(st, axis=0, keepdims=True))
        p = jnp.exp2(st - m_new)
        alpha = jnp.exp2(m - m_new)
        acc = alpha * acc + _dot(vt_ref[hd, c], p.astype(BF16))
        return m_new, acc

    first = 2 * i
    for hd in heads:
        sa_ref[hd] = logits(hd, 0)

    def pair(t, carries):
        out = []
        for hd in heads:
            sb_ref[hd] = logits(hd, 2 * t + 1)
        for hd in heads:
            out.append(update(sa_ref[hd], hd, 2 * t, carries[hd], None))
        for hd in heads:
            sa_ref[hd] = logits(hd, 2 * t + 2)
        return tuple(update(sb_ref[hd], hd, 2 * t + 1, out[hd], None) for hd in heads)

    init = (jnp.full((1, tq), -jnp.inf, F32), jnp.zeros((dh + SUM_ROWS, tq), F32))
    carries = lax.fori_loop(0, i, pair, tuple(init for _ in heads))
    for hd in heads:
        sb_ref[hd] = logits(hd, first + 1)
    for hd in heads:
        carry = update(sa_ref[hd], hd, first, carries[hd], 0)
        _, acc = update(sb_ref[hd], hd, first + 1, carry, tk)
        out = acc[:dh, :] / acc[dh:dh + 1, :]
        o_ref[:, hd * dh:(hd + 1) * dh] = out.T.astype(o_ref.dtype)


def _attention(y, faug, tq, tk, nh):
    s = y.shape[0]
    width = nh * ATT_HEAD_DIM
    groups = ATT_HEADS // nh
    kern = functools.partial(_attn_kernel, tq=tq, tk=tk, nh=nh)
    q_block = lambda g0: pl.BlockSpec((tq, width), lambda g, i: (i, g0 + g))
    kv_block = lambda g0: pl.BlockSpec((s, width), lambda g, i: (0, g0 + g))
    return pl.pallas_call(
        kern,
        grid=(groups, s // tq),
        in_specs=[q_block(0), q_block(0), kv_block(groups), kv_block(groups),
                  kv_block(2 * groups)],
        out_specs=pl.BlockSpec((tq, width), lambda g, i: (i, g)),
        out_shape=jax.ShapeDtypeStruct((s, D_ATT), BF16),
        scratch_shapes=[pltpu.VMEM((nh, s // tk, ATT_HEAD_DIM + SUM_ROWS, tk), BF16),
                        pltpu.VMEM((nh, tk, tq), F32), pltpu.VMEM((nh, tk, tq), F32)],
        compiler_params=_params("arbitrary", "arbitrary"),
        name="fox_attention",
    )(y, faug, y, faug, y)


def _expand_heads(w, n_heads):
    length = w.shape[0]
    lane = lax.broadcasted_iota(I32, (length, LANES), 1)
    blocks = []
    for m in range(n_heads // 2):
        blocks.append(jnp.where(lane < SSD_HEAD_DIM, w[:, 2 * m:2 * m + 1],
                                w[:, 2 * m + 1:2 * m + 2]))
    return jnp.concatenate(blocks, axis=1)


def _ssd_kernel(win_ref, tail_ref, dt_ref, convw_ref, convb_ref, a_ref, dskip_ref,
                normw_ref, tri_ref, o_ref, ubuf_ref, state_ref):
    length = SSD_CHUNK

    @pl.when(pl.program_id(0) == 0)
    def _():
        ubuf_ref[0:CONV_HALO, :] = jnp.zeros((CONV_HALO, D_CONV), F32)
        state_ref[...] = jnp.zeros_like(state_ref)

    win = jnp.concatenate([win_ref[...].astype(F32), tail_ref[...]], axis=1)
    win = jnp.concatenate([win[:, ATT_HEADS:], win[:, :ATT_HEADS]], axis=1)
    z = win[:, :D_SSD]

    ubuf_ref[CONV_HALO:CONV_HALO + length, :] = win[:, D_SSD:D_SSD + D_CONV]
    conv = convb_ref[...]
    for k in range(CONV_WIDTH):
        off = CONV_HALO - (CONV_WIDTH - 1) + k
        conv = conv + convw_ref[k:k + 1, :] * ubuf_ref[off:off + length, :]
    tail = ubuf_ref[length:length + CONV_HALO, :]
    ubuf_ref[0:CONV_HALO, :] = tail
    xc = _silu(conv)
    xs = xc[:, :D_SSD]

    dt = dt_ref[...]
    da = dt * a_ref[...]
    acum = _exact_left_dot(tri_ref[...], da)
    acum_t = acum.T
    a_last = acum[length - 1:length, :]
    w_off = _expand_heads(jnp.exp(acum), SSD_HEADS)
    w_state = _expand_heads(jnp.exp(a_last - acum) * dt, SSD_HEADS)
    w_dt = _expand_heads(dt, SSD_HEADS)
    chunk_decay = _expand_heads(jnp.exp(a_last), SSD_HEADS)

    x_dt = (xs * w_dt).astype(BF16)
    x_state = (xs * w_state).astype(BF16)
    row = lax.broadcasted_iota(I32, (length, length), 0)
    col = lax.broadcasted_iota(I32, (length, length), 1)
    causal = col <= row
    grp_lane = lax.broadcasted_iota(I32, (length, D_GROUP), 1) // SSD_HEAD_DIM

    y_parts = []
    for g in range(SSD_GROUPS):
        b_g = xc[:, D_SSD + g * SSD_STATE:D_SSD + (g + 1) * SSD_STATE].astype(BF16)
        c_g = xc[:, D_SSD + D_BC + g * SSD_STATE:
                 D_SSD + D_BC + (g + 1) * SSD_STATE].astype(BF16)
        cb = lax.dot_general(c_g, b_g, (((1,), (1,)), ((), ())),
                             preferred_element_type=F32)
        cols = slice(g * D_GROUP, (g + 1) * D_GROUP)
        x_dt_g = x_dt[:, cols]
        m_blocks, x_blocks = [], []
        for r in range(HEADS_PER_GROUP):
            hd = g * HEADS_PER_GROUP + r
            seg = acum[:, hd:hd + 1] - acum_t[hd:hd + 1, :]
            decay = jnp.exp(jnp.where(causal, seg, -jnp.inf))
            m_blocks.append((cb * decay).astype(BF16))
            x_blocks.append(jnp.where(grp_lane == r, x_dt_g, jnp.zeros_like(x_dt_g)))
        y_diag = _dot(jnp.concatenate(m_blocks, axis=1),
                      jnp.concatenate(x_blocks, axis=0))
        st = state_ref[g]
        y_off = _dot(c_g, st.astype(BF16)) * w_off[:, cols]
        y_parts.append(y_diag + y_off)
        new_st = lax.dot_general(b_g, x_state[:, cols], (((0,), (0,)), ((), ())),
                                 preferred_element_type=F32)
        state_ref[g] = st * chunk_decay[:, cols] + new_st

    y = jnp.concatenate(y_parts, axis=1) + dskip_ref[...] * xs
    yg = y * _silu(z)
    outs = []
    for g in range(SSD_GROUPS):
        blk = yg[:, g * D_GROUP:(g + 1) * D_GROUP]
        ms = jnp.mean(blk * blk, axis=1, keepdims=True)
        outs.append(blk * lax.rsqrt(ms + RMS_EPS))
    o_ref[...] = (jnp.concatenate(outs, axis=1) * normw_ref[...]).astype(o_ref.dtype)


def _ssd(y, tail, dtv, conv_w, conv_b, a_row, dskip_row, normw_row):
    s = y.shape[0]
    length = SSD_CHUNK
    tri = jnp.tril(jnp.ones((length, length), F32)).astype(BF16)
    full = lambda shape: pl.BlockSpec(shape, lambda c: (0,) * len(shape))
    return pl.pallas_call(
        _ssd_kernel,
        grid=(s // length,),
        in_specs=[pl.BlockSpec((length, D_WIN), lambda c: (c, COL_WIN // D_WIN)),
                  pl.BlockSpec((length, LANES), lambda c: (c, 0)),
                  pl.BlockSpec((length, LANES), lambda c: (c, 0)),
                  full((CONV_WIDTH, D_CONV)), full((1, D_CONV)), full((1, LANES)),
                  full((1, D_SSD)), full((1, D_SSD)), full((length, length))],
        out_specs=pl.BlockSpec((length, D_SSD), lambda c: (c, 0)),
        out_shape=jax.ShapeDtypeStruct((s, D_SSD), BF16),
        scratch_shapes=[pltpu.VMEM((length + CONV_HALO, D_CONV), F32),
                        pltpu.VMEM((SSD_GROUPS, SSD_STATE, D_GROUP), F32)],
        compiler_params=_params("arbitrary"),
        name="ssd",
    )(y, tail, dtv, conv_w, conv_b, a_row, dskip_row, normw_row, tri)


def _layer_norm(xf, g, b):
    mu = jnp.mean(xf, axis=1, keepdims=True)
    xc = xf - mu
    var = jnp.mean(xc * xc, axis=1, keepdims=True)
    return xc * lax.rsqrt(var + LN_EPS) * g + b


def _resident_weight(shape, layer):
    return pl.BlockSpec((1,) + shape, lambda i, *_: (layer,) + (0,) * len(shape),
                        pipeline_mode=pl.Buffered(1))


def _out_proj_kernel(att_ref, ssd_ref, w_ref, h_ref, g_ref, b_ref, wr_ref, rbias_ref,
                     triu_ref, o_ref, obf_ref, idx_ref, wcol_ref, cnt_ref, w_bf,
                     carry_ref):
    @pl.when(pl.program_id(0) == 0)
    def _():
        w_bf[...] = w_ref[0].astype(BF16)

    mix = _dot(att_ref[...], w_bf[:D_ATT, :]) + _dot(ssd_ref[...], w_bf[D_ATT:, :])
    out = _layer_norm(DEEPNORM_ALPHA * h_ref[...] + mix, g_ref[...], b_ref[...])
    o_ref[...] = out
    obf_ref[...] = out.astype(BF16)
    _route(out, wr_ref, rbias_ref, triu_ref, idx_ref, wcol_ref, cnt_ref, carry_ref)


def _out_proj_ln_route(att, ssd, w_out, layer, h, g, b, w_router2, bias_col, tm):
    s = h.shape[0]
    full = lambda shape: pl.BlockSpec(shape, lambda i: (0,) * len(shape))
    row_tile = lambda width: pl.BlockSpec((tm, width), lambda i: (i, 0))
    triu = jnp.triu(jnp.ones((tm, tm), F32)).astype(BF16)
    return pl.pallas_call(
        _out_proj_kernel,
        grid=(s // tm,),
        in_specs=[row_tile(D_ATT), row_tile(D_SSD),
                  _resident_weight((D_MODEL, D_MODEL), layer),
                  row_tile(D_MODEL), full((1, D_MODEL)), full((1, D_MODEL)),
                  full((2, D_MODEL, LANES)), full((N_EXPERTS, 1)), full((tm, tm))],
        out_specs=[row_tile(D_MODEL), row_tile(D_MODEL),
                   pl.BlockSpec((SUBLANES, tm), lambda i: (0, i)),
                   row_tile(LANES), full((N_EXPERTS, LANES))],
        out_shape=[jax.ShapeDtypeStruct((s, D_MODEL), F32),
                   jax.ShapeDtypeStruct((s, D_MODEL), BF16),
                   jax.ShapeDtypeStruct((SUBLANES, s), I32),
                   jax.ShapeDtypeStruct((s, LANES), F32),
                   jax.ShapeDtypeStruct((N_EXPERTS, LANES), F32)],
        scratch_shapes=[pltpu.VMEM((D_MODEL, D_MODEL), BF16),
                        pltpu.VMEM((N_EXPERTS, LANES), F32)],
        compiler_params=_params("arbitrary"),
        name="out_proj_ln_route",
    )(att, ssd, w_out, h, g, b, w_router2, bias_col, triu)


def _route(h, wr_ref, bias_ref, triu_ref, idx_ref, wcol_ref, cnt_ref, carry_ref):
    tm = h.shape[0]

    @pl.when(pl.program_id(0) == 0)
    def _():
        carry_ref[...] = jnp.zeros_like(carry_ref)

    h_hi = h.astype(BF16)
    h_lo = (h - h_hi.astype(F32)).astype(BF16)
    logits = _dot(h_hi, wr_ref[0]) + (_dot(h_hi, wr_ref[1]) + _dot(h_lo, wr_ref[0]))
    lt = logits.T
    scores = _sigmoid(lt[:N_EXPERTS, :])
    biased = scores + bias_ref[...]
    npg = EXPERTS_PER_GROUP
    s_r = [scores[npg * r:npg * (r + 1), :] for r in range(npg)]
    b_r = [biased[npg * r:npg * (r + 1), :] for r in range(npg)]

    gs = None
    for a in range(npg):
        for b in range(a + 1, npg):
            pair = b_r[a] + b_r[b]
            gs = pair if gs is None else jnp.maximum(gs, pair)
    best = jnp.zeros((1, tm), I32)
    top = gs[0:1, :]
    for g in range(1, N_EXPERT_GROUPS):
        better = gs[g:g + 1, :] > top
        best = jnp.where(better, g, best)
        top = jnp.where(better, gs[g:g + 1, :], top)

    def pick_group(a):
        out = a[0:1, :]
        for g in range(1, N_EXPERT_GROUPS):
            out = jnp.where(best == g, a[g:g + 1, :], out)
        return out

    sb = [pick_group(b) for b in b_r]
    ss = [pick_group(s) for s in s_r]
    first = jnp.zeros((1, tm), I32)
    fmax = sb[0]
    for r in range(1, npg):
        better = sb[r] > fmax
        first = jnp.where(better, r, first)
        fmax = jnp.where(better, sb[r], fmax)
    second = jnp.zeros((1, tm), I32)
    smax = jnp.full((1, tm), -jnp.inf, F32)
    for r in range(npg):
        cand = jnp.where(first == r, -jnp.inf, sb[r])
        better = cand > smax
        second = jnp.where(better, r, second)
        smax = jnp.where(better, cand, smax)

    def pick_expert(which):
        out = ss[0]
        for r in range(1, npg):
            out = jnp.where(which == r, ss[r], out)
        return out

    a0 = pick_expert(first)
    a1 = pick_expert(second)
    denom = a0 + a1
    e0 = best * npg + first
    e1 = best * npg + second

    erow = lax.broadcasted_iota(I32, (N_EXPERTS, tm), 0)
    hit0 = erow == e0
    hit1 = erow == e1
    onehot = jnp.where(hit0, 1.0, 0.0) + jnp.where(hit1, 1.0, 0.0)
    cum = _dot(onehot.astype(BF16), triu_ref[...])
    carry = carry_ref[...][:, 0:1]
    before = cum - onehot + carry
    rank0 = jnp.sum(jnp.where(hit0, before, 0.0), axis=0, keepdims=True)
    rank1 = jnp.sum(jnp.where(hit1, before, 0.0), axis=0, keepdims=True)
    new_carry = carry_ref[...] + cum[:, tm - 1:tm]
    carry_ref[...] = new_carry
    cnt_ref[...] = new_carry

    zeros_i = jnp.zeros((SUBLANES - 4, tm), I32)
    idx_ref[...] = jnp.concatenate(
        [e0, e1, rank0.astype(I32), rank1.astype(I32), zeros_i], axis=0)
    wrows = jnp.concatenate([a0 / denom, a1 / denom,
                             jnp.zeros((LANES - 2, tm), F32)], axis=0)
    wcol_ref[...] = wrows.T


def _plan_kernel(idx_ref, cnt_ref, pos_ref, items_ref, *, tm):
    ne = N_EXPERTS
    counts = cnt_ref[...][:, 0:1]

    def across(col):
        wide = jnp.concatenate([jnp.broadcast_to(col, (ne, LANES)),
                                jnp.zeros((LANES - ne, LANES), F32)], axis=0)
        return wide.T[:ne, :ne]

    r_id = lax.broadcasted_iota(I32, (ne, ne), 0)
    c_id = lax.broadcasted_iota(I32, (ne, ne), 1)

    def running_sum(col):
        return jnp.sum(jnp.where(c_id <= r_id, across(col), 0.0), axis=1, keepdims=True)

    ends = running_sum(counts)
    starts = ends - counts
    used = jnp.where(counts > 0, 1.0, 0.0)
    first_tile = jnp.floor(starts / tm)
    last_tile = jnp.floor((ends - 1.0) / tm)
    n_e = used * (last_tile - first_tile + 1.0)
    item_end = running_sum(n_e)
    item_start = item_end - n_e
    total = item_end[ne - 1:ne, :]
    later_used = jnp.logical_and(c_id > r_id, across(used) > 0)
    next_e = jnp.min(jnp.where(later_used, c_id.astype(F32), float(ne)), axis=1,
                     keepdims=True)
    next_e = jnp.where(next_e == ne, -1.0, next_e)
    ordinal = running_sum(used) - 1.0
    slot_e = ordinal - 2.0 * jnp.floor(ordinal / 2.0)

    w = lax.broadcasted_iota(I32, (1, LANES), 1).astype(F32)
    valid = w < total
    wc = jnp.minimum(w, total - 1.0)
    e_w = jnp.minimum(jnp.sum(jnp.where(item_end <= wc, 1.0, 0.0), axis=0, keepdims=True),
                      ne - 1.0)
    e_id = lax.broadcasted_iota(I32, (ne, LANES), 0).astype(F32)
    mine = e_id == e_w
    pick = lambda col: jnp.sum(jnp.where(mine, col, 0.0), axis=0, keepdims=True)
    tile_w = pick(first_tile) + (wc - pick(item_start))
    lo = jnp.where(valid, jnp.clip(pick(starts) - tile_w * tm, 0.0, tm), 0.0)
    hi = jnp.where(valid, jnp.clip(pick(ends) - tile_w * tm, 0.0, tm), 0.0)
    rows = [tile_w, e_w, lo, hi, pick(slot_e), pick(next_e)]
    rows += [jnp.zeros((1, LANES), F32)] * (SUBLANES - len(rows))
    items_ref[...] = jnp.concatenate(rows, axis=0).astype(I32)

    s = idx_ref.shape[1]
    tok_e = lax.broadcasted_iota(I32, (ne, s), 0)
    starts_i = starts.astype(I32)
    pos = []
    for k in range(2):
        base = jnp.sum(jnp.where(tok_e == idx_ref[k:k + 1, :], starts_i, 0), axis=0,
                       keepdims=True)
        pos.append(base + idx_ref[2 + k:3 + k, :])
    pos += [jnp.zeros((1, s), I32)] * (SUBLANES - len(pos))
    pos_ref[...] = jnp.concatenate(pos, axis=0)


def _plan(idx, cnt, tm):
    s = idx.shape[1]
    n_items = 2 * s // tm + N_EXPERTS - 1
    assert n_items <= LANES
    pos, items = pl.pallas_call(
        functools.partial(_plan_kernel, tm=tm),
        out_shape=[jax.ShapeDtypeStruct((SUBLANES, s), I32),
                   jax.ShapeDtypeStruct((SUBLANES, LANES), I32)],
        compiler_params=pltpu.CompilerParams(vmem_limit_bytes=VMEM_LIMIT_BYTES),
        name="plan",
    )(idx, cnt)
    return pos[0], pos[1], tuple(items[k, :n_items] for k in range(6))


def _dispatch_kernel(pos0_ref, pos1_ref, h_ref, xs_ref, sems, *, tm):
    base = pl.program_id(0) * tm

    def issue(r, carry):
        src = h_ref.at[pl.ds(r, 1), :]
        pltpu.make_async_copy(src, xs_ref.at[pl.ds(pos0_ref[base + r], 1), :],
                              sems.at[0]).start(priority=0)
        pltpu.make_async_copy(src, xs_ref.at[pl.ds(pos1_ref[base + r], 1), :],
                              sems.at[1]).start(priority=1)
        return carry

    lax.fori_loop(0, tm, issue, 0, unroll=8)
    for k in range(2):
        pltpu.make_async_copy(h_ref, xs_ref.at[pl.ds(0, tm), :], sems.at[k]).wait()


def _dispatch(pos0, pos1, h, tm):
    s = h.shape[0]
    kern = functools.partial(_dispatch_kernel, tm=tm)
    return pl.pallas_call(
        kern,
        grid_spec=pltpu.PrefetchScalarGridSpec(
            num_scalar_prefetch=2,
            grid=(s // tm,),
            in_specs=[pl.BlockSpec((tm, D_MODEL), lambda i, p0, p1: (i, 0))],
            out_specs=pl.BlockSpec(memory_space=pl.ANY),
            scratch_shapes=[pltpu.SemaphoreType.DMA((2,))]),
        out_shape=jax.ShapeDtypeStruct((2 * s, D_MODEL), F32),
        compiler_params=_params("arbitrary"),
        name="dispatch",
    )(pos0, pos1, h)


def _item_flags(tile_ref, exp_ref):
    w = pl.program_id(0)
    prev = jnp.maximum(w - 1, 0)
    fresh_tile = jnp.logical_or(w == 0, tile_ref[w] != tile_ref[prev])
    new_expert = jnp.logical_or(w == 0, exp_ref[w] != exp_ref[prev])
    return fresh_tile, new_expert


def _store_rows(o_ref, val, lo, hi, fresh_tile):
    row = lax.broadcasted_iota(I32, (val.shape[0], 1), 0)
    keep = jnp.logical_and(row >= lo, row < hi)

    @pl.when(fresh_tile)
    def _():
        o_ref[...] = jnp.where(keep, val, jnp.zeros_like(val))

    @pl.when(jnp.logical_not(fresh_tile))
    def _():
        o_ref[...] = jnp.where(keep, val, o_ref[...])


def _expert_weights(exp_ref, slot_ref, next_ref, new_expert, layer, w_hbm, stage_ref,
                    w_bf, sems):
    w = pl.program_id(0)
    n_mats = len(w_hbm)

    def fetch(expert, slot):
        return [pltpu.make_async_copy(w_hbm[m].at[layer, expert], stage_ref.at[slot, m],
                                      sems.at[slot, m]) for m in range(n_mats)]

    @pl.when(w == 0)
    def _():
        for copy in fetch(exp_ref[0], slot_ref[0]):
            copy.start(priority=1)

    @pl.when(new_expert)
    def _():
        slot = slot_ref[w]
        for m, copy in enumerate(fetch(exp_ref[w], slot)):
            copy.wait()
            w_bf[m] = stage_ref[slot, m].astype(BF16)

        @pl.when(next_ref[w] >= 0)
        def _():
            for copy in fetch(next_ref[w], 1 - slot):
                copy.start(priority=1)


def _moe_up_kernel(tile_ref, exp_ref, lo_ref, hi_ref, slot_ref, next_ref, x_ref, wg_hbm,
                   wu_hbm, o_ref, stage_ref, w_bf, sems, *, layer):
    w = pl.program_id(0)
    fresh_tile, new_expert = _item_flags(tile_ref, exp_ref)
    _expert_weights(exp_ref, slot_ref, next_ref, new_expert, layer, (wg_hbm, wu_hbm),
                    stage_ref, w_bf, sems)

    @pl.when(hi_ref[w] > lo_ref[w])
    def _():
        x = x_ref[...].astype(BF16)
        hid = _silu(_dot(x, w_bf[0])) * _dot(x, w_bf[1])
        _store_rows(o_ref, hid.astype(BF16), lo_ref[w], hi_ref[w], fresh_tile)


def _moe_down_kernel(tile_ref, exp_ref, lo_ref, hi_ref, slot_ref, next_ref, h_ref, wd_hbm,
                     o_ref, stage_ref, w_bf, sems, *, layer):
    w = pl.program_id(0)
    fresh_tile, new_expert = _item_flags(tile_ref, exp_ref)
    _expert_weights(exp_ref, slot_ref, next_ref, new_expert, layer, (wd_hbm,),
                    stage_ref, w_bf, sems)

    @pl.when(hi_ref[w] > lo_ref[w])
    def _():
        y = _dot(h_ref[...], w_bf[0])
        _store_rows(o_ref, y, lo_ref[w], hi_ref[w], fresh_tile)


def _moe(items, xs, w_gate, w_up, w_down, layer, tm):
    rows = xs.shape[0]
    n_items = items[0].shape[0]
    row_tile = lambda width: pl.BlockSpec((tm, width), lambda w, tile, *_: (tile[w], 0))
    in_hbm = pl.BlockSpec(memory_space=pl.ANY)

    def weight_scratch(n_mats, shape):
        return [pltpu.VMEM((2, n_mats) + shape, F32), pltpu.VMEM((n_mats,) + shape, BF16),
                pltpu.SemaphoreType.DMA((2, n_mats))]

    hid = pl.pallas_call(
        functools.partial(_moe_up_kernel, layer=layer),
        grid_spec=pltpu.PrefetchScalarGridSpec(
            num_scalar_prefetch=len(items),
            grid=(n_items,),
            in_specs=[row_tile(D_MODEL), in_hbm, in_hbm],
            out_specs=row_tile(D_EXPERT),
            scratch_shapes=weight_scratch(2, (D_MODEL, D_EXPERT))),
        out_shape=jax.ShapeDtypeStruct((rows, D_EXPERT), BF16),
        compiler_params=_params("arbitrary"),
        name="moe_up",
    )(*items, xs, w_gate, w_up)
    return pl.pallas_call(
        functools.partial(_moe_down_kernel, layer=layer),
        grid_spec=pltpu.PrefetchScalarGridSpec(
            num_scalar_prefetch=len(items),
            grid=(n_items,),
            in_specs=[row_tile(D_EXPERT), in_hbm],
            out_specs=row_tile(D_MODEL),
            scratch_shapes=weight_scratch(1, (D_EXPERT, D_MODEL))),
        out_shape=jax.ShapeDtypeStruct((rows, D_MODEL), F32),
        compiler_params=_params("arbitrary"),
        name="moe_down",
    )(*items, hid, w_down)


def _ple_kernel(pos0_ref, pos1_ref, hbf_ref, h_ref, p_ref, wgate_ref, wple_ref,
                wcol_ref, g_ref, b_ref, ys_ref, o_ref, obf_ref, buf_ref, sems,
                wgate_bf, wple_bf, *, tm):
    base = pl.program_id(0) * tm

    @pl.when(pl.program_id(0) == 0)
    def _():
        wgate_bf[...] = wgate_ref[0].astype(BF16)
        wple_bf[...] = wple_ref[0].astype(BF16)

    def issue(r, carry):
        for k, pos_ref in enumerate((pos0_ref, pos1_ref)):
            pltpu.make_async_copy(ys_ref.at[pl.ds(pos_ref[base + r], 1), :],
                                  buf_ref.at[k, pl.ds(r, 1), :],
                                  sems.at[k]).start(priority=k)
        return carry

    lax.fori_loop(0, tm, issue, 0, unroll=8)
    gate = _sigmoid(_dot(hbf_ref[...], wgate_bf[...]))
    ple = gate * _dot(p_ref[0, 0].astype(BF16), wple_bf[...])
    for k in range(2):
        pltpu.make_async_copy(ys_ref.at[pl.ds(0, tm), :], buf_ref.at[k], sems.at[k]).wait()
    wc = wcol_ref[...]
    moe = wc[:, 0:1] * buf_ref[0] + wc[:, 1:2] * buf_ref[1]
    out = _layer_norm(DEEPNORM_ALPHA * h_ref[...] + moe + ple, g_ref[...], b_ref[...])
    o_ref[...] = out
    obf_ref[...] = out.astype(BF16)


def _ple_combine_ln(pos0, pos1, h_bf, h, p_all, bi, w_ple_gate, w_ple, layer, wcol, g, b,
                    ys, tm):
    s = h.shape[0]
    kern = functools.partial(_ple_kernel, tm=tm)
    tile = lambda width: pl.BlockSpec((tm, width), lambda i, p0, p1: (i, 0))
    full = lambda shape: pl.BlockSpec(shape, lambda i, p0, p1: (0,) * len(shape))
    p_tile = pl.BlockSpec((1, 1, tm, PLE_DIM), lambda i, p0, p1: (layer, bi, i, 0))
    return pl.pallas_call(
        kern,
        grid_spec=pltpu.PrefetchScalarGridSpec(
            num_scalar_prefetch=2,
            grid=(s // tm,),
            in_specs=[tile(D_MODEL), tile(D_MODEL), p_tile,
                      _resident_weight((D_MODEL, D_MODEL), layer),
                      _resident_weight((PLE_DIM, D_MODEL), layer),
                      tile(LANES), full((1, D_MODEL)), full((1, D_MODEL)),
                      pl.BlockSpec(memory_space=pl.ANY)],
            out_specs=[tile(D_MODEL), tile(D_MODEL)],
            scratch_shapes=[pltpu.VMEM((2, tm, D_MODEL), F32),
                            pltpu.SemaphoreType.DMA((2,)),
                            pltpu.VMEM((D_MODEL, D_MODEL), BF16),
                            pltpu.VMEM((PLE_DIM, D_MODEL), BF16)]),
        out_shape=[jax.ShapeDtypeStruct((s, D_MODEL), F32),
                   jax.ShapeDtypeStruct((s, D_MODEL), BF16)],
        compiler_params=_params("arbitrary"),
        name="ple_combine_ln",
    )(pos0, pos1, h_bf, h, p_all, w_ple_gate, w_ple, wcol, g, b, ys)


def _row(v, width):
    v = v.astype(F32)
    return jnp.concatenate([v, jnp.zeros((width - v.shape[0],), F32)]).reshape(1, width)


def _mixer_layer(h, h_bf, prm, layer, w_router2, bias_col):
    s = h.shape[0]
    w_in = prm["w_in"]
    c_f = 3 * D_ATT
    c_dt = D_MAIN + XBC_TAIL
    pad = lambda cols: jnp.concatenate(
        [cols, jnp.zeros((D_MODEL, LANES - cols.shape[1]), cols.dtype)], axis=1)
    w_small = jnp.concatenate(
        [pad(w_in[layer, :, c_f:c_f + ATT_HEADS]), pad(w_in[layer, :, c_dt:]),
         pad(w_in[layer, :, D_MAIN:c_dt])], axis=1)
    b_small = jnp.concatenate([_row(prm["b_forget"][layer], LANES),
                               _row(prm["dt_bias"][layer], LANES),
                               jnp.zeros((1, LANES), F32)], axis=1)
    a_row = _row(-jnp.exp(prm["a_log"][layer].astype(F32)), LANES)
    dskip_row = jnp.repeat(prm["d_skip"][layer].astype(F32), SSD_HEAD_DIM).reshape(1, D_SSD)

    y = _in_proj(h_bf, w_in, layer, min(s, 1024), 1024)
    faug, dtv, tail = _gates(h_bf, w_small, b_small, min(s, 512))
    att = _attention(y, faug, min(s, 1024), min(s, 512), 2)
    ssd = _ssd(y, tail, dtv, prm["conv_w"][layer], _row(prm["conv_b"][layer], D_CONV),
               a_row, dskip_row, _row(prm["ssd_norm_w"][layer], D_SSD))
    return _out_proj_ln_route(att, ssd, prm["w_out"], layer, h,
                              _row(prm["ln1_g"][layer], D_MODEL),
                              _row(prm["ln1_b"][layer], D_MODEL),
                              w_router2, bias_col, min(s, 256))


def _channel_layer(h, h_bf, routing, p_all, layer, bi, prm):
    s = h.shape[0]
    tm = min(s, 256)
    idx, wcol, cnt = routing
    pos0, pos1, items = _plan(idx, cnt, tm)
    xs = _dispatch(pos0, pos1, h, tm)
    ys = _moe(items, xs, prm["w_gate"], prm["w_up"], prm["w_down"], layer, tm)
    return _ple_combine_ln(pos0, pos1, h_bf, h, p_all, bi, prm["w_ple_gate"], prm["w_ple"],
                           layer, wcol, _row(prm["ln2_g"][layer], D_MODEL),
                           _row(prm["ln2_b"][layer], D_MODEL), ys, tm)


def _prep_router(p):
    order = jnp.arange(N_EXPERTS).reshape(N_EXPERT_GROUPS, EXPERTS_PER_GROUP).T.reshape(-1)
    w = p["w_router"].astype(F32)[:, order]
    w = jnp.concatenate([w, jnp.zeros((D_MODEL, LANES - N_EXPERTS), F32)], axis=1)
    w_hi = w.astype(BF16)
    w_lo = (w - w_hi.astype(F32)).astype(BF16)
    bias_col = p["router_bias"].astype(F32)[order].reshape(N_EXPERTS, 1)
    return jnp.stack([w_hi, w_lo]), bias_col


def kernel(x, p, w_in, b_forget, conv_w, conv_b, dt_bias, a_log, d_skip, ssd_norm_w,
           w_out, ln1_g, ln1_b, w_router, router_bias, w_gate, w_up, w_down, w_ple,
           w_ple_gate, ln2_g, ln2_b):
    prm = dict(w_in=w_in, b_forget=b_forget, conv_w=conv_w, conv_b=conv_b,
               dt_bias=dt_bias, a_log=a_log, d_skip=d_skip, ssd_norm_w=ssd_norm_w,
               w_out=w_out, ln1_g=ln1_g, ln1_b=ln1_b, w_router=w_router,
               router_bias=router_bias, w_gate=w_gate, w_up=w_up, w_down=w_down,
               w_ple=w_ple, w_ple_gate=w_ple_gate, ln2_g=ln2_g, ln2_b=ln2_b)
    batch = x.shape[0]
    w_router2, bias_col = _prep_router(prm)
    outs = []
    for bi in range(batch):
        h = x[bi]
        h_bf = h.astype(BF16)
        for layer in range(w_in.shape[0]):
            h, h_bf, *routing = _mixer_layer(h, h_bf, prm, layer, w_router2, bias_col)
            h, h_bf = _channel_layer(h, h_bf, routing, p, layer, bi, prm)
        outs.append(h)
    return jnp.stack(outs)
```
